```python
import jax, jax.numpy as jnp
from jax import lax
import numpy as np

D_MODEL = 2048
BATCH = 1
SEQ = 16384
DEPTH = 2
DEC_BATCH = 8
DEC_SEQ = 16
PAST_LEN = 2048

CHUNK = 64
N_LAYERS_A = DEPTH // 2
N_LAYERS_B = DEPTH - N_LAYERS_A
GLA_HEADS = 4
GLA_DK = D_MODEL // 2
GLA_DV = D_MODEL
GLA_HK = GLA_DK // GLA_HEADS
GLA_HV = GLA_DV // GLA_HEADS
GLA_GATE_RANK = 16
GLA_GATE_NORMALIZER = 16.0
GLA_BLOCK = CHUNK
GLA_IN = 2 * GLA_DK + 2 * GLA_DV + GLA_GATE_RANK
ATT_HEADS = 16
ATT_HD = D_MODEL // ATT_HEADS
PAST_CHUNKS = 8
BAND_PAST = PAST_CHUNKS * CHUNK
BAND = BAND_PAST + CHUNK
MAX_REL = 256
N_REL = 2 * MAX_REL + 1
D_FF = 5504
ALPHA = (2 * DEPTH) ** 0.25
BETA = (8 * DEPTH) ** -0.25
ADA_SCALE = 0.5
LN_EPS = 1e-5
RMS_EPS = 1e-6
NEG = -1e30

kernel_name = 'streaming_gla_chunkband_yoco_trunk'


def layer_norm(x, g, b):
    xf = x.astype(jnp.float32)
    mu = jnp.mean(xf, -1, keepdims=True)
    var = jnp.mean(jnp.square(xf - mu), -1, keepdims=True)
    return ((xf - mu) * lax.rsqrt(var + LN_EPS)).astype(x.dtype) * g + b


def modulate(x, shift, scale):
    return x * (1 + scale[:, None, :]) + shift[:, None, :]


def post_norm(x, y, gate, g, b):
    return layer_norm(ALPHA * x + (1 + gate)[:, None, :] * y, g, b)


def swiglu(h, w_up, w_down):
    a, u = jnp.split(h @ w_up, 2, axis=-1)
    return (jax.nn.silu(a) * u) @ w_down


def gla_chunked(q, k, v, log_a, s0):
    B, T, H, K = q.shape
    V = v.shape[-1]
    L = GLA_BLOCK
    pad = (-T) % L
    f32 = jnp.float32
    if pad:
        pw = ((0, 0), (0, pad), (0, 0), (0, 0))
        q, k, v, log_a = (jnp.pad(t, pw) for t in (q, k, v, log_a))
    N = (T + pad) // L
    qf = q.astype(f32).reshape(B, N, L, H, K)
    kf = k.astype(f32).reshape(B, N, L, H, K)
    vf = v.astype(f32).reshape(B, N, L, H, V)
    b = jnp.cumsum(log_a.astype(f32).reshape(B, N, L, H, K), axis=2)
    b_last = b[:, :, -1:]
    q_t = qf * jnp.exp(b)
    k_t = kf * jnp.exp(-b)
    k_dec = kf * jnp.exp(b_last - b)
    causal = jnp.tril(jnp.ones((L, L), bool))
    scores = jnp.where(causal, jnp.einsum('bnihk,bnjhk->bnhij', q_t, k_t), 0.0)
    o_intra = jnp.einsum('bnhij,bnjhv->bnihv', scores, vf)
    decay = jnp.exp(b_last[:, :, 0])

    def step(S, xs):
        q_n, kd_n, v_n, dec_n = xs
        o = jnp.einsum('blhk,bhkv->blhv', q_n, S)
        S = dec_n[..., None] * S + jnp.einsum('blhk,blhv->bhkv', kd_n, v_n)
        return S, o

    xs = (jnp.moveaxis(q_t, 1, 0), jnp.moveaxis(k_dec, 1, 0), jnp.moveaxis(vf, 1, 0), jnp.moveaxis(decay, 1, 0))
    S, o_inter = lax.scan(step, s0.astype(f32), xs)
    o = (o_intra + jnp.moveaxis(o_inter, 0, 1)).reshape(B, N * L, H, V)[:, :T]
    return o.astype(v.dtype), S.astype(s0.dtype)


def gla_mixer(h, s0, w_in, w_gk2, b_gk, g_norm, w_out):
    B, T, _ = h.shape
    proj = h @ w_in
    cuts = [GLA_DK, 2 * GLA_DK, 2 * GLA_DK + GLA_DV, 2 * GLA_DK + 2 * GLA_DV]
    q, k, v, g, gk_low = jnp.split(proj, cuts, axis=-1)
    log_a = jax.nn.log_sigmoid((gk_low @ w_gk2 + b_gk).astype(jnp.float32)) / GLA_GATE_NORMALIZER
    q = q.reshape(B, T, GLA_HEADS, GLA_HK) * (GLA_HK ** -0.5)
    k = k.reshape(B, T, GLA_HEADS, GLA_HK)
    v = v.reshape(B, T, GLA_HEADS, GLA_HV)
    log_a = log_a.reshape(B, T, GLA_HEADS, GLA_HK)
    o, s_new = gla_chunked(q, k, v, log_a, s0)
    of = o.astype(jnp.float32)
    on = (of * lax.rsqrt(jnp.mean(jnp.square(of), -1, keepdims=True) + RMS_EPS)).astype(h.dtype) * g_norm
    o = on * jax.nn.silu(g.reshape(B, T, GLA_HEADS, GLA_HV))
    return o.reshape(B, T, GLA_DV) @ w_out, s_new


def rel_bias_matrix(table, n_q, n_k, offset):
    dist = offset + jnp.arange(n_q)[:, None] - jnp.arange(n_k)[None, :]
    idx = jnp.clip(dist, -MAX_REL, MAX_REL) + MAX_REL
    return table[:, idx].astype(jnp.float32)


def attend(q, k, v, bias, valid=None):
    s = jnp.einsum('bihd,bjhd->bhij', q, k).astype(jnp.float32) * (ATT_HD ** -0.5) + bias
    if valid is not None:
        s = jnp.where(valid, s, NEG)
    p = jax.nn.softmax(s, axis=-1).astype(v.dtype)
    return jnp.einsum('bhij,bjhd->bihd', p, v)


def band_attention_prompt(q, k, v, table):
    B, T, H, Dh = q.shape
    n_chunks = T // CHUNK
    pw = ((0, 0), (BAND_PAST, 0), (0, 0), (0, 0))
    kp, vp = jnp.pad(k, pw), jnp.pad(v, pw)
    bias = rel_bias_matrix(table, CHUNK, BAND, BAND_PAST)
    key_local = jnp.arange(BAND)

    def one_chunk(n):
        start = n * CHUNK
        qc = lax.dynamic_slice_in_dim(q, start, CHUNK, axis=1)
        kc = lax.dynamic_slice_in_dim(kp, start, BAND, axis=1)
        vc = lax.dynamic_slice_in_dim(vp, start, BAND, axis=1)
        valid = (start - BAND_PAST + key_local) >= 0
        return attend(qc, kc, vc, bias, valid)

    o = lax.map(one_chunk, jnp.arange(n_chunks))
    return jnp.moveaxis(o, 0, 1).reshape(B, T, H, Dh)


def band_attention_sample(q, k_new, v_new, past_k, past_v, table):
    n_past = past_k.shape[1]
    S = q.shape[1]
    k = jnp.concatenate([past_k, k_new], axis=1)
    v = jnp.concatenate([past_v, v_new], axis=1)
    bias = rel_bias_matrix(table, S, n_past + S, n_past)
    return attend(q, k, v, bias)


def shared_kv(x, cs, w_ada_kv, b_ada_kv, w_kv):
    B, T, _ = x.shape
    shift, scale = jnp.split(cs @ w_ada_kv + b_ada_kv, 2, axis=-1)
    k, v = jnp.split(modulate(x, shift, scale) @ w_kv, 2, axis=-1)
    return k.reshape(B, T, ATT_HEADS, ATT_HD), v.reshape(B, T, ATT_HEADS, ATT_HD)


def trunk(x, c, gla_s0, past_k, past_v, w_ada, b_ada, ln_g, ln_b, w_ffn_up, w_ffn_down,
          w_a_in, w_a_gk2, b_a_gk, g_a_norm, w_a_out, w_ada_kv, b_ada_kv, w_kv,
          w_b_q, rel_bias, w_b_out):
    B, T, D = x.shape
    cs = jax.nn.silu(c)
    gla_states = []
    k_sh = v_sh = None
    for l in range(DEPTH):
        if l == N_LAYERS_A:
            k_sh, v_sh = shared_kv(x, cs, w_ada_kv, b_ada_kv, w_kv)
        ada = (cs @ w_ada[l] + b_ada[l]).reshape(B, 3, 3, D)
        y = 0.5 * swiglu(modulate(x, ada[:, 0, 0], ada[:, 0, 1]), w_ffn_up[l, 0], w_ffn_down[l, 0])
        x = post_norm(x, y, ada[:, 0, 2], ln_g[l, 0], ln_b[l, 0])
        h = modulate(x, ada[:, 1, 0], ada[:, 1, 1])
        if l < N_LAYERS_A:
            y, s_new = gla_mixer(h, gla_s0[l], w_a_in[l], w_a_gk2[l], b_a_gk[l], g_a_norm[l], w_a_out[l])
            gla_states.append(s_new)
        else:
            j = l - N_LAYERS_A
            q = (h @ w_b_q[j]).reshape(B, T, ATT_HEADS, ATT_HD)
            if past_k is None:
                o = band_attention_prompt(q, k_sh, v_sh, rel_bias[j])
            else:
                o = band_attention_sample(q, k_sh, v_sh, past_k, past_v, rel_bias[j])
            y = o.reshape(B, T, D) @ w_b_out[j]
        x = post_norm(x, y, ada[:, 1, 2], ln_g[l, 1], ln_b[l, 1])
        y = 0.5 * swiglu(modulate(x, ada[:, 2, 0], ada[:, 2, 1]), w_ffn_up[l, 1], w_ffn_down[l, 1])
        x = post_norm(x, y, ada[:, 2, 2], ln_g[l, 2], ln_b[l, 2])
    return x, jnp.stack(gla_states), k_sh, v_sh


def setup_inputs(seed: int = 0) -> dict:
    key = jax.random.key(seed)
    ks = iter(jax.random.split(key, 40))

    def nrm(shape, s=1.0):
        return jax.random.normal(next(ks), shape, jnp.float32) * s

    D = D_MODEL
    n_band = min(BAND_PAST, PAST_LEN)
    inv_d = D ** -0.5
    w_a_in = jnp.concatenate([
        nrm((N_LAYERS_A, D, 2 * GLA_DK), inv_d),
        nrm((N_LAYERS_A, D, GLA_DV), inv_d * BETA),
        nrm((N_LAYERS_A, D, GLA_DV + GLA_GATE_RANK), inv_d)], axis=-1)
    w_kv = jnp.concatenate([nrm((D, D), inv_d), nrm((D, D), inv_d * BETA)], axis=-1)
    return {
        'x_prompt': nrm((BATCH, SEQ, D)),
        'x_sample': nrm((DEC_BATCH, DEC_SEQ, D)),
        'state_gla': nrm((N_LAYERS_A, DEC_BATCH, GLA_HEADS, GLA_HK, GLA_HV)),
        'cache_band_k': nrm((DEC_BATCH, n_band, ATT_HEADS, ATT_HD)),
        'cache_band_v': nrm((DEC_BATCH, n_band, ATT_HEADS, ATT_HD)),
        'c_prompt': nrm((BATCH, D)),
        'c_sample': nrm((DEC_BATCH, D)),
        'w_ada': nrm((DEPTH, D, 9 * D), ADA_SCALE * inv_d),
        'b_ada': nrm((DEPTH, 9 * D), 0.02),
        'ln_g': 1.0 + nrm((DEPTH, 3, D), 0.02),
        'ln_b': nrm((DEPTH, 3, D), 0.02),
        'w_ffn_up': nrm((DEPTH, 2, D, 2 * D_FF), inv_d),
        'w_ffn_down': nrm((DEPTH, 2, D_FF, D), D_FF ** -0.5 * BETA),
        'w_a_in': w_a_in,
        'w_a_gk2': nrm((N_LAYERS_A, GLA_GATE_RANK, GLA_DK), GLA_GATE_RANK ** -0.5),
        'b_a_gk': nrm((N_LAYERS_A, GLA_DK), 0.1),
        'g_a_norm': 1.0 + nrm((N_LAYERS_A, GLA_HV), 0.02),
        'w_a_out': nrm((N_LAYERS_A, GLA_DV, D), GLA_DV ** -0.5 * BETA),
        'w_ada_kv': nrm((D, 2 * D), ADA_SCALE * inv_d),
        'b_ada_kv': nrm((2 * D,), 0.02),
        'w_kv': w_kv,
        'w_b_q': nrm((N_LAYERS_B, D, D), inv_d),
        'rel_bias': nrm((N_LAYERS_B, ATT_HEADS, N_REL), 0.5),
        'w_b_out': nrm((N_LAYERS_B, D, D), inv_d * BETA),
    }


def reference(x_prompt, x_sample, state_gla, cache_band_k, cache_band_v, c_prompt, c_sample,
              w_ada, b_ada, ln_g, ln_b, w_ffn_up, w_ffn_down, w_a_in, w_a_gk2, b_a_gk,
              g_a_norm, w_a_out, w_ada_kv, b_ada_kv, w_kv, w_b_q, rel_bias, w_b_out):
    gla_zero = jnp.zeros((N_LAYERS_A, x_prompt.shape[0], GLA_HEADS, GLA_HK, GLA_HV), state_gla.dtype)
    y_prompt, s_gla_p, k_p, v_p = trunk(
        x_prompt, c_prompt, gla_zero, None, None, w_ada, b_ada, ln_g, ln_b, w_ffn_up, w_ffn_down,
        w_a_in, w_a_gk2, b_a_gk, g_a_norm, w_a_out, w_ada_kv, b_ada_kv, w_kv, w_b_q, rel_bias, w_b_out)
    y_sample, s_gla_s, k_s, v_s = trunk(
        x_sample, c_sample, state_gla, cache_band_k, cache_band_v, w_ada, b_ada, ln_g, ln_b,
        w_ffn_up, w_ffn_down, w_a_in, w_a_gk2, b_a_gk, g_a_norm, w_a_out, w_ada_kv, b_ada_kv,
        w_kv, w_b_q, rel_bias, w_b_out)
    n_keep = min(BAND_PAST, x_prompt.shape[1])
    k_p_new = k_p[:, -n_keep:]
    v_p_new = v_p[:, -n_keep:]
    return (y_prompt, y_sample, s_gla_p, s_gla_s, k_p_new, v_p_new, k_s, v_s)
```

```python
import functools

import jax
import jax.numpy as jnp
from jax import lax
from jax.experimental import pallas as pl
from jax.experimental.pallas import tpu as pltpu

F32 = jnp.float32
BF16 = jnp.bfloat16

D_MODEL = 2048
DEPTH = 2
CHUNK = 64
GLA_HEADS = 4
GLA_DK = D_MODEL // 2
GLA_DV = D_MODEL
GLA_HK = GLA_DK // GLA_HEADS
GLA_HV = GLA_DV // GLA_HEADS
GLA_GATE_RANK = 16
GLA_GATE_NORMALIZER = 16.0
GLA_MAIN = 2 * GLA_DK + 2 * GLA_DV
ATT_HEADS = 16
ATT_HD = D_MODEL // ATT_HEADS
BAND_PAST = 8 * CHUNK
MAX_REL = 256
D_FF = 5504
ALPHA = (2 * DEPTH) ** 0.25
LN_EPS = 1e-5
RMS_EPS = 1e-6
NEG = -1e30

LANE = 128
FF_TILE = 512
D_FF_PAD = -(-D_FF // FF_TILE) * FF_TILE
ROW_TILE = 512
ATT_SUB = 256
ADA_ROWS = 16
ADA_TILE = 1024
VMEM_LIMIT = 56 * 1024 * 1024


def _params(*sem):
    return pltpu.CompilerParams(dimension_semantics=sem, vmem_limit_bytes=VMEM_LIMIT)


def _layer_norm(z, g, b):
    mu = jnp.mean(z, axis=-1, keepdims=True)
    zc = z - mu
    var = jnp.mean(zc * zc, axis=-1, keepdims=True)
    return zc * lax.rsqrt(var + LN_EPS) * g + b


def _silu(a):
    return a * jax.nn.sigmoid(a)


def _ada_kernel(c_ref, w_ref, b_ref, o_ref):
    o_ref[...] = jnp.dot(_silu(c_ref[...]), w_ref[...], preferred_element_type=F32) + b_ref[...]


def _ada(c_all, w, b):
    n_l, _, n = w.shape
    return pl.pallas_call(
        _ada_kernel,
        grid=(n_l, n // ADA_TILE),
        in_specs=[
            pl.BlockSpec((ADA_ROWS, D_MODEL), lambda l, j: (0, 0)),
            pl.BlockSpec((None, D_MODEL, ADA_TILE), lambda l, j: (l, 0, j)),
            pl.BlockSpec((None, 1, ADA_TILE), lambda l, j: (l, 0, j)),
        ],
        out_specs=pl.BlockSpec((None, ADA_ROWS, ADA_TILE), lambda l, j: (l, 0, j)),
        out_shape=jax.ShapeDtypeStruct((n_l, ADA_ROWS, n), F32),
        compiler_params=_params("arbitrary", "arbitrary"),
        name="ada",
    )(c_all, w, b.reshape(n_l, 1, n))


def _mod_spec(mod_rows, tm):
    if mod_rows == 1:
        return pl.BlockSpec((1, D_MODEL), lambda i, *_: (0, 0))
    return pl.BlockSpec((tm, D_MODEL), lambda i, *_: (i, 0))


def _ffn_kernel(x_ref, sh_ref, sc_ref, gt_ref, wa_ref, wu_ref, wd_ref, g_ref, b_ref, o_ref, h_ref):
    j = pl.program_id(1)

    @pl.when(j == 0)
    def _():
        h_ref[...] = (x_ref[...] * (1.0 + sc_ref[...]) + sh_ref[...]).astype(BF16)

    h = h_ref[...]
    a = jnp.dot(h, wa_ref[...], preferred_element_type=F32)
    u = jnp.dot(h, wu_ref[...], preferred_element_type=F32)
    y = jnp.dot((_silu(a) * u).astype(BF16), wd_ref[...], preferred_element_type=F32)

    @pl.when(j == 0)
    def _():
        o_ref[...] = y

    @pl.when(j > 0)
    def _():
        o_ref[...] += y

    @pl.when(j == pl.num_programs(1) - 1)
    def _():
        z = ALPHA * x_ref[...] + (1.0 + gt_ref[...]) * (0.5 * o_ref[...])
        o_ref[...] = _layer_norm(z, g_ref[...], b_ref[...])


def _ffn(x, shift, scale, gate, w_up, w_down, ln_g, ln_b, tm):
    m = x.shape[0]
    n_ff = D_FF_PAD // FF_TILE
    mod = _mod_spec(shift.shape[0], tm)
    vec = pl.BlockSpec((1, D_MODEL), lambda i, j: (0, 0))
    return pl.pallas_call(
        _ffn_kernel,
        grid=(m // tm, n_ff),
        in_specs=[
            pl.BlockSpec((tm, D_MODEL), lambda i, j: (i, 0)),
            mod, mod, mod,
            pl.BlockSpec((D_MODEL, FF_TILE), lambda i, j: (0, j)),
            pl.BlockSpec((D_MODEL, FF_TILE), lambda i, j: (0, n_ff + j)),
            pl.BlockSpec((FF_TILE, D_MODEL), lambda i, j: (j, 0)),
            vec, vec,
        ],
        out_specs=pl.BlockSpec((tm, D_MODEL), lambda i, j: (i, 0)),
        out_shape=jax.ShapeDtypeStruct((m, D_MODEL), F32),
        scratch_shapes=[pltpu.VMEM((tm, D_MODEL), BF16)],
        compiler_params=_params("parallel", "arbitrary"),
        name="ffn",
    )(x, shift, scale, gate, w_up, w_up, w_down, ln_g, ln_b)


def _modmm_kernel(x_ref, sh_ref, sc_ref, w_ref, *rest, has_extra):
    if has_extra:
        w2_ref, o_ref, o2_ref, h_ref = rest
    else:
        o_ref, h_ref = rest
    j = pl.program_id(1)

    @pl.when(j == 0)
    def _():
        h = (x_ref[...] * (1.0 + sc_ref[...]) + sh_ref[...]).astype(BF16)
        h_ref[...] = h
        if has_extra:
            o2_ref[...] = jnp.dot(h, w2_ref[...], preferred_element_type=F32).astype(o2_ref.dtype)

    o_ref[...] = jnp.dot(h_ref[...], w_ref[...], preferred_element_type=F32).astype(o_ref.dtype)


def _modmm(x, shift, scale, w, tm, tn, out_dtype, w_extra=None):
    m = x.shape[0]
    n = w.shape[1]
    mod = _mod_spec(shift.shape[0], tm)
    in_specs = [
        pl.BlockSpec((tm, D_MODEL), lambda i, j: (i, 0)),
        mod, mod,
        pl.BlockSpec((D_MODEL, tn), lambda i, j: (0, j)),
    ]
    out_specs = pl.BlockSpec((tm, tn), lambda i, j: (i, j))
    out_shape = jax.ShapeDtypeStruct((m, n), out_dtype)
    args = [x, shift, scale, w]
    if w_extra is not None:
        n2 = w_extra.shape[1]
        in_specs.append(pl.BlockSpec((D_MODEL, n2), lambda i, j: (0, 0)))
        out_specs = [out_specs, pl.BlockSpec((tm, n2), lambda i, j: (i, 0))]
        out_shape = [out_shape, jax.ShapeDtypeStruct((m, n2), out_dtype)]
        args.append(w_extra)
    return pl.pallas_call(
        functools.partial(_modmm_kernel, has_extra=w_extra is not None),
        grid=(m // tm, n // tn),
        in_specs=in_specs,
        out_specs=out_specs,
        out_shape=out_shape,
        scratch_shapes=[pltpu.VMEM((tm, D_MODEL), BF16)],
        compiler_params=_params("parallel", "arbitrary"),
        name="modmm",
    )(*args)


def _proj_ln_kernel(a_ref, w_ref, x_ref, gt_ref, g_ref, b_ref, o_ref):
    y = jnp.dot(a_ref[...], w_ref[...], preferred_element_type=F32)
    z = ALPHA * x_ref[...] + (1.0 + gt_ref[...]) * y
    o_ref[...] = _layer_norm(z, g_ref[...], b_ref[...])


def _proj_ln(a, w, x, gate, ln_g, ln_b, tm):
    m = x.shape[0]
    vec = pl.BlockSpec((1, D_MODEL), lambda i: (0, 0))
    return pl.pallas_call(
        _proj_ln_kernel,
        grid=(m // tm,),
        in_specs=[
            pl.BlockSpec((tm, D_MODEL), lambda i: (i, 0)),
            pl.BlockSpec((D_MODEL, D_MODEL), lambda i: (0, 0)),
            pl.BlockSpec((tm, D_MODEL), lambda i: (i, 0)),
            _mod_spec(gate.shape[0], tm),
            vec, vec,
        ],
        out_specs=pl.BlockSpec((tm, D_MODEL), lambda i: (i, 0)),
        out_shape=jax.ShapeDtypeStruct((m, D_MODEL), F32),
        compiler_params=_params("parallel"),
        name="proj_ln",
    )(a, w, x, gate, ln_g, ln_b)


def _gla_kernel(q_ref, k_ref, v_ref, g_ref, gk_ref, w2_ref, bgk_ref, gn_ref, s0_ref,
                o_ref, s_ref, *, blk, n_blk):
    @pl.when(pl.program_id(2) == 0)
    def _():
        s_ref[...] = s0_ref[...]

    causal = (lax.broadcasted_iota(jnp.int32, (blk, blk), 0)
              >= lax.broadcasted_iota(jnp.int32, (blk, blk), 1))
    tril = jnp.where(causal, 1.0, 0.0).astype(BF16)
    eye = (lax.broadcasted_iota(jnp.int32, (GLA_HK, GLA_HK), 0)
           == lax.broadcasted_iota(jnp.int32, (GLA_HK, GLA_HK), 1))
    nt = (((1,), (1,)), ((), ()))
    tn = (((0,), (0,)), ((), ()))

    for s in range(n_blk):
        rows = slice(s * blk, (s + 1) * blk)
        z = jnp.dot(gk_ref[rows, :], w2_ref[...], preferred_element_type=F32) + bgk_ref[...]
        log_a = (jnp.minimum(z, 0.0) - jnp.log1p(jnp.exp(-jnp.abs(z)))) * (1.0 / GLA_GATE_NORMALIZER)
        hi = log_a.astype(BF16)
        lo = (log_a - hi.astype(F32)).astype(BF16)
        b = (jnp.dot(tril, hi, preferred_element_type=F32)
             + jnp.dot(tril, lo, preferred_element_type=F32))
        b_last = b[blk - 1:blk, :]
        q = q_ref[rows, :].astype(F32) * (GLA_HK ** -0.5)
        k = k_ref[rows, :].astype(F32)
        v = v_ref[rows, :]
        q_t = (q * jnp.exp(b)).astype(BF16)
        k_t = (k * jnp.exp(-b)).astype(BF16)
        k_dec = (k * jnp.exp(b_last - b)).astype(BF16)
        scores = lax.dot_general(q_t, k_t, nt, preferred_element_type=F32)
        scores = jnp.where(causal, scores, 0.0).astype(BF16)
        state = s_ref[0, 0]
        o = (jnp.dot(scores, v, preferred_element_type=F32)
             + jnp.dot(q_t, state.astype(BF16), preferred_element_type=F32))
        decay = jnp.broadcast_to(jnp.exp(b_last), (GLA_HK, GLA_HK))
        decay_col = jnp.sum(jnp.where(eye, decay, 0.0), axis=1, keepdims=True)
        s_ref[0, 0] = decay_col * state + lax.dot_general(k_dec, v, tn, preferred_element_type=F32)
        on = o * lax.rsqrt(jnp.mean(o * o, axis=-1, keepdims=True) + RMS_EPS) * gn_ref[...]
        o_ref[rows, :] = (on * _silu(g_ref[rows, :].astype(F32))).astype(o_ref.dtype)


def _gla(proj, gk_low, w_gk2, b_gk, g_norm, s0, batch, seq):
    blk = min(CHUNK, seq)
    rows = min(ROW_TILE, seq)
    n_steps = seq // rows
    m = batch * seq
    k_off = GLA_DK // GLA_HK
    v_off = 2 * GLA_DK // GLA_HV
    g_off = (2 * GLA_DK + GLA_DV) // GLA_HV
    row = lambda b, h, c: b * n_steps + c
    state_spec = pl.BlockSpec((1, 1, GLA_HK, GLA_HV), lambda b, h, c: (b, h, 0, 0))
    return pl.pallas_call(
        functools.partial(_gla_kernel, blk=blk, n_blk=rows // blk),
        grid=(batch, GLA_HEADS, n_steps),
        in_specs=[
            pl.BlockSpec((rows, GLA_HK), lambda b, h, c: (row(b, h, c), h)),
            pl.BlockSpec((rows, GLA_HK), lambda b, h, c: (row(b, h, c), k_off + h)),
            pl.BlockSpec((rows, GLA_HV), lambda b, h, c: (row(b, h, c), v_off + h)),
            pl.BlockSpec((rows, GLA_HV), lambda b, h, c: (row(b, h, c), g_off + h)),
            pl.BlockSpec((rows, LANE), lambda b, h, c: (row(b, h, c), 0)),
            pl.BlockSpec((LANE, GLA_HK), lambda b, h, c: (0, h)),
            pl.BlockSpec((1, GLA_HK), lambda b, h, c: (0, h)),
            pl.BlockSpec((1, GLA_HV), lambda b, h, c: (0, 0)),
            state_spec,
        ],
        out_specs=[
            pl.BlockSpec((rows, GLA_HV), lambda b, h, c: (row(b, h, c), h)),
            state_spec,
        ],
        out_shape=[
            jax.ShapeDtypeStruct((m, GLA_DV), BF16),
            jax.ShapeDtypeStruct(s0.shape, F32),
        ],
        compiler_params=_params("parallel", "parallel", "arbitrary"),
        name="gla",
    )(proj, proj, proj, proj, gk_low, w_gk2, b_gk, g_norm, s0)


def _band_attn_kernel(q_ref, kp_ref, kc_ref, vp_ref, vc_ref, bias_ref, o_ref):
    q_start = pl.program_id(1) * BAND_PAST
    nt = (((1,), (1,)), ((), ()))
    n_keys = BAND_PAST + ATT_SUB
    for s in range(BAND_PAST // ATT_SUB):
        lo, hi = s * ATT_SUB, (s + 1) * ATT_SUB
        k = jnp.concatenate([kp_ref[lo:, :], kc_ref[:hi, :]], axis=0)
        v = jnp.concatenate([vp_ref[lo:, :], vc_ref[:hi, :]], axis=0)
        sc = lax.dot_general(q_ref[lo:hi, :], k, nt, preferred_element_type=F32)
        sc = sc * (ATT_HD ** -0.5) + bias_ref[0]
        key_pos = q_start + (lo - BAND_PAST) + lax.broadcasted_iota(jnp.int32, (1, n_keys), 1)
        sc = jnp.where(key_pos >= 0, sc, NEG)
        p = jnp.exp(sc - jnp.max(sc, axis=-1, keepdims=True))
        denom = jnp.sum(p, axis=-1, keepdims=True)
        o = jnp.dot(p.astype(BF16), v, preferred_element_type=F32) / denom
        o_ref[lo:hi, :] = o.astype(o_ref.dtype)


def _band_bias(table):
    n_keys = BAND_PAST + ATT_SUB
    i = jnp.arange(ATT_SUB)[:, None]
    j = jnp.arange(n_keys)[None, :]
    idx = jnp.clip(BAND_PAST + i - j, -MAX_REL, MAX_REL) + MAX_REL
    dc = j // CHUNK - i // CHUNK
    in_band = (dc >= 0) & (dc <= BAND_PAST // CHUNK)
    return jnp.where(in_band[None], table[:, idx], NEG).astype(F32)


def _band_attn(q, kv, table):
    t = q.shape[0]
    n_keys = BAND_PAST + ATT_SUB
    prev = lambda i: jnp.maximum(i - 1, 0)
    blk = (BAND_PAST, ATT_HD)
    return pl.pallas_call(
        _band_attn_kernel,
        grid=(ATT_HEADS, t // BAND_PAST),
        in_specs=[
            pl.BlockSpec(blk, lambda h, i: (i, h)),
            pl.BlockSpec(blk, lambda h, i: (prev(i), h)),
            pl.BlockSpec(blk, lambda h, i: (i, h)),
            pl.BlockSpec(blk, lambda h, i: (prev(i), ATT_HEADS + h)),
            pl.BlockSpec(blk, lambda h, i: (i, ATT_HEADS + h)),
            pl.BlockSpec((1, ATT_SUB, n_keys), lambda h, i: (h, 0, 0)),
        ],
        out_specs=pl.BlockSpec(blk, lambda h, i: (i, h)),
        out_shape=jax.ShapeDtypeStruct((t, D_MODEL), BF16),
        compiler_params=_params("parallel", "arbitrary"),
        name="band_attn",
    )(q, kv, kv, kv, kv, _band_bias(table))


def _step_attn_kernel(q_ref, kvn_ref, ck_ref, cv_ref, bp_ref, bn_ref, o_ref):
    nt = (((1,), (1,)), ((), ()))
    for h in range(ATT_HEADS):
        cols = slice(h * ATT_HD, (h + 1) * ATT_HD)
        q = q_ref[:, cols]
        k_new = kvn_ref[:, cols].astype(BF16)
        v_new = kvn_ref[:, D_MODEL + h * ATT_HD:D_MODEL + (h + 1) * ATT_HD].astype(BF16)
        k_past = ck_ref[0, :, cols].astype(BF16)
        v_past = cv_ref[0, :, cols].astype(BF16)
        scale = ATT_HD ** -0.5
        sp = lax.dot_general(q, k_past, nt, preferred_element_type=F32) * scale + bp_ref[h]
        sn = lax.dot_general(q, k_new, nt, preferred_element_type=F32) * scale + bn_ref[h]
        mx = jnp.maximum(jnp.max(sp, axis=-1, keepdims=True), jnp.max(sn, axis=-1, keepdims=True))
        pp = jnp.exp(sp - mx)
        pn = jnp.exp(sn - mx)
        denom = jnp.sum(pp, axis=-1, keepdims=True) + jnp.sum(pn, axis=-1, keepdims=True)
        o = (jnp.dot(pp.astype(BF16), v_past, preferred_element_type=F32)
             + jnp.dot(pn.astype(BF16), v_new, preferred_element_type=F32)) / denom
        o_ref[:, cols] = o.astype(o_ref.dtype)


def _step_attn(q, kv_new, cache_k, cache_v, table, batch, seq):
    n_past = cache_k.shape[1]
    dist = n_past + jnp.arange(seq)[:, None] - jnp.arange(n_past + seq)[None, :]
    bias = table[:, jnp.clip(dist, -MAX_REL, MAX_REL) + MAX_REL].astype(F32)
    return pl.pallas_call(
        _step_attn_kernel,
        grid=(batch,),
        in_specs=[
            pl.BlockSpec((seq, D_MODEL), lambda b: (b, 0)),
            pl.BlockSpec((seq, 2 * D_MODEL), lambda b: (b, 0)),
            pl.BlockSpec((1, n_past, D_MODEL), lambda b: (b, 0, 0)),
            pl.BlockSpec((1, n_past, D_MODEL), lambda b: (b, 0, 0)),
            pl.BlockSpec((ATT_HEADS, seq, n_past), lambda b: (0, 0, 0)),
            pl.BlockSpec((ATT_HEADS, seq, seq), lambda b: (0, 0, 0)),
        ],
        out_specs=pl.BlockSpec((seq, D_MODEL), lambda b: (b, 0)),
        out_shape=jax.ShapeDtypeStruct((batch * seq, D_MODEL), BF16),
        compiler_params=_params("parallel"),
        name="step_attn",
    )(q, kv_new, cache_k, cache_v, bias[:, :, :n_past], bias[:, :, n_past:])


def _trunk(x, ada, ada_kv, gla_s0, past, wts):
    batch, seq, _ = x.shape
    m = batch * seq
    tm = min(ROW_TILE, m)
    x = x.reshape(m, D_MODEL)

    def rows(v):
        return v if batch == 1 else jnp.repeat(v, seq, axis=0)

    def mod(l, sub, which):
        return rows(ada[l][:, (3 * sub + which) * D_MODEL:(3 * sub + which + 1) * D_MODEL])

    def ffn(x, l, sub, idx):
        return _ffn(x, mod(l, sub, 0), mod(l, sub, 1), mod(l, sub, 2),
                    wts["ffn_up"][l, idx], wts["ffn_down"][l, idx],
                    wts["ln_g"][l, sub][None], wts["ln_b"][l, sub][None], tm)

    x = ffn(x, 0, 0, 0)
    proj, gk_low = _modmm(x, mod(0, 1, 0), mod(0, 1, 1), wts["a_in"], tm, 1024, BF16,
                          w_extra=wts["a_in_gk"])
    o, gla_state = _gla(proj, gk_low, wts["a_gk2"], wts["b_a_gk"], wts["g_a_norm"], gla_s0, batch, seq)
    x = _proj_ln(o, wts["a_out"], x, mod(0, 1, 2), wts["ln_g"][0, 1][None], wts["ln_b"][0, 1][None], tm)
    x = ffn(x, 0, 2, 1)

    kv_shift = rows(ada_kv[:, :D_MODEL])
    kv_scale = rows(ada_kv[:, D_MODEL:])
    if past is None:
        kv = _modmm(x, kv_shift, kv_scale, wts["kv"], tm, 1024, BF16)
        n_keep = min(BAND_PAST, seq)
        kv_out = _modmm(x[m - n_keep:], kv_shift, kv_scale, wts["kv"], n_keep, 1024, F32)
    else:
        kv_out = _modmm(x, kv_shift, kv_scale, wts["kv"], tm, 1024, F32)

    x = ffn(x, 1, 0, 0)
    q = _modmm(x, mod(1, 1, 0), mod(1, 1, 1), wts["b_q"], tm, 1024, BF16)
    if past is None:
        o = _band_attn(q, kv, wts["rel_bias"])
    else:
        o = _step_attn(q, kv_out, past[0], past[1], wts["rel_bias"], batch, seq)
    x = _proj_ln(o, wts["b_out"], x, mod(1, 1, 2), wts["ln_g"][1, 1][None], wts["ln_b"][1, 1][None], tm)
    x = ffn(x, 1, 2, 1)

    n_rows = kv_out.shape[0] // batch
    k_out = kv_out[:, :D_MODEL].reshape(batch, n_rows, ATT_HEADS, ATT_HD)
    v_out = kv_out[:, D_MODEL:].reshape(batch, n_rows, ATT_HEADS, ATT_HD)
    return x.reshape(batch, seq, D_MODEL), gla_state[None], k_out, v_out


def kernel(x_prompt, x_sample, state_gla, cache_band_k, cache_band_v, c_prompt, c_sample, w_ada, b_ada, ln_g, ln_b, w_ffn_up, w_ffn_down, w_a_in, w_a_gk2, b_a_gk, g_a_norm, w_a_out, w_ada_kv, b_ada_kv, w_kv, w_b_q, rel_bias, w_b_out):
    assert DEPTH == 2 and w_a_in.shape[0] == 1 and w_b_q.shape[0] == 1
    n_prompt, n_sample = x_prompt.shape[0], x_sample.shape[0]
    assert n_prompt == 1 and n_prompt + n_sample <= ADA_ROWS

    pad_ff = D_FF_PAD - D_FF
    up = w_ffn_up.reshape(DEPTH, 2, D_MODEL, 2, D_FF)
    up = jnp.pad(up.astype(BF16), ((0, 0),) * 4 + ((0, pad_ff),)).reshape(DEPTH, 2, D_MODEL, 2 * D_FF_PAD)
    down = jnp.pad(w_ffn_down.astype(BF16), ((0, 0), (0, 0), (0, pad_ff), (0, 0)))
    wts = {
        "ffn_up": up,
        "ffn_down": down,
        "a_in": w_a_in[0, :, :GLA_MAIN].astype(BF16),
        "a_in_gk": jnp.pad(w_a_in[0, :, GLA_MAIN:].astype(BF16), ((0, 0), (0, LANE - GLA_GATE_RANK))),
        "a_gk2": jnp.pad(w_a_gk2[0].astype(BF16), ((0, LANE - GLA_GATE_RANK), (0, 0))),
        "b_a_gk": b_a_gk[0][None],
        "g_a_norm": g_a_norm[0][None],
        "a_out": w_a_out[0].astype(BF16),
        "kv": w_kv.astype(BF16),
        "b_q": w_b_q[0].astype(BF16),
        "b_out": w_b_out[0].astype(BF16),
        "rel_bias": rel_bias[0],
        "ln_g": ln_g,
        "ln_b": ln_b,
    }

    c_all = jnp.concatenate([c_prompt, c_sample], axis=0)
    c_all = jnp.pad(c_all, ((0, ADA_ROWS - c_all.shape[0]), (0, 0)))
    ada = _ada(c_all, w_ada, b_ada)
    ada_kv = _ada(c_all, w_ada_kv[None], b_ada_kv[None])[0]
    p_rows = slice(0, n_prompt)
    s_rows = slice(n_prompt, n_prompt + n_sample)

    gla_zero = jnp.zeros((n_prompt, GLA_HEADS, GLA_HK, GLA_HV), state_gla.dtype)
    y_p, s_p, k_p, v_p = _trunk(x_prompt, ada[:, p_rows], ada_kv[p_rows], gla_zero, None, wts)
    n_past = cache_band_k.shape[1]
    past = (cache_band_k.reshape(n_sample, n_past, D_MODEL), cache_band_v.reshape(n_sample, n_past, D_MODEL))
    y_s, s_s, k_s, v_s = _trunk(x_sample, ada[:, s_rows], ada_kv[s_rows], state_gla[0], past, wts)
    return (y_p, y_s, s_p, s_s, k_p, v_p, k_s, v_s)
```

```python
import functools

import jax
import jax.numpy as jnp
from jax import lax
from jax.experimental import pallas as pl
from jax.experimental.pallas import tpu as pltpu

F32 = jnp.float32
BF16 = jnp.bfloat16

D_MODEL = 2048
DEPTH = 2
CHUNK = 64
GLA_HEADS = 4
GLA_DK = D_MODEL // 2
GLA_DV = D_MODEL
GLA_HK = GLA_DK // GLA_HEADS
GLA_HV = GLA_DV // GLA_HEADS
GLA_GATE_RANK = 16
GLA_GATE_NORMALIZER = 16.0
GLA_MAIN = 2 * GLA_DK + 2 * GLA_DV
ATT_HEADS = 16
ATT_HD = D_MODEL // ATT_HEADS
BAND_PAST = 8 * CHUNK
MAX_REL = 256
D_FF = 5504
ALPHA = (2 * DEPTH) ** 0.25
LN_EPS = 1e-5
RMS_EPS = 1e-6
NEG = -1e30

LANE = 128
FF_TILE = 512
D_FF_PAD = -(-D_FF // FF_TILE) * FF_TILE
ROW_TILE = 512
MODMM_COLS = 1024
ATT_SUB = 256
REL_LEN = BAND_PAST + 2 * ATT_SUB
ADA_ROWS = 16
ADA_TILE = 1024
VMEM_LIMIT = 56 * 1024 * 1024


def _params(*sem):
    return pltpu.CompilerParams(dimension_semantics=sem, vmem_limit_bytes=VMEM_LIMIT)


def _layer_norm(z, g, b):
    mu = jnp.mean(z, axis=-1, keepdims=True)
    zc = z - mu
    var = jnp.mean(zc * zc, axis=-1, keepdims=True)
    return zc * lax.rsqrt(var + LN_EPS) * g + b


def _silu(a):
    return a * jax.nn.sigmoid(a)


def _ada_kernel(c_ref, w_ref, b_ref, o_ref):
    o_ref[...] = jnp.dot(_silu(c_ref[...]), w_ref[...], preferred_element_type=F32) + b_ref[...]


def _ada(c_all, w, b):
    n_l, _, n = w.shape
    return pl.pallas_call(
        _ada_kernel,
        grid=(n_l, n // ADA_TILE),
        in_specs=[
            pl.BlockSpec((ADA_ROWS, D_MODEL), lambda l, j: (0, 0)),
            pl.BlockSpec((None, D_MODEL, ADA_TILE), lambda l, j: (l, 0, j)),
            pl.BlockSpec((None, 1, ADA_TILE), lambda l, j: (l, 0, j)),
        ],
        out_specs=pl.BlockSpec((None, ADA_ROWS, ADA_TILE), lambda l, j: (l, 0, j)),
        out_shape=jax.ShapeDtypeStruct((n_l, ADA_ROWS, n), F32),
        compiler_params=_params("arbitrary", "arbitrary"),
        name="ada",
    )(c_all, w, b.reshape(n_l, 1, n))


def _mod_spec(mod_rows, tm):
    if mod_rows == 1:
        return pl.BlockSpec((1, D_MODEL), lambda i, *_: (0, 0))
    return pl.BlockSpec((tm, D_MODEL), lambda i, *_: (i, 0))


def _ffn_kernel(x_ref, sh_ref, sc_ref, gt_ref, wa_ref, wu_ref, wd_ref, g_ref, b_ref, o_ref, h_ref):
    j = pl.program_id(1)

    @pl.when(j == 0)
    def _():
        h_ref[...] = (x_ref[...] * (1.0 + sc_ref[...]) + sh_ref[...]).astype(BF16)
        o_ref[...] = jnp.zeros_like(o_ref)

    h = h_ref[...]
    a = jnp.dot(h, wa_ref[...], preferred_element_type=F32)
    u = jnp.dot(h, wu_ref[...], preferred_element_type=F32)
    o_ref[...] += jnp.dot((_silu(a) * u).astype(BF16), wd_ref[...], preferred_element_type=F32)

    @pl.when(j == pl.num_programs(1) - 1)
    def _():
        z = ALPHA * x_ref[...] + (1.0 + gt_ref[...]) * (0.5 * o_ref[...])
        o_ref[...] = _layer_norm(z, g_ref[...], b_ref[...])


def _ffn(x, shift, scale, gate, w_up, w_down, ln_g, ln_b, tm):
    m = x.shape[0]
    n_ff = D_FF_PAD // FF_TILE
    mod = _mod_spec(shift.shape[0], tm)
    vec = pl.BlockSpec((1, D_MODEL), lambda i, j: (0, 0))
    return pl.pallas_call(
        _ffn_kernel,
        grid=(m // tm, n_ff),
        in_specs=[
            pl.BlockSpec((tm, D_MODEL), lambda i, j: (i, 0)),
            mod, mod, mod,
            pl.BlockSpec((D_MODEL, FF_TILE), lambda i, j: (0, j)),
            pl.BlockSpec((D_MODEL, FF_TILE), lambda i, j: (0, n_ff + j)),
            pl.BlockSpec((FF_TILE, D_MODEL), lambda i, j: (j, 0)),
            vec, vec,
        ],
        out_specs=pl.BlockSpec((tm, D_MODEL), lambda i, j: (i, 0)),
        out_shape=jax.ShapeDtypeStruct((m, D_MODEL), F32),
        scratch_shapes=[pltpu.VMEM((tm, D_MODEL), BF16)],
        compiler_params=_params("parallel", "arbitrary"),
        name="ffn",
    )(x, shift, scale, gate, w_up, w_up, w_down, ln_g, ln_b)


def _modmm_kernel(x_ref, sh_ref, sc_ref, w_ref, *rest, has_extra):
    if has_extra:
        w2_ref, o_ref, o2_ref, h_ref = rest
    else:
        o_ref, h_ref = rest
    h_ref[...] = (x_ref[...] * (1.0 + sc_ref[...]) + sh_ref[...]).astype(BF16)
    h = h_ref[...]
    if has_extra:
        o2_ref[...] = jnp.dot(h, w2_ref[...], preferred_element_type=F32).astype(o2_ref.dtype)
    for c in range(w_ref.shape[1] // MODMM_COLS):
        cols = slice(c * MODMM_COLS, (c + 1) * MODMM_COLS)
        o_ref[:, cols] = jnp.dot(h, w_ref[:, cols], preferred_element_type=F32).astype(o_ref.dtype)


def _modmm(x, shift, scale, w, tm, out_dtype, w_extra=None):
    m = x.shape[0]
    n = w.shape[1]
    mod = _mod_spec(shift.shape[0], tm)
    resident = functools.partial(pl.BlockSpec, index_map=lambda i: (0, 0), pipeline_mode=pl.Buffered(1))
    in_specs = [
        pl.BlockSpec((tm, D_MODEL), lambda i: (i, 0)),
        mod, mod,
        resident((D_MODEL, n)),
    ]
    out_specs = pl.BlockSpec((tm, n), lambda i: (i, 0))
    out_shape = jax.ShapeDtypeStruct((m, n), out_dtype)
    args = [x, shift, scale, w]
    if w_extra is not None:
        n2 = w_extra.shape[1]
        in_specs.append(resident((D_MODEL, n2)))
        out_specs = [out_specs, pl.BlockSpec((tm, n2), lambda i: (i, 0))]
        out_shape = [out_shape, jax.ShapeDtypeStruct((m, n2), out_dtype)]
        args.append(w_extra)
    return pl.pallas_call(
        functools.partial(_modmm_kernel, has_extra=w_extra is not None),
        grid=(m // tm,),
        in_specs=in_specs,
        out_specs=out_specs,
        out_shape=out_shape,
        scratch_shapes=[pltpu.VMEM((tm, D_MODEL), BF16)],
        compiler_params=_params("parallel"),
        name="modmm",
    )(*args)


def _proj_ln_kernel(a_ref, w_ref, x_ref, gt_ref, g_ref, b_ref, o_ref):
    y = jnp.dot(a_ref[...], w_ref[...], preferred_element_type=F32)
    z = ALPHA * x_ref[...] + (1.0 + gt_ref[...]) * y
    o_ref[...] = _layer_norm(z, g_ref[...], b_ref[...])


def _proj_ln(a, w, x, gate, ln_g, ln_b, tm):
    m = x.shape[0]
    vec = pl.BlockSpec((1, D_MODEL), lambda i: (0, 0))
    return pl.pallas_call(
        _proj_ln_kernel,
        grid=(m // tm,),
        in_specs=[
            pl.BlockSpec((tm, D_MODEL), lambda i: (i, 0)),
            pl.BlockSpec((D_MODEL, D_MODEL), lambda i: (0, 0)),
            pl.BlockSpec((tm, D_MODEL), lambda i: (i, 0)),
            _mod_spec(gate.shape[0], tm),
            vec, vec,
        ],
        out_specs=pl.BlockSpec((tm, D_MODEL), lambda i: (i, 0)),
        out_shape=jax.ShapeDtypeStruct((m, D_MODEL), F32),
        compiler_params=_params("parallel"),
        name="proj_ln",
    )(a, w, x, gate, ln_g, ln_b)


def _gla_kernel(q_ref, k_ref, v_ref, g_ref, gk_ref, w2_ref, bgk_ref, gn_ref, s0_ref,
                o_ref, s_ref, *, blk, n_blk):
    @pl.when(pl.program_id(2) == 0)
    def _():
        s_ref[...] = s0_ref[...]

    causal = (lax.broadcasted_iota(jnp.int32, (blk, blk), 0)
              >= lax.broadcasted_iota(jnp.int32, (blk, blk), 1))
    tril = jnp.where(causal, 1.0, 0.0).astype(BF16)
    eye = (lax.broadcasted_iota(jnp.int32, (GLA_HK, GLA_HK), 0)
           == lax.broadcasted_iota(jnp.int32, (GLA_HK, GLA_HK), 1))
    nt = (((1,), (1,)), ((), ()))
    tn = (((0,), (0,)), ((), ()))

    for s in range(n_blk):
        rows = slice(s * blk, (s + 1) * blk)
        z = jnp.dot(gk_ref[rows, :], w2_ref[...], preferred_element_type=F32) + bgk_ref[...]
        log_a = (jnp.minimum(z, 0.0) - jnp.log1p(jnp.exp(-jnp.abs(z)))) * (1.0 / GLA_GATE_NORMALIZER)
        hi = log_a.astype(BF16)
        lo = (log_a - hi.astype(F32)).astype(BF16)
        b = (jnp.dot(tril, hi, preferred_element_type=F32)
             + jnp.dot(tril, lo, preferred_element_type=F32))
        b_last = b[blk - 1:blk, :]
        q = q_ref[rows, :].astype(F32) * (GLA_HK ** -0.5)
        k = k_ref[rows, :].astype(F32)
        v = v_ref[rows, :]
        q_t = (q * jnp.exp(b)).astype(BF16)
        k_t = (k * jnp.exp(-b)).astype(BF16)
        k_dec = (k * jnp.exp(b_last - b)).astype(BF16)
        scores = lax.dot_general(q_t, k_t, nt, preferred_element_type=F32)
        scores = jnp.where(causal, scores, 0.0).astype(BF16)
        state = s_ref[0, 0]
        o = (jnp.dot(scores, v, preferred_element_type=F32)
             + jnp.dot(q_t, state.astype(BF16), preferred_element_type=F32))
        decay = jnp.broadcast_to(jnp.exp(b_last), (GLA_HK, GLA_HK))
        decay_col = jnp.sum(jnp.where(eye, decay, 0.0), axis=1, keepdims=True)
        s_ref[0, 0] = decay_col * state + lax.dot_general(k_dec, v, tn, preferred_element_type=F32)
        on = o * lax.rsqrt(jnp.mean(o * o, axis=-1, keepdims=True) + RMS_EPS) * gn_ref[...]
        o_ref[rows, :] = (on * _silu(g_ref[rows, :].astype(F32))).astype(o_ref.dtype)


def _gla(proj, gk_low, w_gk2, b_gk, g_norm, s0, batch, seq):
    blk = min(CHUNK, seq)
    rows = min(ROW_TILE, seq)
    n_steps = seq // rows
    m = batch * seq
    k_off = GLA_DK // GLA_HK
    v_off = 2 * GLA_DK // GLA_HV
    g_off = (2 * GLA_DK + GLA_DV) // GLA_HV
    row = lambda b, h, c: b * n_steps + c
    state_spec = pl.BlockSpec((1, 1, GLA_HK, GLA_HV), lambda b, h, c: (b, h, 0, 0))
    return pl.pallas_call(
        functools.partial(_gla_kernel, blk=blk, n_blk=rows // blk),
        grid=(batch, GLA_HEADS, n_steps),
        in_specs=[
            pl.BlockSpec((rows, GLA_HK), lambda b, h, c: (row(b, h, c), h)),
            pl.BlockSpec((rows, GLA_HK), lambda b, h, c: (row(b, h, c), k_off + h)),
            pl.BlockSpec((rows, GLA_HV), lambda b, h, c: (row(b, h, c), v_off + h)),
            pl.BlockSpec((rows, GLA_HV), lambda b, h, c: (row(b, h, c), g_off + h)),
            pl.BlockSpec((rows, LANE), lambda b, h, c: (row(b, h, c), 0)),
            pl.BlockSpec((LANE, GLA_HK), lambda b, h, c: (0, h)),
            pl.BlockSpec((1, GLA_HK), lambda b, h, c: (0, h)),
            pl.BlockSpec((1, GLA_HV), lambda b, h, c: (0, 0)),
            state_spec,
        ],
        out_specs=[
            pl.BlockSpec((rows, GLA_HV), lambda b, h, c: (row(b, h, c), h)),
            state_spec,
        ],
        out_shape=[
            jax.ShapeDtypeStruct((m, GLA_DV), BF16),
            jax.ShapeDtypeStruct(s0.shape, F32),
        ],
        compiler_params=_params("parallel", "parallel", "arbitrary"),
        name="gla",
    )(proj, proj, proj, proj, gk_low, w_gk2, b_gk, g_norm, s0)


def _band_attn_kernel(q_ref, kp_ref, kc_ref, vp_ref, vc_ref, rel_ref, o_ref, bias_ref):
    q_start = pl.program_id(1) * BAND_PAST
    nt = (((1,), (1,)), ((), ()))
    n_keys = BAND_PAST + ATT_SUB

    @pl.when(pl.program_id(1) == 0)
    def _():
        rel = jnp.broadcast_to(rel_ref[0], (ATT_SUB, REL_LEN))
        toeplitz = pltpu.roll(rel, 0, 1, stride=1, stride_axis=0)[:, ATT_SUB:]
        qc = lax.broadcasted_iota(jnp.int32, (ATT_SUB, n_keys), 0) // CHUNK
        kc = lax.broadcasted_iota(jnp.int32, (ATT_SUB, n_keys), 1) // CHUNK
        in_band = (kc >= qc) & (kc <= qc + BAND_PAST // CHUNK)
        bias_ref[0] = jnp.where(in_band, toeplitz, NEG)

    for s in range(BAND_PAST // ATT_SUB):
        lo, hi = s * ATT_SUB, (s + 1) * ATT_SUB
        k = jnp.concatenate([kp_ref[lo:, :], kc_ref[:hi, :]], axis=0)
        v = jnp.concatenate([vp_ref[lo:, :], vc_ref[:hi, :]], axis=0)
        sc = lax.dot_general(q_ref[lo:hi, :], k, nt, preferred_element_type=F32)
        sc = sc * (ATT_HD ** -0.5) + bias_ref[0]
        key_pos = q_start + (lo - BAND_PAST) + lax.broadcasted_iota(jnp.int32, (1, n_keys), 1)
        sc = jnp.where(key_pos >= 0, sc, NEG)
        p = jnp.exp(sc - jnp.max(sc, axis=-1, keepdims=True))
        denom = jnp.sum(p, axis=-1, keepdims=True)
        o = jnp.dot(p.astype(BF16), v, preferred_element_type=F32) / denom
        o_ref[lo:hi, :] = o.astype(o_ref.dtype)


def _band_rel(table):
    assert BAND_PAST + ATT_SUB - REL_LEN // 2 == MAX_REL and REL_LEN // 2 <= 2 * MAX_REL
    far = jnp.broadcast_to(table[:, 2 * MAX_REL:], (ATT_HEADS, REL_LEN // 2))
    near = table[:, 2 * MAX_REL - REL_LEN // 2 + 1:][:, ::-1]
    return jnp.concatenate([far, near], axis=1).astype(F32)[:, None, :]


def _band_attn(q, kv, table):
    t = q.shape[0]
    n_keys = BAND_PAST + ATT_SUB
    prev = lambda i: jnp.maximum(i - 1, 0)
    blk = (BAND_PAST, ATT_HD)
    return pl.pallas_call(
        _band_attn_kernel,
        grid=(ATT_HEADS, t // BAND_PAST),
        in_specs=[
            pl.BlockSpec(blk, lambda h, i: (i, h)),
            pl.BlockSpec(blk, lambda h, i: (prev(i), h)),
            pl.BlockSpec(blk, lambda h, i: (i, h)),
            pl.BlockSpec(blk, lambda h, i: (prev(i), ATT_HEADS + h)),
            pl.BlockSpec(blk, lambda h, i: (i, ATT_HEADS + h)),
            pl.BlockSpec((1, 1, REL_LEN), lambda h, i: (h, 0, 0)),
        ],
        out_specs=pl.BlockSpec(blk, lambda h, i: (i, h)),
        out_shape=jax.ShapeDtypeStruct((t, D_MODEL), BF16),
        scratch_shapes=[pltpu.VMEM((1, ATT_SUB, n_keys), F32)],
        compiler_params=_params("parallel", "arbitrary"),
        name="band_attn",
    )(q, kv, kv, kv, kv, _band_rel(table))


def _step_attn_kernel(q_ref, kvn_ref, ck_ref, cv_ref, bp_ref, bn_ref, o_ref):
    nt = (((1,), (1,)), ((), ()))
    for h in range(ATT_HEADS):
        cols = slice(h * ATT_HD, (h + 1) * ATT_HD)
        q = q_ref[:, cols]
        k_new = kvn_ref[:, cols].astype(BF16)
        v_new = kvn_ref[:, D_MODEL + h * ATT_HD:D_MODEL + (h + 1) * ATT_HD].astype(BF16)
        k_past = ck_ref[0, :, cols].astype(BF16)
        v_past = cv_ref[0, :, cols].astype(BF16)
        scale = ATT_HD ** -0.5
        sp = lax.dot_general(q, k_past, nt, preferred_element_type=F32) * scale + bp_ref[h]
        sn = lax.dot_general(q, k_new, nt, preferred_element_type=F32) * scale + bn_ref[h]
        mx = jnp.maximum(jnp.max(sp, axis=-1, keepdims=True), jnp.max(sn, axis=-1, keepdims=True))
        pp = jnp.exp(sp - mx)
        pn = jnp.exp(sn - mx)
        denom = jnp.sum(pp, axis=-1, keepdims=True) + jnp.sum(pn, axis=-1, keepdims=True)
        o = (jnp.dot(pp.astype(BF16), v_past, preferred_element_type=F32)
             + jnp.dot(pn.astype(BF16), v_new, preferred_element_type=F32)) / denom
        o_ref[:, cols] = o.astype(o_ref.dtype)


def _step_attn(q, kv_new, cache_k, cache_v, table, batch, seq):
    n_past = cache_k.shape[1]
    dist = n_past + jnp.arange(seq)[:, None] - jnp.arange(n_past + seq)[None, :]
    bias = table[:, jnp.clip(dist, -MAX_REL, MAX_REL) + MAX_REL].astype(F32)
    return pl.pallas_call(
        _step_attn_kernel,
        grid=(batch,),
        in_specs=[
            pl.BlockSpec((seq, D_MODEL), lambda b: (b, 0)),
            pl.BlockSpec((seq, 2 * D_MODEL), lambda b: (b, 0)),
            pl.BlockSpec((1, n_past, D_MODEL), lambda b: (b, 0, 0)),
            pl.BlockSpec((1, n_past, D_MODEL), lambda b: (b, 0, 0)),
            pl.BlockSpec((ATT_HEADS, seq, n_past), lambda b: (0, 0, 0)),
            pl.BlockSpec((ATT_HEADS, seq, seq), lambda b: (0, 0, 0)),
        ],
        out_specs=pl.BlockSpec((seq, D_MODEL), lambda b: (b, 0)),
        out_shape=jax.ShapeDtypeStruct((batch * seq, D_MODEL), BF16),
        compiler_params=_params("parallel"),
        name="step_attn",
    )(q, kv_new, cache_k, cache_v, bias[:, :, :n_past], bias[:, :, n_past:])


def _trunk(x, ada, ada_kv, gla_s0, past, wts):
    batch, seq, _ = x.shape
    m = batch * seq
    tm = min(ROW_TILE, m)
    x = x.reshape(m, D_MODEL)

    def rows(v):
        return v if batch == 1 else jnp.repeat(v, seq, axis=0)

    def mod(l, sub, which):
        return rows(ada[l][:, (3 * sub + which) * D_MODEL:(3 * sub + which + 1) * D_MODEL])

    def ffn(x, l, sub, idx):
        return _ffn(x, mod(l, sub, 0), mod(l, sub, 1), mod(l, sub, 2),
                    wts["ffn_up"][l, idx], wts["ffn_down"][l, idx],
                    wts["ln_g"][l, sub][None], wts["ln_b"][l, sub][None], tm)

    x = ffn(x, 0, 0, 0)
    proj, gk_low = _modmm(x, mod(0, 1, 0), mod(0, 1, 1), wts["a_in"], tm, BF16,
                          w_extra=wts["a_in_gk"])
    o, gla_state = _gla(proj, gk_low, wts["a_gk2"], wts["b_a_gk"], wts["g_a_norm"], gla_s0, batch, seq)
    x = _proj_ln(o, wts["a_out"], x, mod(0, 1, 2), wts["ln_g"][0, 1][None], wts["ln_b"][0, 1][None], tm)
    x = ffn(x, 0, 2, 1)

    kv_shift = rows(ada_kv[:, :D_MODEL])
    kv_scale = rows(ada_kv[:, D_MODEL:])
    if past is None:
        kv = _modmm(x, kv_shift, kv_scale, wts["kv"], tm, BF16)
        n_keep = min(BAND_PAST, seq)
        kv_out = _modmm(x[m - n_keep:], kv_shift, kv_scale, wts["kv"], n_keep, F32)
    else:
        kv_out = _modmm(x, kv_shift, kv_scale, wts["kv"], tm, F32)

    x = ffn(x, 1, 0, 0)
    q = _modmm(x, mod(1, 1, 0), mod(1, 1, 1), wts["b_q"], tm, BF16)
    if past is None:
        o = _band_attn(q, kv, wts["rel_bias"])
    else:
        o = _step_attn(q, kv_out, past[0], past[1], wts["rel_bias"], batch, seq)
    x = _proj_ln(o, wts["b_out"], x, mod(1, 1, 2), wts["ln_g"][1, 1][None], wts["ln_b"][1, 1][None], tm)
    x = ffn(x, 1, 2, 1)

    n_rows = kv_out.shape[0] // batch
    k_out = kv_out[:, :D_MODEL].reshape(batch, n_rows, ATT_HEADS, ATT_HD)
    v_out = kv_out[:, D_MODEL:].reshape(batch, n_rows, ATT_HEADS, ATT_HD)
    return x.reshape(batch, seq, D_MODEL), gla_state[None], k_out, v_out


def kernel(x_prompt, x_sample, state_gla, cache_band_k, cache_band_v, c_prompt, c_sample, w_ada, b_ada, ln_g, ln_b, w_ffn_up, w_ffn_down, w_a_in, w_a_gk2, b_a_gk, g_a_norm, w_a_out, w_ada_kv, b_ada_kv, w_kv, w_b_q, rel_bias, w_b_out):
    assert DEPTH == 2 and w_a_in.shape[0] == 1 and w_b_q.shape[0] == 1
    n_prompt, n_sample = x_prompt.shape[0], x_sample.shape[0]
    assert n_prompt == 1 and n_prompt + n_sample <= ADA_ROWS

    pad_ff = D_FF_PAD - D_FF
    up_pad = jnp.zeros((DEPTH, 2, D_MODEL, pad_ff), BF16)
    up = jnp.concatenate([w_ffn_up[..., :D_FF].astype(BF16), up_pad,
                          w_ffn_up[..., D_FF:].astype(BF16), up_pad], axis=-1)
    down = jnp.concatenate([w_ffn_down.astype(BF16), jnp.zeros((DEPTH, 2, pad_ff, D_MODEL), BF16)], axis=2)
    wts = {
        "ffn_up": up,
        "ffn_down": down,
        "a_in": w_a_in[0, :, :GLA_MAIN].astype(BF16),
        "a_in_gk": jnp.pad(w_a_in[0, :, GLA_MAIN:].astype(BF16), ((0, 0), (0, LANE - GLA_GATE_RANK))),
        "a_gk2": jnp.pad(w_a_gk2[0].astype(BF16), ((0, LANE - GLA_GATE_RANK), (0, 0))),
        "b_a_gk": b_a_gk[0][None],
        "g_a_norm": g_a_norm[0][None],
        "a_out": w_a_out[0].astype(BF16),
        "kv": w_kv.astype(BF16),
        "b_q": w_b_q[0].astype(BF16),
        "b_out": w_b_out[0].astype(BF16),
        "rel_bias": rel_bias[0],
        "ln_g": ln_g,
        "ln_b": ln_b,
    }

    c_all = jnp.concatenate([c_prompt, c_sample], axis=0)
    c_all = jnp.pad(c_all, ((0, ADA_ROWS - c_all.shape[0]), (0, 0)))
    ada = _ada(c_all, w_ada, b_ada)
    ada_kv = _ada(c_all, w_ada_kv[None], b_ada_kv[None])[0]
    p_rows = slice(0, n_prompt)
    s_rows = slice(n_prompt, n_prompt + n_sample)

    gla_zero = jnp.zeros((n_prompt, GLA_HEADS, GLA_HK, GLA_HV), state_gla.dtype)
    y_p, s_p, k_p, v_p = _trunk(x_prompt, ada[:, p_rows], ada_kv[p_rows], gla_zero, None, wts)
    n_past = cache_band_k.shape[1]
    past = (cache_band_k.reshape(n_sample, n_past, D_MODEL), cache_band_v.reshape(n_sample, n_past, D_MODEL))
    y_s, s_s, k_s, v_s = _trunk(x_sample, ada[:, s_rows], ada_kv[s_rows], state_gla[0], past, wts)
    return (y_p, y_s, s_p, s_s, k_p, v_p, k_s, v_s)
```

```python
import functools

import jax
import jax.numpy as jnp
from jax import lax
from jax.experimental import pallas as pl
from jax.experimental.pallas import tpu as pltpu

F32 = jnp.float32
BF16 = jnp.bfloat16

D_MODEL = 2048
DEPTH = 2
CHUNK = 64
GLA_HEADS = 4
GLA_DK = D_MODEL // 2
GLA_DV = D_MODEL
GLA_HK = GLA_DK // GLA_HEADS
GLA_HV = GLA_DV // GLA_HEADS
GLA_GATE_RANK = 16
GLA_GATE_NORMALIZER = 16.0
GLA_MAIN = 2 * GLA_DK + 2 * GLA_DV
ATT_HEADS = 16
ATT_HD = D_MODEL // ATT_HEADS
BAND_PAST = 8 * CHUNK
MAX_REL = 256
D_FF = 5504
ALPHA = (2 * DEPTH) ** 0.25
LN_EPS = 1e-5
RMS_EPS = 1e-6
NEG = -1e30
LOG2E = 1.4426950408889634

LANE = 128
FF_TILE = 512
D_FF_PAD = -(-D_FF // FF_TILE) * FF_TILE
ROW_TILE = 512
MODMM_COLS = 1024
ATT_SUB = 256
ATT_HEADS_PER_STEP = 2
REL_LEN = BAND_PAST + 2 * ATT_SUB
ADA_ROWS = 16
ADA_TILE = 1024
VMEM_LIMIT = 56 * 1024 * 1024


def _params(*sem):
    return pltpu.CompilerParams(dimension_semantics=sem, vmem_limit_bytes=VMEM_LIMIT)


def _layer_norm(z, g, b):
    mu = jnp.mean(z, axis=-1, keepdims=True)
    zc = z - mu
    var = jnp.mean(zc * zc, axis=-1, keepdims=True)
    return zc * lax.rsqrt(var + LN_EPS) * g + b


def _silu(a):
    return a * jax.nn.sigmoid(a)


def _ada_kernel(c_ref, w_ref, b_ref, o_ref):
    o_ref[...] = jnp.dot(_silu(c_ref[...]), w_ref[...], preferred_element_type=F32) + b_ref[...]


def _ada(c_all, w, b):
    n_l, _, n = w.shape
    return pl.pallas_call(
        _ada_kernel,
        grid=(n_l, n // ADA_TILE),
        in_specs=[
            pl.BlockSpec((ADA_ROWS, D_MODEL), lambda l, j: (0, 0)),
            pl.BlockSpec((None, D_MODEL, ADA_TILE), lambda l, j: (l, 0, j)),
            pl.BlockSpec((None, 1, ADA_TILE), lambda l, j: (l, 0, j)),
        ],
        out_specs=pl.BlockSpec((None, ADA_ROWS, ADA_TILE), lambda l, j: (l, 0, j)),
        out_shape=jax.ShapeDtypeStruct((n_l, ADA_ROWS, n), F32),
        compiler_params=_params("arbitrary", "arbitrary"),
        name="ada",
    )(c_all, w, b.reshape(n_l, 1, n))


def _mod_spec(mod_rows, tm):
    if mod_rows == 1:
        return pl.BlockSpec((1, D_MODEL), lambda i, *_: (0, 0))
    return pl.BlockSpec((tm, D_MODEL), lambda i, *_: (i, 0))


def _ffn_kernel(x_ref, sh_ref, sc_ref, gt_ref, wa_ref, wu_ref, wd_ref, g_ref, b_ref, o_ref, h_ref):
    j = pl.program_id(1)

    @pl.when(j == 0)
    def _():
        h_ref[...] = (x_ref[...] * (1.0 + sc_ref[...]) + sh_ref[...]).astype(BF16)
        o_ref[...] = jnp.zeros_like(o_ref)

    h = h_ref[...]
    a = jnp.dot(h, wa_ref[...], preferred_element_type=F32)
    u = jnp.dot(h, wu_ref[...], preferred_element_type=F32)
    o_ref[...] += jnp.dot((_silu(a) * u).astype(BF16), wd_ref[...], preferred_element_type=F32)

    @pl.when(j == pl.num_programs(1) - 1)
    def _():
        z = ALPHA * x_ref[...] + (1.0 + gt_ref[...]) * (0.5 * o_ref[...])
        o_ref[...] = _layer_norm(z, g_ref[...], b_ref[...])


def _ffn(x, shift, scale, gate, w_up, w_down, ln_g, ln_b, tm):
    m = x.shape[0]
    n_ff = D_FF_PAD // FF_TILE
    mod = _mod_spec(shift.shape[0], tm)
    vec = pl.BlockSpec((1, D_MODEL), lambda i, j: (0, 0))
    return pl.pallas_call(
        _ffn_kernel,
        grid=(m // tm, n_ff),
        in_specs=[
            pl.BlockSpec((tm, D_MODEL), lambda i, j: (i, 0)),
            mod, mod, mod,
            pl.BlockSpec((D_MODEL, FF_TILE), lambda i, j: (0, j)),
            pl.BlockSpec((D_MODEL, FF_TILE), lambda i, j: (0, n_ff + j)),
            pl.BlockSpec((FF_TILE, D_MODEL), lambda i, j: (j, 0)),
            vec, vec,
        ],
        out_specs=pl.BlockSpec((tm, D_MODEL), lambda i, j: (i, 0)),
        out_shape=jax.ShapeDtypeStruct((m, D_MODEL), F32),
        scratch_shapes=[pltpu.VMEM((tm, D_MODEL), BF16)],
        compiler_params=_params("parallel", "arbitrary"),
        name="ffn",
    )(x, shift, scale, gate, w_up, w_up, w_down, ln_g, ln_b)


def _modmm_kernel(x_ref, sh_ref, sc_ref, w_ref, *rest, has_extra, out_scale):
    if has_extra:
        w2_ref, o_ref, o2_ref, h_ref = rest
    else:
        o_ref, h_ref = rest
    h_ref[...] = (x_ref[...] * (1.0 + sc_ref[...]) + sh_ref[...]).astype(BF16)
    h = h_ref[...]
    if has_extra:
        o2_ref[...] = jnp.dot(h, w2_ref[...], preferred_element_type=F32).astype(o2_ref.dtype)
    for c in range(w_ref.shape[1] // MODMM_COLS):
        cols = slice(c * MODMM_COLS, (c + 1) * MODMM_COLS)
        y = jnp.dot(h, w_ref[:, cols], preferred_element_type=F32)
        if out_scale is not None:
            y = y * out_scale
        o_ref[:, cols] = y.astype(o_ref.dtype)


def _modmm(x, shift, scale, w, tm, out_dtype, w_extra=None, out_scale=None):
    m = x.shape[0]
    n = w.shape[1]
    mod = _mod_spec(shift.shape[0], tm)
    resident = functools.partial(pl.BlockSpec, index_map=lambda i: (0, 0), pipeline_mode=pl.Buffered(1))
    in_specs = [
        pl.BlockSpec((tm, D_MODEL), lambda i: (i, 0)),
        mod, mod,
        resident((D_MODEL, n)),
    ]
    out_specs = pl.BlockSpec((tm, n), lambda i: (i, 0))
    out_shape = jax.ShapeDtypeStruct((m, n), out_dtype)
    args = [x, shift, scale, w]
    if w_extra is not None:
        n2 = w_extra.shape[1]
        in_specs.append(resident((D_MODEL, n2)))
        out_specs = [out_specs, pl.BlockSpec((tm, n2), lambda i: (i, 0))]
        out_shape = [out_shape, jax.ShapeDtypeStruct((m, n2), out_dtype)]
        args.append(w_extra)
    return pl.pallas_call(
        functools.partial(_modmm_kernel, has_extra=w_extra is not None, out_scale=out_scale),
        grid=(m // tm,),
        in_specs=in_specs,
        out_specs=out_specs,
        out_shape=out_shape,
        scratch_shapes=[pltpu.VMEM((tm, D_MODEL), BF16)],
        compiler_params=_params("parallel"),
        name="modmm",
    )(*args)


def _proj_ln_kernel(a_ref, w_ref, x_ref, gt_ref, g_ref, b_ref, o_ref):
    y = jnp.dot(a_ref[...], w_ref[...], preferred_element_type=F32)
    z = ALPHA * x_ref[...] + (1.0 + gt_ref[...]) * y
    o_ref[...] = _layer_norm(z, g_ref[...], b_ref[...])


def _proj_ln(a, w, x, gate, ln_g, ln_b, tm):
    m = x.shape[0]
    vec = pl.BlockSpec((1, D_MODEL), lambda i: (0, 0))
    return pl.pallas_call(
        _proj_ln_kernel,
        grid=(m // tm,),
        in_specs=[
            pl.BlockSpec((tm, D_MODEL), lambda i: (i, 0)),
            pl.BlockSpec((D_MODEL, D_MODEL), lambda i: (0, 0)),
            pl.BlockSpec((tm, D_MODEL), lambda i: (i, 0)),
            _mod_spec(gate.shape[0], tm),
            vec, vec,
        ],
        out_specs=pl.BlockSpec((tm, D_MODEL), lambda i: (i, 0)),
        out_shape=jax.ShapeDtypeStruct((m, D_MODEL), F32),
        compiler_params=_params("parallel"),
        name="proj_ln",
    )(a, w, x, gate, ln_g, ln_b)


def _gla_kernel(q_ref, k_ref, v_ref, g_ref, gk_ref, w2_ref, bgk_ref, gn_ref, s0_ref,
                o_ref, s_ref, *, blk, n_blk):
    @pl.when(pl.program_id(2) == 0)
    def _():
        s_ref[...] = s0_ref[...]

    causal = (lax.broadcasted_iota(jnp.int32, (blk, blk), 0)
              >= lax.broadcasted_iota(jnp.int32, (blk, blk), 1))
    tril = jnp.broadcast_to(jnp.where(causal, 1.0, 0.0).astype(BF16), (n_blk, blk, blk))
    eye = (lax.broadcasted_iota(jnp.int32, (GLA_HK, GLA_HK), 0)
           == lax.broadcasted_iota(jnp.int32, (GLA_HK, GLA_HK), 1))
    tn = (((0,), (0,)), ((), ()))
    bnn = (((2,), (1,)), ((0,), (0,)))
    bnt = (((2,), (2,)), ((0,), (0,)))

    def blocks(t):
        return t.reshape(n_blk, blk, t.shape[-1])

    z = jnp.dot(gk_ref[...], w2_ref[...], preferred_element_type=F32) + bgk_ref[...]
    log_a = blocks((jnp.minimum(z, 0.0) - jnp.log1p(jnp.exp(-jnp.abs(z)))) * (1.0 / GLA_GATE_NORMALIZER))
    hi = log_a.astype(BF16)
    lo = (log_a - hi.astype(F32)).astype(BF16)
    b = (lax.dot_general(tril, hi, bnn, preferred_element_type=F32)
         + lax.dot_general(tril, lo, bnn, preferred_element_type=F32))
    b_last = b[:, blk - 1:blk, :]
    q = blocks(q_ref[...].astype(F32)) * (GLA_HK ** -0.5)
    k = blocks(k_ref[...].astype(F32))
    v = blocks(v_ref[...])
    q_t = (q * jnp.exp(b)).astype(BF16)
    k_t = (k * jnp.exp(-b)).astype(BF16)
    k_dec = (k * jnp.exp(b_last - b)).astype(BF16)
    scores = lax.dot_general(q_t, k_t, bnt, preferred_element_type=F32)
    scores = jnp.where(causal[None], scores, 0.0).astype(BF16)
    o_intra = lax.dot_general(scores, v, bnn, preferred_element_type=F32)
    decay = jnp.exp(b_last)

    state = s_ref[0, 0]
    for c in range(n_blk):
        rows = slice(c * blk, (c + 1) * blk)
        o = o_intra[c] + jnp.dot(q_t[c], state.astype(BF16), preferred_element_type=F32)
        decay_col = jnp.sum(jnp.where(eye, jnp.broadcast_to(decay[c], (GLA_HK, GLA_HK)), 0.0),
                            axis=1, keepdims=True)
        state = decay_col * state + lax.dot_general(k_dec[c], v[c], tn, preferred_element_type=F32)
        on = o * lax.rsqrt(jnp.mean(o * o, axis=-1, keepdims=True) + RMS_EPS) * gn_ref[...]
        o_ref[rows, :] = (on * _silu(g_ref[rows, :].astype(F32))).astype(o_ref.dtype)
    s_ref[0, 0] = state


def _gla(proj, gk_low, w_gk2, b_gk, g_norm, s0, batch, seq):
    blk = min(CHUNK, seq)
    rows = min(ROW_TILE, seq)
    n_steps = seq // rows
    m = batch * seq
    k_off = GLA_DK // GLA_HK
    v_off = 2 * GLA_DK // GLA_HV
    g_off = (2 * GLA_DK + GLA_DV) // GLA_HV
    row = lambda b, h, c: b * n_steps + c
    state_spec = pl.BlockSpec((1, 1, GLA_HK, GLA_HV), lambda b, h, c: (b, h, 0, 0))
    return pl.pallas_call(
        functools.partial(_gla_kernel, blk=blk, n_blk=rows // blk),
        grid=(batch, GLA_HEADS, n_steps),
        in_specs=[
            pl.BlockSpec((rows, GLA_HK), lambda b, h, c: (row(b, h, c), h)),
            pl.BlockSpec((rows, GLA_HK), lambda b, h, c: (row(b, h, c), k_off + h)),
            pl.BlockSpec((rows, GLA_HV), lambda b, h, c: (row(b, h, c), v_off + h)),
            pl.BlockSpec((rows, GLA_HV), lambda b, h, c: (row(b, h, c), g_off + h)),
            pl.BlockSpec((rows, LANE), lambda b, h, c: (row(b, h, c), 0)),
            pl.BlockSpec((LANE, GLA_HK), lambda b, h, c: (0, h)),
            pl.BlockSpec((1, GLA_HK), lambda b, h, c: (0, h)),
            pl.BlockSpec((1, GLA_HV), lambda b, h, c: (0, 0)),
            state_spec,
        ],
        out_specs=[
            pl.BlockSpec((rows, GLA_HV), lambda b, h, c: (row(b, h, c), h)),
            state_spec,
        ],
        out_shape=[
            jax.ShapeDtypeStruct((m, GLA_DV), BF16),
            jax.ShapeDtypeStruct(s0.shape, F32),
        ],
        compiler_params=_params("parallel", "parallel", "arbitrary"),
        name="gla",
    )(proj, proj, proj, proj, gk_low, w_gk2, b_gk, g_norm, s0)


def _band_attn_kernel(q_ref, kp_ref, kc_ref, vp_ref, vc_ref, rel_ref, o_ref, bias_ref):
    i = pl.program_id(1)
    nt = (((1,), (1,)), ((), ()))
    n_keys = BAND_PAST + ATT_SUB
    n_sub = BAND_PAST // ATT_SUB

    @pl.when(i <= 1)
    def _():
        col = lax.broadcasted_iota(jnp.int32, (ATT_SUB, n_keys), 1)
        qc = lax.broadcasted_iota(jnp.int32, (ATT_SUB, n_keys), 0) // CHUNK
        kc = col // CHUNK
        in_band = (kc >= qc) & (kc <= qc + BAND_PAST // CHUNK)
        for h in range(ATT_HEADS_PER_STEP):
            rel = jnp.broadcast_to(rel_ref[h], (ATT_SUB, REL_LEN))
            toeplitz = pltpu.roll(rel, 0, 1, stride=1, stride_axis=0)[:, ATT_SUB:] * LOG2E
            for s in range(n_sub):
                key_pos = (i - 1) * BAND_PAST + s * ATT_SUB + col
                bias_ref[h, s] = jnp.where(in_band & (key_pos >= 0), toeplitz, NEG)

    units = [(h, s) for h in range(ATT_HEADS_PER_STEP) for s in range(n_sub)]

    def scores(h, s):
        lo, hi = s * ATT_SUB, (s + 1) * ATT_SUB
        cols = slice(h * ATT_HD, (h + 1) * ATT_HD)
        k = jnp.concatenate([kp_ref[lo:, cols], kc_ref[:hi, cols]], axis=0)
        return lax.dot_general(q_ref[lo:hi, cols], k, nt, preferred_element_type=F32) + bias_ref[h, s]

    def finish(h, s, sc):
        lo, hi = s * ATT_SUB, (s + 1) * ATT_SUB
        cols = slice(h * ATT_HD, (h + 1) * ATT_HD)
        v = jnp.concatenate([vp_ref[lo:, cols], vc_ref[:hi, cols]], axis=0)
        p = jnp.exp2(sc - jnp.max(sc, axis=-1, keepdims=True))
        denom = jnp.sum(p, axis=-1, keepdims=True)
        o = jnp.dot(p.astype(BF16), v, preferred_element_type=F32) / denom
        o_ref[lo:hi, cols] = o.astype(o_ref.dtype)

    sc = scores(*units[0])
    for n, unit in enumerate(units):
        sc_next = scores(*units[n + 1]) if n + 1 < len(units) else None
        finish(*unit, sc)
        sc = sc_next


def _band_rel(table):
    assert BAND_PAST + ATT_SUB - REL_LEN // 2 == MAX_REL and REL_LEN // 2 <= 2 * MAX_REL
    far = jnp.broadcast_to(table[:, 2 * MAX_REL:], (ATT_HEADS, REL_LEN // 2))
    near = table[:, 2 * MAX_REL - REL_LEN // 2 + 1:][:, ::-1]
    return jnp.concatenate([far, near], axis=1).astype(F32)[:, None, :]


def _band_attn(q, kv, table):
    t = q.shape[0]
    n_keys = BAND_PAST + ATT_SUB
    prev = lambda i: jnp.maximum(i - 1, 0)
    n_groups = ATT_HEADS // ATT_HEADS_PER_STEP
    blk = (BAND_PAST, ATT_HEADS_PER_STEP * ATT_HD)
    return pl.pallas_call(
        _band_attn_kernel,
        grid=(n_groups, t // BAND_PAST),
        in_specs=[
            pl.BlockSpec(blk, lambda h, i: (i, h)),
            pl.BlockSpec(blk, lambda h, i: (prev(i), h)),
            pl.BlockSpec(blk, lambda h, i: (i, h)),
            pl.BlockSpec(blk, lambda h, i: (prev(i), n_groups + h)),
            pl.BlockSpec(blk, lambda h, i: (i, n_groups + h)),
            pl.BlockSpec((ATT_HEADS_PER_STEP, 1, REL_LEN), lambda h, i: (h, 0, 0)),
        ],
        out_specs=pl.BlockSpec(blk, lambda h, i: (i, h)),
        out_shape=jax.ShapeDtypeStruct((t, D_MODEL), BF16),
        scratch_shapes=[pltpu.VMEM((ATT_HEADS_PER_STEP, BAND_PAST // ATT_SUB, ATT_SUB, n_keys), F32)],
        compiler_params=_params("parallel", "arbitrary"),
        name="band_attn",
    )(q, kv, kv, kv, kv, _band_rel(table))


def _step_attn_kernel(q_ref, kvn_ref, ck_ref, cv_ref, bp_ref, bn_ref, o_ref):
    nt = (((1,), (1,)), ((), ()))
    for h in range(ATT_HEADS):
        cols = slice(h * ATT_HD, (h + 1) * ATT_HD)
        q = q_ref[:, cols]
        k_new = kvn_ref[:, cols].astype(BF16)
        v_new = kvn_ref[:, D_MODEL + h * ATT_HD:D_MODEL + (h + 1) * ATT_HD].astype(BF16)
        k_past = ck_ref[0, :, cols].astype(BF16)
        v_past = cv_ref[0, :, cols].astype(BF16)
        sp = lax.dot_general(q, k_past, nt, preferred_element_type=F32) + bp_ref[h] * LOG2E
        sn = lax.dot_general(q, k_new, nt, preferred_element_type=F32) + bn_ref[h] * LOG2E
        mx = jnp.maximum(jnp.max(sp, axis=-1, keepdims=True), jnp.max(sn, axis=-1, keepdims=True))
        pp = jnp.exp2(sp - mx)
        pn = jnp.exp2(sn - mx)
        denom = jnp.sum(pp, axis=-1, keepdims=True) + jnp.sum(pn, axis=-1, keepdims=True)
        o = (jnp.dot(pp.astype(BF16), v_past, preferred_element_type=F32)
             + jnp.dot(pn.astype(BF16), v_new, preferred_element_type=F32)) / denom
        o_ref[:, cols] = o.astype(o_ref.dtype)


def _step_attn(q, kv_new, cache_k, cache_v, table, batch, seq):
    n_past = cache_k.shape[1]
    dist = n_past + jnp.arange(seq)[:, None] - jnp.arange(n_past + seq)[None, :]
    bias = table[:, jnp.clip(dist, -MAX_REL, MAX_REL) + MAX_REL].astype(F32)
    return pl.pallas_call(
        _step_attn_kernel,
        grid=(batch,),
        in_specs=[
            pl.BlockSpec((seq, D_MODEL), lambda b: (b, 0)),
            pl.BlockSpec((seq, 2 * D_MODEL), lambda b: (b, 0)),
            pl.BlockSpec((1, n_past, D_MODEL), lambda b: (b, 0, 0)),
            pl.BlockSpec((1, n_past, D_MODEL), lambda b: (b, 0, 0)),
            pl.BlockSpec((ATT_HEADS, seq, n_past), lambda b: (0, 0, 0)),
            pl.BlockSpec((ATT_HEADS, seq, seq), lambda b: (0, 0, 0)),
        ],
        out_specs=pl.BlockSpec((seq, D_MODEL), lambda b: (b, 0)),
        out_shape=jax.ShapeDtypeStruct((batch * seq, D_MODEL), BF16),
        compiler_params=_params("parallel"),
        name="step_attn",
    )(q, kv_new, cache_k, cache_v, bias[:, :, :n_past], bias[:, :, n_past:])


def _trunk(x, ada, ada_kv, gla_s0, past, wts):
    batch, seq, _ = x.shape
    m = batch * seq
    tm = min(ROW_TILE, m)
    x = x.reshape(m, D_MODEL)

    def rows(v):
        return v if batch == 1 else jnp.repeat(v, seq, axis=0)

    def mod(l, sub, which):
        return rows(ada[l][:, (3 * sub + which) * D_MODEL:(3 * sub + which + 1) * D_MODEL])

    def ffn(x, l, sub, idx):
        return _ffn(x, mod(l, sub, 0), mod(l, sub, 1), mod(l, sub, 2),
                    wts["ffn_up"][l, idx], wts["ffn_down"][l, idx],
                    wts["ln_g"][l, sub][None], wts["ln_b"][l, sub][None], tm)

    x = ffn(x, 0, 0, 0)
    proj, gk_low = _modmm(x, mod(0, 1, 0), mod(0, 1, 1), wts["a_in"], tm, BF16,
                          w_extra=wts["a_in_gk"])
    o, gla_state = _gla(proj, gk_low, wts["a_gk2"], wts["b_a_gk"], wts["g_a_norm"], gla_s0, batch, seq)
    x = _proj_ln(o, wts["a_out"], x, mod(0, 1, 2), wts["ln_g"][0, 1][None], wts["ln_b"][0, 1][None], tm)
    x = ffn(x, 0, 2, 1)

    kv_shift = rows(ada_kv[:, :D_MODEL])
    kv_scale = rows(ada_kv[:, D_MODEL:])
    if past is None:
        kv = _modmm(x, kv_shift, kv_scale, wts["kv"], tm, BF16)
        n_keep = min(BAND_PAST, seq)
        kv_out = _modmm(x[m - n_keep:], kv_shift, kv_scale, wts["kv"], n_keep, F32)
    else:
        kv_out = _modmm(x, kv_shift, kv_scale, wts["kv"], tm, F32)

    x = ffn(x, 1, 0, 0)
    q = _modmm(x, mod(1, 1, 0), mod(1, 1, 1), wts["b_q"], tm, BF16,
               out_scale=ATT_HD ** -0.5 * LOG2E)
    if past is None:
        o = _band_attn(q, kv, wts["rel_bias"])
    else:
        o = _step_attn(q, kv_out, past[0], past[1], wts["rel_bias"], batch, seq)
    x = _proj_ln(o, wts["b_out"], x, mod(1, 1, 2), wts["ln_g"][1, 1][None], wts["ln_b"][1, 1][None], tm)
    x = ffn(x, 1, 2, 1)

    n_rows = kv_out.shape[0] // batch
    k_out = kv_out[:, :D_MODEL].reshape(batch, n_rows, ATT_HEADS, ATT_HD)
    v_out = kv_out[:, D_MODEL:].reshape(batch, n_rows, ATT_HEADS, ATT_HD)
    return x.reshape(batch, seq, D_MODEL), gla_state[None], k_out, v_out


def kernel(x_prompt, x_sample, state_gla, cache_band_k, cache_band_v, c_prompt, c_sample, w_ada, b_ada, ln_g, ln_b, w_ffn_up, w_ffn_down, w_a_in, w_a_gk2, b_a_gk, g_a_norm, w_a_out, w_ada_kv, b_ada_kv, w_kv, w_b_q, rel_bias, w_b_out):
    assert DEPTH == 2 and w_a_in.shape[0] == 1 and w_b_q.shape[0] == 1
    n_prompt, n_sample = x_prompt.shape[0], x_sample.shape[0]
    assert n_prompt == 1 and n_prompt + n_sample <= ADA_ROWS

    pad_ff = D_FF_PAD - D_FF
    up_pad = jnp.zeros((DEPTH, 2, D_MODEL, pad_ff), BF16)
    up = jnp.concatenate([w_ffn_up[..., :D_FF].astype(BF16), up_pad,
                          w_ffn_up[..., D_FF:].astype(BF16), up_pad], axis=-1)
    down = jnp.concatenate([w_ffn_down.astype(BF16), jnp.zeros((DEPTH, 2, pad_ff, D_MODEL), BF16)], axis=2)
    wts = {
        "ffn_up": up,
        "ffn_down": down,
        "a_in": w_a_in[0, :, :GLA_MAIN].astype(BF16),
        "a_in_gk": jnp.pad(w_a_in[0, :, GLA_MAIN:].astype(BF16), ((0, 0), (0, LANE - GLA_GATE_RANK))),
        "a_gk2": jnp.pad(w_a_gk2[0].astype(BF16), ((0, LANE - GLA_GATE_RANK), (0, 0))),
        "b_a_gk": b_a_gk[0][None],
        "g_a_norm": g_a_norm[0][None],
        "a_out": w_a_out[0].astype(BF16),
        "kv": w_kv.astype(BF16),
        "b_q": w_b_q[0].astype(BF16),
        "b_out": w_b_out[0].astype(BF16),
        "rel_bias": rel_bias[0],
        "ln_g": ln_g,
        "ln_b": ln_b,
    }

    c_all = jnp.concatenate([c_prompt, c_sample], axis=0)
    c_all = jnp.pad(c_all, ((0, ADA_ROWS - c_all.shape[0]), (0, 0)))
    ada = _ada(c_all, w_ada, b_ada)
    ada_kv = _ada(c_all, w_ada_kv[None], b_ada_kv[None])[0]
    p_rows = slice(0, n_prompt)
    s_rows = slice(n_prompt, n_prompt + n_sample)

    gla_zero = jnp.zeros((n_prompt, GLA_HEADS, GLA_HK, GLA_HV), state_gla.dtype)
    y_p, s_p, k_p, v_p = _trunk(x_prompt, ada[:, p_rows], ada_kv[p_rows], gla_zero, None, wts)
    n_past = cache_band_k.shape[1]
    past = (cache_band_k.reshape(n_sample, n_past, D_MODEL), cache_band_v.reshape(n_sample, n_past, D_MODEL))
    y_s, s_s, k_s, v_s = _trunk(x_sample, ada[:, s_rows], ada_kv[s_rows], state_gla[0], past, wts)
    return (y_p, y_s, s_p, s_s, k_p, v_p, k_s, v_s)
```

```python
import functools

import jax
import jax.numpy as jnp
from jax import lax
from jax.experimental import pallas as pl
from jax.experimental.pallas import tpu as pltpu

F32 = jnp.float32
BF16 = jnp.bfloat16

D_MODEL = 2048
DEPTH = 2
CHUNK = 64
GLA_HEADS = 4
GLA_DK = D_MODEL // 2
GLA_DV = D_MODEL
GLA_HK = GLA_DK // GLA_HEADS
GLA_HV = GLA_DV // GLA_HEADS
GLA_GATE_RANK = 16
GLA_GATE_NORMALIZER = 16.0
GLA_MAIN = 2 * GLA_DK + 2 * GLA_DV
ATT_HEADS = 16
ATT_HD = D_MODEL // ATT_HEADS
BAND_PAST = 8 * CHUNK
MAX_REL = 256
D_FF = 5504
ALPHA = (2 * DEPTH) ** 0.25
LN_EPS = 1e-5
RMS_EPS = 1e-6
NEG = -1e30
LOG2E = 1.4426950408889634

LANE = 128
FF_TILE = 512
FF_LAST = D_FF - (D_FF - 1) // FF_TILE * FF_TILE
ROW_TILE = 512
CAST_ROWS = 256
MODMM_COLS = 1024
ATT_SUB = 256
ATT_HEADS_PER_STEP = 2
REL_LEN = BAND_PAST + 2 * ATT_SUB
ADA_ROWS = 16
ADA_TILE = 1024
VMEM_LIMIT = 56 * 1024 * 1024


def _params(*sem):
    return pltpu.CompilerParams(dimension_semantics=sem, vmem_limit_bytes=VMEM_LIMIT)


def _layer_norm(z, g, b):
    mu = jnp.mean(z, axis=-1, keepdims=True)
    zc = z - mu
    var = jnp.mean(zc * zc, axis=-1, keepdims=True)
    return zc * lax.rsqrt(var + LN_EPS) * g + b


def _silu(a):
    return a * jax.nn.sigmoid(a)


def _ada_kernel(c_ref, w_ref, b_ref, o_ref):
    o_ref[...] = jnp.dot(_silu(c_ref[...]), w_ref[...], preferred_element_type=F32) + b_ref[...]


def _ada(c_all, w, b):
    n_l, _, n = w.shape
    return pl.pallas_call(
        _ada_kernel,
        grid=(n_l, n // ADA_TILE),
        in_specs=[
            pl.BlockSpec((ADA_ROWS, D_MODEL), lambda l, j: (0, 0)),
            pl.BlockSpec((None, D_MODEL, ADA_TILE), lambda l, j: (l, 0, j)),
            pl.BlockSpec((None, 1, ADA_TILE), lambda l, j: (l, 0, j)),
        ],
        out_specs=pl.BlockSpec((None, ADA_ROWS, ADA_TILE), lambda l, j: (l, 0, j)),
        out_shape=jax.ShapeDtypeStruct((n_l, ADA_ROWS, n), F32),
        compiler_params=_params("arbitrary", "arbitrary"),
        name="ada",
    )(c_all, w, b.reshape(n_l, 1, n))


def _mod_spec(mod_rows, tm):
    if mod_rows == 1:
        return pl.BlockSpec((1, D_MODEL), lambda i, *_: (0, 0))
    return pl.BlockSpec((tm, D_MODEL), lambda i, *_: (i, 0))


def _ffn_kernel(x_ref, sh_ref, sc_ref, gt_ref, wa_ref, wu_ref, wd_ref, g_ref, b_ref, o_ref, h_ref):
    j = pl.program_id(1)
    last = pl.num_programs(1) - 1

    @pl.when(j == 0)
    def _():
        h_ref[...] = (x_ref[...] * (1.0 + sc_ref[...]) + sh_ref[...]).astype(BF16)
        o_ref[...] = jnp.zeros_like(o_ref)

    def chunk(cols):
        h = h_ref[...]
        a = jnp.dot(h, wa_ref[:, :cols], preferred_element_type=F32)
        u = jnp.dot(h, wu_ref[:, :cols], preferred_element_type=F32)
        o_ref[...] += jnp.dot((_silu(a) * u).astype(BF16), wd_ref[:cols, :], preferred_element_type=F32)

    @pl.when(j < last)
    def _():
        chunk(FF_TILE)

    @pl.when(j == last)
    def _():
        chunk(FF_LAST)
        z = ALPHA * x_ref[...] + (1.0 + gt_ref[...]) * (0.5 * o_ref[...])
        o_ref[...] = _layer_norm(z, g_ref[...], b_ref[...])


def _ffn(x, shift, scale, gate, w_a, w_u, w_down, layer, idx, ln_g, ln_b, tm):
    m = x.shape[0]
    n_ff = pl.cdiv(D_FF, FF_TILE)
    mod = _mod_spec(shift.shape[0], tm)
    vec = pl.BlockSpec((1, D_MODEL), lambda i, j: (0, 0))
    up = pl.BlockSpec((None, None, D_MODEL, FF_TILE), lambda i, j: (layer, idx, 0, j))
    return pl.pallas_call(
        _ffn_kernel,
        grid=(m // tm, n_ff),
        in_specs=[
            pl.BlockSpec((tm, D_MODEL), lambda i, j: (i, 0)),
            mod, mod, mod,
            up, up,
            pl.BlockSpec((None, None, FF_TILE, D_MODEL), lambda i, j: (layer, idx, j, 0)),
            vec, vec,
        ],
        out_specs=pl.BlockSpec((tm, D_MODEL), lambda i, j: (i, 0)),
        out_shape=jax.ShapeDtypeStruct((m, D_MODEL), F32),
        scratch_shapes=[pltpu.VMEM((tm, D_MODEL), BF16)],
        compiler_params=_params("parallel", "arbitrary"),
        name="ffn",
    )(x, shift, scale, gate, w_a, w_u, w_down, ln_g, ln_b)


def _cast_kernel(x_ref, *o_refs):
    off = 0
    for o_ref in o_refs:
        n = o_ref.shape[-1]
        o_ref[...] = x_ref[:, off:off + n].astype(o_ref.dtype)
        off += n


def _cast_split(w, widths, rows):
    r, n = w.shape
    assert r % rows == 0 and sum(widths) <= n and all(wd % LANE == 0 for wd in widths)
    out = pl.pallas_call(
        _cast_kernel,
        grid=(r // rows,),
        in_specs=[pl.BlockSpec((rows, n), lambda i: (i, 0))],
        out_specs=[pl.BlockSpec((rows, wd), lambda i: (i, 0)) for wd in widths],
        out_shape=[jax.ShapeDtypeStruct((r, wd), BF16) for wd in widths],
        compiler_params=_params("parallel"),
        name="cast",
    )(w)
    return out


def _modmm_kernel(x_ref, sh_ref, sc_ref, w_ref, *rest, has_extra, out_scale):
    if has_extra:
        w2_ref, o_ref, o2_ref, h_ref = rest
    else:
        o_ref, h_ref = rest
    h_ref[...] = (x_ref[...] * (1.0 + sc_ref[...]) + sh_ref[...]).astype(BF16)
    h = h_ref[...]
    if has_extra:
        o2_ref[...] = jnp.dot(h, w2_ref[...], preferred_element_type=F32).astype(o2_ref.dtype)
    for c in range(w_ref.shape[1] // MODMM_COLS):
        cols = slice(c * MODMM_COLS, (c + 1) * MODMM_COLS)
        y = jnp.dot(h, w_ref[:, cols], preferred_element_type=F32)
        if out_scale is not None:
            y = y * out_scale
        o_ref[:, cols] = y.astype(o_ref.dtype)


def _modmm(x, shift, scale, w, tm, out_dtype, w_extra=None, out_scale=None):
    m = x.shape[0]
    n = w.shape[1]
    mod = _mod_spec(shift.shape[0], tm)
    resident = functools.partial(pl.BlockSpec, index_map=lambda i: (0, 0), pipeline_mode=pl.Buffered(1))
    in_specs = [
        pl.BlockSpec((tm, D_MODEL), lambda i: (i, 0)),
        mod, mod,
        resident((D_MODEL, n)),
    ]
    out_specs = pl.BlockSpec((tm, n), lambda i: (i, 0))
    out_shape = jax.ShapeDtypeStruct((m, n), out_dtype)
    args = [x, shift, scale, w]
    if w_extra is not None:
        n2 = w_extra.shape[1]
        in_specs.append(resident((D_MODEL, n2)))
        out_specs = [out_specs, pl.BlockSpec((tm, n2), lambda i: (i, 0))]
        out_shape = [out_shape, jax.ShapeDtypeStruct((m, n2), out_dtype)]
        args.append(w_extra)
    return pl.pallas_call(
        functools.partial(_modmm_kernel, has_extra=w_extra is not None, out_scale=out_scale),
        grid=(m // tm,),
        in_specs=in_specs,
        out_specs=out_specs,
        out_shape=out_shape,
        scratch_shapes=[pltpu.VMEM((tm, D_MODEL), BF16)],
        compiler_params=_params("parallel"),
        name="modmm",
    )(*args)


def _proj_ln_kernel(a_ref, w_ref, x_ref, gt_ref, g_ref, b_ref, o_ref):
    y = jnp.dot(a_ref[...], w_ref[...], preferred_element_type=F32)
    z = ALPHA * x_ref[...] + (1.0 + gt_ref[...]) * y
    o_ref[...] = _layer_norm(z, g_ref[...], b_ref[...])


def _proj_ln(a, w, x, gate, ln_g, ln_b, tm):
    m = x.shape[0]
    vec = pl.BlockSpec((1, D_MODEL), lambda i: (0, 0))
    return pl.pallas_call(
        _proj_ln_kernel,
        grid=(m // tm,),
        in_specs=[
            pl.BlockSpec((tm, D_MODEL), lambda i: (i, 0)),
            pl.BlockSpec((D_MODEL, D_MODEL), lambda i: (0, 0)),
            pl.BlockSpec((tm, D_MODEL), lambda i: (i, 0)),
            _mod_spec(gate.shape[0], tm),
            vec, vec,
        ],
        out_specs=pl.BlockSpec((tm, D_MODEL), lambda i: (i, 0)),
        out_shape=jax.ShapeDtypeStruct((m, D_MODEL), F32),
        compiler_params=_params("parallel"),
        name="proj_ln",
    )(a, w, x, gate, ln_g, ln_b)


def _gla_kernel(q_ref, k_ref, v_ref, g_ref, gk_ref, w2_ref, bgk_ref, gn_ref, s0_ref,
                o_ref, s_ref, *, blk, n_blk):
    @pl.when(pl.program_id(2) == 0)
    def _():
        s_ref[...] = s0_ref[...]

    causal = (lax.broadcasted_iota(jnp.int32, (blk, blk), 0)
              >= lax.broadcasted_iota(jnp.int32, (blk, blk), 1))
    tril = jnp.broadcast_to(jnp.where(causal, 1.0, 0.0).astype(BF16), (n_blk, blk, blk))
    eye = (lax.broadcasted_iota(jnp.int32, (GLA_HK, GLA_HK), 0)
           == lax.broadcasted_iota(jnp.int32, (GLA_HK, GLA_HK), 1))
    tn = (((0,), (0,)), ((), ()))
    bnn = (((2,), (1,)), ((0,), (0,)))
    bnt = (((2,), (2,)), ((0,), (0,)))

    def blocks(t):
        return t.reshape(n_blk, blk, t.shape[-1])

    z = jnp.dot(gk_ref[...], w2_ref[...], preferred_element_type=F32) + bgk_ref[...]
    log_a = blocks((jnp.minimum(z, 0.0) - jnp.log1p(jnp.exp(-jnp.abs(z)))) * (1.0 / GLA_GATE_NORMALIZER))
    hi = log_a.astype(BF16)
    lo = (log_a - hi.astype(F32)).astype(BF16)
    b = (lax.dot_general(tril, hi, bnn, preferred_element_type=F32)
         + lax.dot_general(tril, lo, bnn, preferred_element_type=F32))
    b_last = b[:, blk - 1:blk, :]
    q = blocks(q_ref[...].astype(F32)) * (GLA_HK ** -0.5)
    k = blocks(k_ref[...].astype(F32))
    v = blocks(v_ref[...])
    q_t = (q * jnp.exp(b)).astype(BF16)
    k_t = (k * jnp.exp(-b)).astype(BF16)
    k_dec = (k * jnp.exp(b_last - b)).astype(BF16)
    scores = lax.dot_general(q_t, k_t, bnt, preferred_element_type=F32)
    scores = jnp.where(causal[None], scores, 0.0).astype(BF16)
    o_intra = lax.dot_general(scores, v, bnn, preferred_element_type=F32)
    decay = jnp.exp(b_last)

    state = s_ref[0, 0]
    for c in range(n_blk):
        rows = slice(c * blk, (c + 1) * blk)
        o = o_intra[c] + jnp.dot(q_t[c], state.astype(BF16), preferred_element_type=F32)
        decay_col = jnp.sum(jnp.where(eye, jnp.broadcast_to(decay[c], (GLA_HK, GLA_HK)), 0.0),
                            axis=1, keepdims=True)
        state = decay_col * state + lax.dot_general(k_dec[c], v[c], tn, preferred_element_type=F32)
        on = o * lax.rsqrt(jnp.mean(o * o, axis=-1, keepdims=True) + RMS_EPS) * gn_ref[...]
        o_ref[rows, :] = (on * _silu(g_ref[rows, :].astype(F32))).astype(o_ref.dtype)
    s_ref[0, 0] = state


def _gla(proj, gk_low, w_gk2, b_gk, g_norm, s0, batch, seq):
    blk = min(CHUNK, seq)
    rows = min(ROW_TILE, seq)
    n_steps = seq // rows
    m = batch * seq
    k_off = GLA_DK // GLA_HK
    v_off = 2 * GLA_DK // GLA_HV
    g_off = (2 * GLA_DK + GLA_DV) // GLA_HV
    row = lambda b, h, c: b * n_steps + c
    state_spec = pl.BlockSpec((1, 1, GLA_HK, GLA_HV), lambda b, h, c: (b, h, 0, 0))
    return pl.pallas_call(
        functools.partial(_gla_kernel, blk=blk, n_blk=rows // blk),
        grid=(batch, GLA_HEADS, n_steps),
        in_specs=[
            pl.BlockSpec((rows, GLA_HK), lambda b, h, c: (row(b, h, c), h)),
            pl.BlockSpec((rows, GLA_HK), lambda b, h, c: (row(b, h, c), k_off + h)),
            pl.BlockSpec((rows, GLA_HV), lambda b, h, c: (row(b, h, c), v_off + h)),
            pl.BlockSpec((rows, GLA_HV), lambda b, h, c: (row(b, h, c), g_off + h)),
            pl.BlockSpec((rows, LANE), lambda b, h, c: (row(b, h, c), 0)),
            pl.BlockSpec((LANE, GLA_HK), lambda b, h, c: (0, h)),
            pl.BlockSpec((1, GLA_HK), lambda b, h, c: (0, h)),
            pl.BlockSpec((1, GLA_HV), lambda b, h, c: (0, 0)),
            state_spec,
        ],
        out_specs=[
            pl.BlockSpec((rows, GLA_HV), lambda b, h, c: (row(b, h, c), h)),
            state_spec,
        ],
        out_shape=[
            jax.ShapeDtypeStruct((m, GLA_DV), BF16),
            jax.ShapeDtypeStruct(s0.shape, F32),
        ],
        compiler_params=_params("parallel", "parallel", "arbitrary"),
        name="gla",
    )(proj, proj, proj, proj, gk_low, w_gk2, b_gk, g_norm, s0)


def _band_attn_kernel(q_ref, kp_ref, kc_ref, vp_ref, vc_ref, rel_ref, o_ref, bias_ref):
    i = pl.program_id(1)
    nt = (((1,), (1,)), ((), ()))
    n_keys = BAND_PAST + ATT_SUB
    n_sub = BAND_PAST // ATT_SUB

    @pl.when(i <= 1)
    def _():
        col = lax.broadcasted_iota(jnp.int32, (ATT_SUB, n_keys), 1)
        qc = lax.broadcasted_iota(jnp.int32, (ATT_SUB, n_keys), 0) // CHUNK
        kc = col // CHUNK
        in_band = (kc >= qc) & (kc <= qc + BAND_PAST // CHUNK)
        for h in range(ATT_HEADS_PER_STEP):
            rel = jnp.broadcast_to(rel_ref[h], (ATT_SUB, REL_LEN))
            toeplitz = pltpu.roll(rel, 0, 1, stride=1, stride_axis=0)[:, ATT_SUB:] * LOG2E
            for s in range(n_sub):
                key_pos = (i - 1) * BAND_PAST + s * ATT_SUB + col
                bias_ref[h, s] = jnp.where(in_band & (key_pos >= 0), toeplitz, NEG)

    units = [(h, s) for h in range(ATT_HEADS_PER_STEP) for s in range(n_sub)]

    def scores(h, s):
        lo, hi = s * ATT_SUB, (s + 1) * ATT_SUB
        cols = slice(h * ATT_HD, (h + 1) * ATT_HD)
        k = jnp.concatenate([kp_ref[lo:, cols], kc_ref[:hi, cols]], axis=0)
        return lax.dot_general(q_ref[lo:hi, cols], k, nt, preferred_element_type=F32) + bias_ref[h, s]

    def finish(h, s, sc):
        lo, hi = s * ATT_SUB, (s + 1) * ATT_SUB
        cols = slice(h * ATT_HD, (h + 1) * ATT_HD)
        v = jnp.concatenate([vp_ref[lo:, cols], vc_ref[:hi, cols]], axis=0)
        p = jnp.exp2(sc - jnp.max(sc, axis=-1, keepdims=True))
        denom = jnp.sum(p, axis=-1, keepdims=True)
        o = jnp.dot(p.astype(BF16), v, preferred_element_type=F32) / denom
        o_ref[lo:hi, cols] = o.astype(o_ref.dtype)

    sc = scores(*units[0])
    for n, unit in enumerate(units):
        sc_next = scores(*units[n + 1]) if n + 1 < len(units) else None
        finish(*unit, sc)
        sc = sc_next


def _band_rel(table):
    assert BAND_PAST + ATT_SUB - REL_LEN // 2 == MAX_REL and REL_LEN // 2 <= 2 * MAX_REL
    far = jnp.broadcast_to(table[:, 2 * MAX_REL:], (ATT_HEADS, REL_LEN // 2))
    near = table[:, 2 * MAX_REL - REL_LEN // 2 + 1:][:, ::-1]
    return jnp.concatenate([far, near], axis=1).astype(F32)[:, None, :]


def _band_attn(q, kv, table):
    t = q.shape[0]
    n_keys = BAND_PAST + ATT_SUB
    prev = lambda i: jnp.maximum(i - 1, 0)
    n_groups = ATT_HEADS // ATT_HEADS_PER_STEP
    blk = (BAND_PAST, ATT_HEADS_PER_STEP * ATT_HD)
    return pl.pallas_call(
        _band_attn_kernel,
        grid=(n_groups, t // BAND_PAST),
        in_specs=[
            pl.BlockSpec(blk, lambda h, i: (i, h)),
            pl.BlockSpec(blk, lambda h, i: (prev(i), h)),
            pl.BlockSpec(blk, lambda h, i: (i, h)),
            pl.BlockSpec(blk, lambda h, i: (prev(i), n_groups + h)),
            pl.BlockSpec(blk, lambda h, i: (i, n_groups + h)),
            pl.BlockSpec((ATT_HEADS_PER_STEP, 1, REL_LEN), lambda h, i: (h, 0, 0)),
        ],
        out_specs=pl.BlockSpec(blk, lambda h, i: (i, h)),
        out_shape=jax.ShapeDtypeStruct((t, D_MODEL), BF16),
        scratch_shapes=[pltpu.VMEM((ATT_HEADS_PER_STEP, BAND_PAST // ATT_SUB, ATT_SUB, n_keys), F32)],
        compiler_params=_params("parallel", "arbitrary"),
        name="band_attn",
    )(q, kv, kv, kv, kv, _band_rel(table))


def _step_attn_kernel(q_ref, kvn_ref, ck_ref, cv_ref, bp_ref, bn_ref, o_ref):
    nt = (((1,), (1,)), ((), ()))
    for h in range(ATT_HEADS):
        cols = slice(h * ATT_HD, (h + 1) * ATT_HD)
        q = q_ref[:, cols]
        k_new = kvn_ref[:, cols].astype(BF16)
        v_new = kvn_ref[:, D_MODEL + h * ATT_HD:D_MODEL + (h + 1) * ATT_HD].astype(BF16)
        k_past = ck_ref[0, :, h, :].astype(BF16)
        v_past = cv_ref[0, :, h, :].astype(BF16)
        sp = lax.dot_general(q, k_past, nt, preferred_element_type=F32) + bp_ref[h] * LOG2E
        sn = lax.dot_general(q, k_new, nt, preferred_element_type=F32) + bn_ref[h] * LOG2E
        mx = jnp.maximum(jnp.max(sp, axis=-1, keepdims=True), jnp.max(sn, axis=-1, keepdims=True))
        pp = jnp.exp2(sp - mx)
        pn = jnp.exp2(sn - mx)
        denom = jnp.sum(pp, axis=-1, keepdims=True) + jnp.sum(pn, axis=-1, keepdims=True)
        o = (jnp.dot(pp.astype(BF16), v_past, preferred_element_type=F32)
             + jnp.dot(pn.astype(BF16), v_new, preferred_element_type=F32)) / denom
        o_ref[:, cols] = o.astype(o_ref.dtype)


def _step_attn(q, kv_new, cache_k, cache_v, table, batch, seq):
    n_past = cache_k.shape[1]
    dist = n_past + jnp.arange(seq)[:, None] - jnp.arange(n_past + seq)[None, :]
    bias = table[:, jnp.clip(dist, -MAX_REL, MAX_REL) + MAX_REL].astype(F32)
    return pl.pallas_call(
        _step_attn_kernel,
        grid=(batch,),
        in_specs=[
            pl.BlockSpec((seq, D_MODEL), lambda b: (b, 0)),
            pl.BlockSpec((seq, 2 * D_MODEL), lambda b: (b, 0)),
            pl.BlockSpec((1, n_past, ATT_HEADS, ATT_HD), lambda b: (b, 0, 0, 0)),
            pl.BlockSpec((1, n_past, ATT_HEADS, ATT_HD), lambda b: (b, 0, 0, 0)),
            pl.BlockSpec((ATT_HEADS, seq, n_past), lambda b: (0, 0, 0)),
            pl.BlockSpec((ATT_HEADS, seq, seq), lambda b: (0, 0, 0)),
        ],
        out_specs=pl.BlockSpec((seq, D_MODEL), lambda b: (b, 0)),
        out_shape=jax.ShapeDtypeStruct((batch * seq, D_MODEL), BF16),
        compiler_params=_params("parallel"),
        name="step_attn",
    )(q, kv_new, cache_k, cache_v, bias[:, :, :n_past], bias[:, :, n_past:])


def _trunk(x, ada, ada_kv, gla_s0, past, wts):
    batch, seq, _ = x.shape
    m = batch * seq
    tm = min(ROW_TILE, m)
    x = x.reshape(m, D_MODEL)

    def rows(v):
        return v if batch == 1 else jnp.repeat(v, seq, axis=0)

    def mod(l, sub, which):
        return rows(ada[l][:, (3 * sub + which) * D_MODEL:(3 * sub + which + 1) * D_MODEL])

    def ffn(x, l, sub, idx):
        return _ffn(x, mod(l, sub, 0), mod(l, sub, 1), mod(l, sub, 2),
                    wts["ffn_a"], wts["ffn_u"], wts["ffn_down"], l, idx,
                    wts["ln_g"][l, sub][None], wts["ln_b"][l, sub][None], tm)

    x = ffn(x, 0, 0, 0)
    proj, gk_low = _modmm(x, mod(0, 1, 0), mod(0, 1, 1), wts["a_in"], tm, BF16,
                          w_extra=wts["a_in_gk"])
    o, gla_state = _gla(proj, gk_low, wts["a_gk2"], wts["b_a_gk"], wts["g_a_norm"], gla_s0, batch, seq)
    x = _proj_ln(o, wts["a_out"], x, mod(0, 1, 2), wts["ln_g"][0, 1][None], wts["ln_b"][0, 1][None], tm)
    x = ffn(x, 0, 2, 1)

    kv_shift = rows(ada_kv[:, :D_MODEL])
    kv_scale = rows(ada_kv[:, D_MODEL:])
    if past is None:
        kv = _modmm(x, kv_shift, kv_scale, wts["kv"], tm, BF16)
        n_keep = min(BAND_PAST, seq)
        kv_out = _modmm(x[m - n_keep:], kv_shift, kv_scale, wts["kv"], n_keep, F32)
    else:
        kv_out = _modmm(x, kv_shift, kv_scale, wts["kv"], tm, F32)

    x = ffn(x, 1, 0, 0)
    q = _modmm(x, mod(1, 1, 0), mod(1, 1, 1), wts["b_q"], tm, BF16,
               out_scale=ATT_HD ** -0.5 * LOG2E)
    if past is None:
        o = _band_attn(q, kv, wts["rel_bias"])
    else:
        o = _step_attn(q, kv_out, past[0], past[1], wts["rel_bias"], batch, seq)
    x = _proj_ln(o, wts["b_out"], x, mod(1, 1, 2), wts["ln_g"][1, 1][None], wts["ln_b"][1, 1][None], tm)
    x = ffn(x, 1, 2, 1)

    n_rows = kv_out.shape[0] // batch
    k_out = kv_out[:, :D_MODEL].reshape(batch, n_rows, ATT_HEADS, ATT_HD)
    v_out = kv_out[:, D_MODEL:].reshape(batch, n_rows, ATT_HEADS, ATT_HD)
    return x.reshape(batch, seq, D_MODEL), gla_state[None], k_out, v_out


def kernel(x_prompt, x_sample, state_gla, cache_band_k, cache_band_v, c_prompt, c_sample, w_ada, b_ada, ln_g, ln_b, w_ffn_up, w_ffn_down, w_a_in, w_a_gk2, b_a_gk, g_a_norm, w_a_out, w_ada_kv, b_ada_kv, w_kv, w_b_q, rel_bias, w_b_out):
    assert DEPTH == 2 and w_a_in.shape[0] == 1 and w_b_q.shape[0] == 1
    n_prompt, n_sample = x_prompt.shape[0], x_sample.shape[0]
    assert n_prompt == 1 and n_prompt + n_sample <= ADA_ROWS

    ffn_a, ffn_u = _cast_split(w_ffn_up.reshape(DEPTH * 2 * D_MODEL, 2 * D_FF), (D_FF, D_FF), CAST_ROWS)
    ffn_down, = _cast_split(w_ffn_down.reshape(DEPTH * 2 * D_FF, D_MODEL), (D_MODEL,), FF_TILE)
    wts = {
        "ffn_a": ffn_a.reshape(DEPTH, 2, D_MODEL, D_FF),
        "ffn_u": ffn_u.reshape(DEPTH, 2, D_MODEL, D_FF),
        "ffn_down": ffn_down.reshape(DEPTH, 2, D_FF, D_MODEL),
        "a_in": _cast_split(w_a_in[0], (GLA_MAIN,), CAST_ROWS)[0],
        "a_in_gk": jnp.pad(w_a_in[0, :, GLA_MAIN:].astype(BF16), ((0, 0), (0, LANE - GLA_GATE_RANK))),
        "a_gk2": jnp.pad(w_a_gk2[0].astype(BF16), ((0, LANE - GLA_GATE_RANK), (0, 0))),
        "b_a_gk": b_a_gk[0][None],
        "g_a_norm": g_a_norm[0][None],
        "a_out": _cast_split(w_a_out[0], (D_MODEL,), CAST_ROWS)[0],
        "kv": _cast_split(w_kv, (2 * D_MODEL,), CAST_ROWS)[0],
        "b_q": _cast_split(w_b_q[0], (D_MODEL,), CAST_ROWS)[0],
        "b_out": _cast_split(w_b_out[0], (D_MODEL,), CAST_ROWS)[0],
        "rel_bias": rel_bias[0],
        "ln_g": ln_g,
        "ln_b": ln_b,
    }

    c_all = jnp.concatenate([c_prompt, c_sample], axis=0)
    c_all = jnp.pad(c_all, ((0, ADA_ROWS - c_all.shape[0]), (0, 0)))
    ada = _ada(c_all, w_ada, b_ada)
    ada_kv = _ada(c_all, w_ada_kv[None], b_ada_kv[None])[0]
    p_rows = slice(0, n_prompt)
    s_rows = slice(n_prompt, n_prompt + n_sample)

    gla_zero = jnp.zeros((n_prompt, GLA_HEADS, GLA_HK, GLA_HV), state_gla.dtype)
    y_p, s_p, k_p, v_p = _trunk(x_prompt, ada[:, p_rows], ada_kv[p_rows], gla_zero, None, wts)
    past = (cache_band_k, cache_band_v)
    y_s, s_s, k_s, v_s = _trunk(x_sample, ada[:, s_rows], ada_kv[s_rows], state_gla[0], past, wts)
    return (y_p, y_s, s_p, s_s, k_p, v_p, k_s, v_s)
```

```python
import functools

import jax
import jax.numpy as jnp
from jax import lax
from jax.experimental import pallas as pl
from jax.experimental.pallas import tpu as pltpu

F32 = jnp.float32
BF16 = jnp.bfloat16

D_MODEL = 2048
DEPTH = 2
CHUNK = 64
GLA_HEADS = 4
GLA_DK = D_MODEL // 2
GLA_DV = D_MODEL
GLA_HK = GLA_DK // GLA_HEADS
GLA_HV = GLA_DV // GLA_HEADS
GLA_GATE_RANK = 16
GLA_GATE_NORMALIZER = 16.0
GLA_MAIN = 2 * GLA_DK + 2 * GLA_DV
ATT_HEADS = 16
ATT_HD = D_MODEL // ATT_HEADS
BAND_PAST = 8 * CHUNK
MAX_REL = 256
D_FF = 5504
ALPHA = (2 * DEPTH) ** 0.25
LN_EPS = 1e-5
RMS_EPS = 1e-6
NEG = -1e30
LOG2E = 1.4426950408889634

LANE = 128
FF_TILE = 512
FF_LAST = D_FF - (D_FF - 1) // FF_TILE * FF_TILE
ROW_TILE = 512
FFN_ROW_TILE = 1024
CAST_ROWS = 256
MODMM_COLS = 1024
ATT_SUB = 256
ATT_HEADS_PER_STEP = 2
REL_LEN = BAND_PAST + 2 * ATT_SUB
ADA_ROWS = 16
ADA_TILE = 1024
VMEM_LIMIT = 56 * 1024 * 1024


def _params(*sem):
    return pltpu.CompilerParams(dimension_semantics=sem, vmem_limit_bytes=VMEM_LIMIT)


def _layer_norm(z, g, b):
    mu = jnp.mean(z, axis=-1, keepdims=True)
    zc = z - mu
    var = jnp.mean(zc * zc, axis=-1, keepdims=True)
    return zc * lax.rsqrt(var + LN_EPS) * g + b


def _silu(a):
    return a * jax.nn.sigmoid(a)


def _ada_kernel(c_ref, w_ref, b_ref, o_ref):
    o_ref[...] = jnp.dot(_silu(c_ref[...]), w_ref[...], preferred_element_type=F32) + b_ref[...]


def _ada(c_all, w, b):
    n_l, _, n = w.shape
    return pl.pallas_call(
        _ada_kernel,
        grid=(n_l, n // ADA_TILE),
        in_specs=[
            pl.BlockSpec((ADA_ROWS, D_MODEL), lambda l, j: (0, 0)),
            pl.BlockSpec((None, D_MODEL, ADA_TILE), lambda l, j: (l, 0, j)),
            pl.BlockSpec((None, 1, ADA_TILE), lambda l, j: (l, 0, j)),
        ],
        out_specs=pl.BlockSpec((None, ADA_ROWS, ADA_TILE), lambda l, j: (l, 0, j)),
        out_shape=jax.ShapeDtypeStruct((n_l, ADA_ROWS, n), F32),
        compiler_params=_params("arbitrary", "arbitrary"),
        name="ada",
    )(c_all, w, b.reshape(n_l, 1, n))


def _mod_spec(mod_rows, tm):
    if mod_rows == 1:
        return pl.BlockSpec((1, D_MODEL), lambda i, *_: (0, 0))
    return pl.BlockSpec((tm, D_MODEL), lambda i, *_: (i, 0))


def _ffn_kernel(x_ref, sh_ref, sc_ref, gt_ref, wa_ref, wu_ref, wd_ref, g_ref, b_ref, o_ref, h_ref):
    j = pl.program_id(1)
    last = pl.num_programs(1) - 1

    @pl.when(j == 0)
    def _():
        h_ref[...] = (x_ref[...] * (1.0 + sc_ref[...]) + sh_ref[...]).astype(BF16)
        o_ref[...] = jnp.zeros_like(o_ref)

    def chunk(cols):
        h = h_ref[...]
        a = jnp.dot(h, wa_ref[:, :cols], preferred_element_type=F32)
        u = jnp.dot(h, wu_ref[:, :cols], preferred_element_type=F32)
        o_ref[...] += jnp.dot((_silu(a) * u).astype(BF16), wd_ref[:cols, :], preferred_element_type=F32)

    @pl.when(j < last)
    def _():
        chunk(FF_TILE)

    @pl.when(j == last)
    def _():
        chunk(FF_LAST)
        z = ALPHA * x_ref[...] + (1.0 + gt_ref[...]) * (0.5 * o_ref[...])
        o_ref[...] = _layer_norm(z, g_ref[...], b_ref[...])


def _ffn(x, shift, scale, gate, w_a, w_u, w_down, layer, idx, ln_g, ln_b, tm):
    m = x.shape[0]
    n_ff = pl.cdiv(D_FF, FF_TILE)
    mod = _mod_spec(shift.shape[0], tm)
    vec = pl.BlockSpec((1, D_MODEL), lambda i, j: (0, 0))
    up = pl.BlockSpec((None, None, D_MODEL, FF_TILE), lambda i, j: (layer, idx, 0, j))
    return pl.pallas_call(
        _ffn_kernel,
        grid=(m // tm, n_ff),
        in_specs=[
            pl.BlockSpec((tm, D_MODEL), lambda i, j: (i, 0)),
            mod, mod, mod,
            up, up,
            pl.BlockSpec((None, None, FF_TILE, D_MODEL), lambda i, j: (layer, idx, j, 0)),
            vec, vec,
        ],
        out_specs=pl.BlockSpec((tm, D_MODEL), lambda i, j: (i, 0)),
        out_shape=jax.ShapeDtypeStruct((m, D_MODEL), F32),
        scratch_shapes=[pltpu.VMEM((tm, D_MODEL), BF16)],
        compiler_params=_params("parallel", "arbitrary"),
        name="ffn",
    )(x, shift, scale, gate, w_a, w_u, w_down, ln_g, ln_b)


def _cast_kernel(x_ref, *o_refs):
    off = 0
    for o_ref in o_refs:
        n = o_ref.shape[-1]
        o_ref[...] = x_ref[:, off:off + n].astype(o_ref.dtype)
        off += n


def _cast_split(w, widths, rows):
    r, n = w.shape
    assert r % rows == 0 and sum(widths) <= n and all(wd % LANE == 0 for wd in widths)
    out = pl.pallas_call(
        _cast_kernel,
        grid=(r // rows,),
        in_specs=[pl.BlockSpec((rows, n), lambda i: (i, 0))],
        out_specs=[pl.BlockSpec((rows, wd), lambda i: (i, 0)) for wd in widths],
        out_shape=[jax.ShapeDtypeStruct((r, wd), BF16) for wd in widths],
        compiler_params=_params("parallel"),
        name="cast",
    )(w)
    return out


def _modmm_kernel(x_ref, sh_ref, sc_ref, w_ref, *rest, has_extra, out_scale):
    if has_extra:
        w2_ref, o_ref, o2_ref, h_ref = rest
    else:
        o_ref, h_ref = rest
    h_ref[...] = (x_ref[...] * (1.0 + sc_ref[...]) + sh_ref[...]).astype(BF16)
    h = h_ref[...]
    if has_extra:
        o2_ref[...] = jnp.dot(h, w2_ref[...], preferred_element_type=F32).astype(o2_ref.dtype)
    for c in range(w_ref.shape[1] // MODMM_COLS):
        cols = slice(c * MODMM_COLS, (c + 1) * MODMM_COLS)
        y = jnp.dot(h, w_ref[:, cols], preferred_element_type=F32)
        if out_scale is not None:
            y = y * out_scale
        o_ref[:, cols] = y.astype(o_ref.dtype)


def _modmm(x, shift, scale, w, tm, out_dtype, w_extra=None, out_scale=None):
    m = x.shape[0]
    n = w.shape[1]
    mod = _mod_spec(shift.shape[0], tm)
    resident = functools.partial(pl.BlockSpec, index_map=lambda i: (0, 0), pipeline_mode=pl.Buffered(1))
    in_specs = [
        pl.BlockSpec((tm, D_MODEL), lambda i: (i, 0)),
        mod, mod,
        resident((D_MODEL, n)),
    ]
    out_specs = pl.BlockSpec((tm, n), lambda i: (i, 0))
    out_shape = jax.ShapeDtypeStruct((m, n), out_dtype)
    args = [x, shift, scale, w]
    if w_extra is not None:
        n2 = w_extra.shape[1]
        in_specs.append(resident((D_MODEL, n2)))
        out_specs = [out_specs, pl.BlockSpec((tm, n2), lambda i: (i, 0))]
        out_shape = [out_shape, jax.ShapeDtypeStruct((m, n2), out_dtype)]
        args.append(w_extra)
    return pl.pallas_call(
        functools.partial(_modmm_kernel, has_extra=w_extra is not None, out_scale=out_scale),
        grid=(m // tm,),
        in_specs=in_specs,
        out_specs=out_specs,
        out_shape=out_shape,
        scratch_shapes=[pltpu.VMEM((tm, D_MODEL), BF16)],
        compiler_params=_params("parallel"),
        name="modmm",
    )(*args)


def _proj_ln_kernel(a_ref, w_ref, x_ref, gt_ref, g_ref, b_ref, o_ref):
    y = jnp.dot(a_ref[...], w_ref[...], preferred_element_type=F32)
    z = ALPHA * x_ref[...] + (1.0 + gt_ref[...]) * y
    o_ref[...] = _layer_norm(z, g_ref[...], b_ref[...])


def _proj_ln(a, w, x, gate, ln_g, ln_b, tm):
    m = x.shape[0]
    vec = pl.BlockSpec((1, D_MODEL), lambda i: (0, 0))
    return pl.pallas_call(
        _proj_ln_kernel,
        grid=(m // tm,),
        in_specs=[
            pl.BlockSpec((tm, D_MODEL), lambda i: (i, 0)),
            pl.BlockSpec((D_MODEL, D_MODEL), lambda i: (0, 0)),
            pl.BlockSpec((tm, D_MODEL), lambda i: (i, 0)),
            _mod_spec(gate.shape[0], tm),
            vec, vec,
        ],
        out_specs=pl.BlockSpec((tm, D_MODEL), lambda i: (i, 0)),
        out_shape=jax.ShapeDtypeStruct((m, D_MODEL), F32),
        compiler_params=_params("parallel"),
        name="proj_ln",
    )(a, w, x, gate, ln_g, ln_b)


def _gla_kernel(q_ref, k_ref, v_ref, g_ref, gk_ref, w2_ref, bgk_ref, gn_ref, s0_ref,
                o_ref, s_ref, *, blk, n_blk):
    @pl.when(pl.program_id(2) == 0)
    def _():
        s_ref[...] = s0_ref[...]

    causal = (lax.broadcasted_iota(jnp.int32, (blk, blk), 0)
              >= lax.broadcasted_iota(jnp.int32, (blk, blk), 1))
    tril = jnp.broadcast_to(jnp.where(causal, 1.0, 0.0).astype(BF16), (n_blk, blk, blk))
    eye = (lax.broadcasted_iota(jnp.int32, (GLA_HK, GLA_HK), 0)
           == lax.broadcasted_iota(jnp.int32, (GLA_HK, GLA_HK), 1))
    tn = (((0,), (0,)), ((), ()))
    bnn = (((2,), (1,)), ((0,), (0,)))
    bnt = (((2,), (2,)), ((0,), (0,)))

    def blocks(t):
        return t.reshape(n_blk, blk, t.shape[-1])

    z = jnp.dot(gk_ref[...], w2_ref[...], preferred_element_type=F32) + bgk_ref[...]
    log_a = blocks((jnp.minimum(z, 0.0) - jnp.log1p(jnp.exp(-jnp.abs(z)))) * (1.0 / GLA_GATE_NORMALIZER))
    hi = log_a.astype(BF16)
    lo = (log_a - hi.astype(F32)).astype(BF16)
    b = (lax.dot_general(tril, hi, bnn, preferred_element_type=F32)
         + lax.dot_general(tril, lo, bnn, preferred_element_type=F32))
    b_last = b[:, blk - 1:blk, :]
    q = blocks(q_ref[...].astype(F32)) * (GLA_HK ** -0.5)
    k = blocks(k_ref[...].astype(F32))
    v = blocks(v_ref[...])
    q_t = (q * jnp.exp(b)).astype(BF16)
    k_t = (k * jnp.exp(-b)).astype(BF16)
    k_dec = (k * jnp.exp(b_last - b)).astype(BF16)
    scores = lax.dot_general(q_t, k_t, bnt, preferred_element_type=F32)
    scores = jnp.where(causal[None], scores, 0.0).astype(BF16)
    o_intra = lax.dot_general(scores, v, bnn, preferred_element_type=F32)
    decay = jnp.exp(b_last)

    state = s_ref[0, 0]
    for c in range(n_blk):
        rows = slice(c * blk, (c + 1) * blk)
        o = o_intra[c] + jnp.dot(q_t[c], state.astype(BF16), preferred_element_type=F32)
        decay_col = jnp.sum(jnp.where(eye, jnp.broadcast_to(decay[c], (GLA_HK, GLA_HK)), 0.0),
                            axis=1, keepdims=True)
        state = decay_col * state + lax.dot_general(k_dec[c], v[c], tn, preferred_element_type=F32)
        on = o * lax.rsqrt(jnp.mean(o * o, axis=-1, keepdims=True) + RMS_EPS) * gn_ref[...]
        o_ref[rows, :] = (on * _silu(g_ref[rows, :].astype(F32))).astype(o_ref.dtype)
    s_ref[0, 0] = state


def _gla(proj, gk_low, w_gk2, b_gk, g_norm, s0, batch, seq):
    blk = min(CHUNK, seq)
    rows = min(ROW_TILE, seq)
    n_steps = seq // rows
    m = batch * seq
    k_off = GLA_DK // GLA_HK
    v_off = 2 * GLA_DK // GLA_HV
    g_off = (2 * GLA_DK + GLA_DV) // GLA_HV
    row = lambda b, h, c: b * n_steps + c
    state_spec = pl.BlockSpec((1, 1, GLA_HK, GLA_HV), lambda b, h, c: (b, h, 0, 0))
    return pl.pallas_call(
        functools.partial(_gla_kernel, blk=blk, n_blk=rows // blk),
        grid=(batch, GLA_HEADS, n_steps),
        in_specs=[
            pl.BlockSpec((rows, GLA_HK), lambda b, h, c: (row(b, h, c), h)),
            pl.BlockSpec((rows, GLA_HK), lambda b, h, c: (row(b, h, c), k_off + h)),
            pl.BlockSpec((rows, GLA_HV), lambda b, h, c: (row(b, h, c), v_off + h)),
            pl.BlockSpec((rows, GLA_HV), lambda b, h, c: (row(b, h, c), g_off + h)),
            pl.BlockSpec((rows, LANE), lambda b, h, c: (row(b, h, c), 0)),
            pl.BlockSpec((LANE, GLA_HK), lambda b, h, c: (0, h)),
            pl.BlockSpec((1, GLA_HK), lambda b, h, c: (0, h)),
            pl.BlockSpec((1, GLA_HV), lambda b, h, c: (0, 0)),
            state_spec,
        ],
        out_specs=[
            pl.BlockSpec((rows, GLA_HV), lambda b, h, c: (row(b, h, c), h)),
            state_spec,
        ],
        out_shape=[
            jax.ShapeDtypeStruct((m, GLA_DV), BF16),
            jax.ShapeDtypeStruct(s0.shape, F32),
        ],
        compiler_params=_params("parallel", "parallel", "arbitrary"),
        name="gla",
    )(proj, proj, proj, proj, gk_low, w_gk2, b_gk, g_norm, s0)


def _band_attn_kernel(q_ref, kp_ref, kc_ref, vp_ref, vc_ref, rel_ref, o_ref, bias_ref):
    i = pl.program_id(1)
    nt = (((1,), (1,)), ((), ()))
    n_keys = BAND_PAST + ATT_SUB
    n_sub = BAND_PAST // ATT_SUB

    @pl.when(i <= 1)
    def _():
        col = lax.broadcasted_iota(jnp.int32, (ATT_SUB, n_keys), 1)
        qc = lax.broadcasted_iota(jnp.int32, (ATT_SUB, n_keys), 0) // CHUNK
        kc = col // CHUNK
        in_band = (kc >= qc) & (kc <= qc + BAND_PAST // CHUNK)
        for h in range(ATT_HEADS_PER_STEP):
            rel = jnp.broadcast_to(rel_ref[h], (ATT_SUB, REL_LEN))
            toeplitz = pltpu.roll(rel, 0, 1, stride=1, stride_axis=0)[:, ATT_SUB:] * LOG2E
            for s in range(n_sub):
                key_pos = (i - 1) * BAND_PAST + s * ATT_SUB + col
                bias_ref[h, s] = jnp.where(in_band & (key_pos >= 0), toeplitz, NEG)

    units = [(h, s) for h in range(ATT_HEADS_PER_STEP) for s in range(n_sub)]

    def scores(h, s):
        lo, hi = s * ATT_SUB, (s + 1) * ATT_SUB
        cols = slice(h * ATT_HD, (h + 1) * ATT_HD)
        k = jnp.concatenate([kp_ref[lo:, cols], kc_ref[:hi, cols]], axis=0)
        return lax.dot_general(q_ref[lo:hi, cols], k, nt, preferred_element_type=F32) + bias_ref[h, s]

    def finish(h, s, sc):
        lo, hi = s * ATT_SUB, (s + 1) * ATT_SUB
        cols = slice(h * ATT_HD, (h + 1) * ATT_HD)
        v = jnp.concatenate([vp_ref[lo:, cols], vc_ref[:hi, cols]], axis=0)
        p = jnp.exp2(sc - jnp.max(sc, axis=-1, keepdims=True))
        denom = jnp.sum(p, axis=-1, keepdims=True)
        o = jnp.dot(p.astype(BF16), v, preferred_element_type=F32) / denom
        o_ref[lo:hi, cols] = o.astype(o_ref.dtype)

    sc = scores(*units[0])
    for n, unit in enumerate(units):
        sc_next = scores(*units[n + 1]) if n + 1 < len(units) else None
        finish(*unit, sc)
        sc = sc_next


def _band_rel(table):
    assert BAND_PAST + ATT_SUB - REL_LEN // 2 == MAX_REL and REL_LEN // 2 <= 2 * MAX_REL
    far = jnp.broadcast_to(table[:, 2 * MAX_REL:], (ATT_HEADS, REL_LEN // 2))
    near = table[:, 2 * MAX_REL - REL_LEN // 2 + 1:][:, ::-1]
    return jnp.concatenate([far, near], axis=1).astype(F32)[:, None, :]


def _band_attn(q, kv, table):
    t = q.shape[0]
    n_keys = BAND_PAST + ATT_SUB
    prev = lambda i: jnp.maximum(i - 1, 0)
    n_groups = ATT_HEADS // ATT_HEADS_PER_STEP
    blk = (BAND_PAST, ATT_HEADS_PER_STEP * ATT_HD)
    return pl.pallas_call(
        _band_attn_kernel,
        grid=(n_groups, t // BAND_PAST),
        in_specs=[
            pl.BlockSpec(blk, lambda h, i: (i, h)),
            pl.BlockSpec(blk, lambda h, i: (prev(i), h)),
            pl.BlockSpec(blk, lambda h, i: (i, h)),
            pl.BlockSpec(blk, lambda h, i: (prev(i), n_groups + h)),
            pl.BlockSpec(blk, lambda h, i: (i, n_groups + h)),
            pl.BlockSpec((ATT_HEADS_PER_STEP, 1, REL_LEN), lambda h, i: (h, 0, 0)),
        ],
        out_specs=pl.BlockSpec(blk, lambda h, i: (i, h)),
        out_shape=jax.ShapeDtypeStruct((t, D_MODEL), BF16),
        scratch_shapes=[pltpu.VMEM((ATT_HEADS_PER_STEP, BAND_PAST // ATT_SUB, ATT_SUB, n_keys), F32)],
        compiler_params=_params("parallel", "arbitrary"),
        name="band_attn",
    )(q, kv, kv, kv, kv, _band_rel(table))


def _step_attn_kernel(q_ref, kvn_ref, ck_ref, cv_ref, bp_ref, bn_ref, o_ref):
    nt = (((1,), (1,)), ((), ()))
    for h in range(ATT_HEADS):
        cols = slice(h * ATT_HD, (h + 1) * ATT_HD)
        q = q_ref[:, cols]
        k_new = kvn_ref[:, cols].astype(BF16)
        v_new = kvn_ref[:, D_MODEL + h * ATT_HD:D_MODEL + (h + 1) * ATT_HD].astype(BF16)
        k_past = ck_ref[0, :, h, :].astype(BF16)
        v_past = cv_ref[0, :, h, :].astype(BF16)
        sp = lax.dot_general(q, k_past, nt, preferred_element_type=F32) + bp_ref[h] * LOG2E
        sn = lax.dot_general(q, k_new, nt, preferred_element_type=F32) + bn_ref[h] * LOG2E
        mx = jnp.maximum(jnp.max(sp, axis=-1, keepdims=True), jnp.max(sn, axis=-1, keepdims=True))
        pp = jnp.exp2(sp - mx)
        pn = jnp.exp2(sn - mx)
        denom = jnp.sum(pp, axis=-1, keepdims=True) + jnp.sum(pn, axis=-1, keepdims=True)
        o = (jnp.dot(pp.astype(BF16), v_past, preferred_element_type=F32)
             + jnp.dot(pn.astype(BF16), v_new, preferred_element_type=F32)) / denom
        o_ref[:, cols] = o.astype(o_ref.dtype)


def _step_attn(q, kv_new, cache_k, cache_v, table, batch, seq):
    n_past = cache_k.shape[1]
    dist = n_past + jnp.arange(seq)[:, None] - jnp.arange(n_past + seq)[None, :]
    bias = table[:, jnp.clip(dist, -MAX_REL, MAX_REL) + MAX_REL].astype(F32)
    return pl.pallas_call(
        _step_attn_kernel,
        grid=(batch,),
        in_specs=[
            pl.BlockSpec((seq, D_MODEL), lambda b: (b, 0)),
            pl.BlockSpec((seq, 2 * D_MODEL), lambda b: (b, 0)),
            pl.BlockSpec((1, n_past, ATT_HEADS, ATT_HD), lambda b: (b, 0, 0, 0)),
            pl.BlockSpec((1, n_past, ATT_HEADS, ATT_HD), lambda b: (b, 0, 0, 0)),
            pl.BlockSpec((ATT_HEADS, seq, n_past), lambda b: (0, 0, 0)),
            pl.BlockSpec((ATT_HEADS, seq, seq), lambda b: (0, 0, 0)),
        ],
        out_specs=pl.BlockSpec((seq, D_MODEL), lambda b: (b, 0)),
        out_shape=jax.ShapeDtypeStruct((batch * seq, D_MODEL), BF16),
        compiler_params=_params("parallel"),
        name="step_attn",
    )(q, kv_new, cache_k, cache_v, bias[:, :, :n_past], bias[:, :, n_past:])


def _trunk(x, ada, ada_kv, gla_s0, past, wts):
    batch, seq, _ = x.shape
    m = batch * seq
    tm = min(ROW_TILE, m)
    x = x.reshape(m, D_MODEL)

    def rows(v):
        return v if batch == 1 else jnp.repeat(v, seq, axis=0)

    def mod(l, sub, which):
        return rows(ada[l][:, (3 * sub + which) * D_MODEL:(3 * sub + which + 1) * D_MODEL])

    def ffn(x, l, sub, idx):
        return _ffn(x, mod(l, sub, 0), mod(l, sub, 1), mod(l, sub, 2),
                    wts["ffn_a"], wts["ffn_u"], wts["ffn_down"], l, idx,
                    wts["ln_g"][l, sub][None], wts["ln_b"][l, sub][None], min(FFN_ROW_TILE, m))

    x = ffn(x, 0, 0, 0)
    proj, gk_low = _modmm(x, mod(0, 1, 0), mod(0, 1, 1), wts["a_in"], tm, BF16,
                          w_extra=wts["a_in_gk"])
    o, gla_state = _gla(proj, gk_low, wts["a_gk2"], wts["b_a_gk"], wts["g_a_norm"], gla_s0, batch, seq)
    x = _proj_ln(o, wts["a_out"], x, mod(0, 1, 2), wts["ln_g"][0, 1][None], wts["ln_b"][0, 1][None], tm)
    x = ffn(x, 0, 2, 1)

    kv_shift = rows(ada_kv[:, :D_MODEL])
    kv_scale = rows(ada_kv[:, D_MODEL:])
    if past is None:
        kv = _modmm(x, kv_shift, kv_scale, wts["kv"], tm, BF16)
        n_keep = min(BAND_PAST, seq)
        kv_out = _modmm(x[m - n_keep:], kv_shift, kv_scale, wts["kv"], n_keep, F32)
    else:
        kv_out = _modmm(x, kv_shift, kv_scale, wts["kv"], tm, F32)

    x = ffn(x, 1, 0, 0)
    q = _modmm(x, mod(1, 1, 0), mod(1, 1, 1), wts["b_q"], tm, BF16,
               out_scale=ATT_HD ** -0.5 * LOG2E)
    if past is None:
        o = _band_attn(q, kv, wts["rel_bias"])
    else:
        o = _step_attn(q, kv_out, past[0], past[1], wts["rel_bias"], batch, seq)
    x = _proj_ln(o, wts["b_out"], x, mod(1, 1, 2), wts["ln_g"][1, 1][None], wts["ln_b"][1, 1][None], tm)
    x = ffn(x, 1, 2, 1)

    n_rows = kv_out.shape[0] // batch
    k_out = kv_out[:, :D_MODEL].reshape(batch, n_rows, ATT_HEADS, ATT_HD)
    v_out = kv_out[:, D_MODEL:].reshape(batch, n_rows, ATT_HEADS, ATT_HD)
    return x.reshape(batch, seq, D_MODEL), gla_state[None], k_out, v_out


def kernel(x_prompt, x_sample, state_gla, cache_band_k, cache_band_v, c_prompt, c_sample, w_ada, b_ada, ln_g, ln_b, w_ffn_up, w_ffn_down, w_a_in, w_a_gk2, b_a_gk, g_a_norm, w_a_out, w_ada_kv, b_ada_kv, w_kv, w_b_q, rel_bias, w_b_out):
    assert DEPTH == 2 and w_a_in.shape[0] == 1 and w_b_q.shape[0] == 1
    n_prompt, n_sample = x_prompt.shape[0], x_sample.shape[0]
    assert n_prompt == 1 and n_prompt + n_sample <= ADA_ROWS

    ffn_a, ffn_u = _cast_split(w_ffn_up.reshape(DEPTH * 2 * D_MODEL, 2 * D_FF), (D_FF, D_FF), CAST_ROWS)
    ffn_down, = _cast_split(w_ffn_down.reshape(DEPTH * 2 * D_FF, D_MODEL), (D_MODEL,), FF_TILE)
    wts = {
        "ffn_a": ffn_a.reshape(DEPTH, 2, D_MODEL, D_FF),
        "ffn_u": ffn_u.reshape(DEPTH, 2, D_MODEL, D_FF),
        "ffn_down": ffn_down.reshape(DEPTH, 2, D_FF, D_MODEL),
        "a_in": _cast_split(w_a_in[0], (GLA_MAIN,), CAST_ROWS)[0],
        "a_in_gk": jnp.pad(w_a_in[0, :, GLA_MAIN:].astype(BF16), ((0, 0), (0, LANE - GLA_GATE_RANK))),
        "a_gk2": jnp.pad(w_a_gk2[0].astype(BF16), ((0, LANE - GLA_GATE_RANK), (0, 0))),
        "b_a_gk": b_a_gk[0][None],
        "g_a_norm": g_a_norm[0][None],
        "a_out": _cast_split(w_a_out[0], (D_MODEL,), CAST_ROWS)[0],
        "kv": _cast_split(w_kv, (2 * D_MODEL,), CAST_ROWS)[0],
        "b_q": _cast_split(w_b_q[0], (D_MODEL,), CAST_ROWS)[0],
        "b_out": _cast_split(w_b_out[0], (D_MODEL,), CAST_ROWS)[0],
        "rel_bias": rel_bias[0],
        "ln_g": ln_g,
        "ln_b": ln_b,
    }

    c_all = jnp.concatenate([c_prompt, c_sample], axis=0)
    c_all = jnp.pad(c_all, ((0, ADA_ROWS - c_all.shape[0]), (0, 0)))
    ada = _ada(c_all, w_ada, b_ada)
    ada_kv = _ada(c_all, w_ada_kv[None], b_ada_kv[None])[0]
    p_rows = slice(0, n_prompt)
    s_rows = slice(n_prompt, n_prompt + n_sample)

    gla_zero = jnp.zeros((n_prompt, GLA_HEADS, GLA_HK, GLA_HV), state_gla.dtype)
    y_p, s_p, k_p, v_p = _trunk(x_prompt, ada[:, p_rows], ada_kv[p_rows], gla_zero, None, wts)
    past = (cache_band_k, cache_band_v)
    y_s, s_s, k_s, v_s = _trunk(x_sample, ada[:, s_rows], ada_kv[s_rows], state_gla[0], past, wts)
    return (y_p, y_s, s_p, s_s, k_p, v_p, k_s, v_s)
```

```python
import functools

import jax
import jax.numpy as jnp
from jax import lax
from jax.experimental import pallas as pl
from jax.experimental.pallas import tpu as pltpu

F32 = jnp.float32
BF16 = jnp.bfloat16

D_MODEL = 2048
DEPTH = 2
CHUNK = 64
GLA_HEADS = 4
GLA_DK = D_MODEL // 2
GLA_DV = D_MODEL
GLA_HK = GLA_DK // GLA_HEADS
GLA_HV = GLA_DV // GLA_HEADS
GLA_GATE_RANK = 16
GLA_GATE_NORMALIZER = 16.0
GLA_MAIN = 2 * GLA_DK + 2 * GLA_DV
ATT_HEADS = 16
ATT_HD = D_MODEL // ATT_HEADS
BAND_PAST = 8 * CHUNK
MAX_REL = 256
D_FF = 5504
ALPHA = (2 * DEPTH) ** 0.25
LN_EPS = 1e-5
RMS_EPS = 1e-6
NEG = -1e30
LOG2E = 1.4426950408889634

LANE = 128
FF_TILE = 512
FF_LAST = D_FF - (D_FF - 1) // FF_TILE * FF_TILE
ROW_TILE = 512
FFN_ROW_TILE = 1024
CAST_ROWS = 256
MODMM_COLS = 1024
ATT_SUB = 256
ATT_HEADS_PER_STEP = 4
REL_LEN = BAND_PAST + 2 * ATT_SUB
ADA_ROWS = 16
ADA_TILE = 1024
VMEM_LIMIT = 56 * 1024 * 1024


def _params(*sem):
    return pltpu.CompilerParams(dimension_semantics=sem, vmem_limit_bytes=VMEM_LIMIT)


def _layer_norm(z, g, b):
    mu = jnp.mean(z, axis=-1, keepdims=True)
    zc = z - mu
    var = jnp.mean(zc * zc, axis=-1, keepdims=True)
    return zc * lax.rsqrt(var + LN_EPS) * g + b


def _silu(a):
    return a * jax.nn.sigmoid(a)


def _ada_kernel(c_ref, w_ref, b_ref, o_ref):
    o_ref[...] = jnp.dot(_silu(c_ref[...]), w_ref[...], preferred_element_type=F32) + b_ref[...]


def _ada(c_all, w, b):
    n_l, _, n = w.shape
    return pl.pallas_call(
        _ada_kernel,
        grid=(n_l, n // ADA_TILE),
        in_specs=[
            pl.BlockSpec((ADA_ROWS, D_MODEL), lambda l, j: (0, 0)),
            pl.BlockSpec((None, D_MODEL, ADA_TILE), lambda l, j: (l, 0, j)),
            pl.BlockSpec((None, 1, ADA_TILE), lambda l, j: (l, 0, j)),
        ],
        out_specs=pl.BlockSpec((None, ADA_ROWS, ADA_TILE), lambda l, j: (l, 0, j)),
        out_shape=jax.ShapeDtypeStruct((n_l, ADA_ROWS, n), F32),
        compiler_params=_params("arbitrary", "arbitrary"),
        name="ada",
    )(c_all, w, b.reshape(n_l, 1, n))


def _mod_spec(mod_rows, tm):
    if mod_rows == 1:
        return pl.BlockSpec((1, D_MODEL), lambda i, *_: (0, 0))
    return pl.BlockSpec((tm, D_MODEL), lambda i, *_: (i, 0))


def _ffn_kernel(x_ref, sh_ref, sc_ref, gt_ref, wa_ref, wu_ref, wd_ref, g_ref, b_ref, o_ref, h_ref):
    j = pl.program_id(1)
    last = pl.num_programs(1) - 1

    @pl.when(j == 0)
    def _():
        h_ref[...] = (x_ref[...] * (1.0 + sc_ref[...]) + sh_ref[...]).astype(BF16)
        o_ref[...] = jnp.zeros_like(o_ref)

    def chunk(cols):
        h = h_ref[...]
        a = jnp.dot(h, wa_ref[:, :cols], preferred_element_type=F32)
        u = jnp.dot(h, wu_ref[:, :cols], preferred_element_type=F32)
        o_ref[...] += jnp.dot((_silu(a) * u).astype(BF16), wd_ref[:cols, :], preferred_element_type=F32)

    @pl.when(j < last)
    def _():
        chunk(FF_TILE)

    @pl.when(j == last)
    def _():
        chunk(FF_LAST)
        z = ALPHA * x_ref[...] + (1.0 + gt_ref[...]) * (0.5 * o_ref[...])
        o_ref[...] = _layer_norm(z, g_ref[...], b_ref[...])


def _ffn(x, shift, scale, gate, w_a, w_u, w_down, layer, idx, ln_g, ln_b, tm):
    m = x.shape[0]
    n_ff = pl.cdiv(D_FF, FF_TILE)
    mod = _mod_spec(shift.shape[0], tm)
    vec = pl.BlockSpec((1, D_MODEL), lambda i, j: (0, 0))
    up = pl.BlockSpec((None, None, D_MODEL, FF_TILE), lambda i, j: (layer, idx, 0, j))
    return pl.pallas_call(
        _ffn_kernel,
        grid=(m // tm, n_ff),
        in_specs=[
            pl.BlockSpec((tm, D_MODEL), lambda i, j: (i, 0)),
            mod, mod, mod,
            up, up,
            pl.BlockSpec((None, None, FF_TILE, D_MODEL), lambda i, j: (layer, idx, j, 0)),
            vec, vec,
        ],
        out_specs=pl.BlockSpec((tm, D_MODEL), lambda i, j: (i, 0)),
        out_shape=jax.ShapeDtypeStruct((m, D_MODEL), F32),
        scratch_shapes=[pltpu.VMEM((tm, D_MODEL), BF16)],
        compiler_params=_params("parallel", "arbitrary"),
        name="ffn",
    )(x, shift, scale, gate, w_a, w_u, w_down, ln_g, ln_b)


def _cast_kernel(x_ref, *o_refs):
    off = 0
    for o_ref in o_refs:
        n = o_ref.shape[-1]
        o_ref[...] = x_ref[:, off:off + n].astype(o_ref.dtype)
        off += n


def _cast_split(w, widths, rows):
    r, n = w.shape
    assert r % rows == 0 and sum(widths) <= n and all(wd % LANE == 0 for wd in widths)
    out = pl.pallas_call(
        _cast_kernel,
        grid=(r // rows,),
        in_specs=[pl.BlockSpec((rows, n), lambda i: (i, 0))],
        out_specs=[pl.BlockSpec((rows, wd), lambda i: (i, 0)) for wd in widths],
        out_shape=[jax.ShapeDtypeStruct((r, wd), BF16) for wd in widths],
        compiler_params=_params("parallel"),
        name="cast",
    )(w)
    return out


def _modmm_kernel(x_ref, sh_ref, sc_ref, w_ref, *rest, has_extra, out_scale):
    if has_extra:
        w2_ref, o_ref, o2_ref, h_ref = rest
    else:
        o_ref, h_ref = rest
    h_ref[...] = (x_ref[...] * (1.0 + sc_ref[...]) + sh_ref[...]).astype(BF16)
    h = h_ref[...]
    if has_extra:
        o2_ref[...] = jnp.dot(h, w2_ref[...], preferred_element_type=F32).astype(o2_ref.dtype)
    for c in range(w_ref.shape[1] // MODMM_COLS):
        cols = slice(c * MODMM_COLS, (c + 1) * MODMM_COLS)
        y = jnp.dot(h, w_ref[:, cols], preferred_element_type=F32)
        if out_scale is not None:
            y = y * out_scale
        o_ref[:, cols] = y.astype(o_ref.dtype)


def _modmm(x, shift, scale, w, tm, out_dtype, w_extra=None, out_scale=None):
    m = x.shape[0]
    n = w.shape[1]
    mod = _mod_spec(shift.shape[0], tm)
    resident = functools.partial(pl.BlockSpec, index_map=lambda i: (0, 0), pipeline_mode=pl.Buffered(1))
    in_specs = [
        pl.BlockSpec((tm, D_MODEL), lambda i: (i, 0)),
        mod, mod,
        resident((D_MODEL, n)),
    ]
    out_specs = pl.BlockSpec((tm, n), lambda i: (i, 0))
    out_shape = jax.ShapeDtypeStruct((m, n), out_dtype)
    args = [x, shift, scale, w]
    if w_extra is not None:
        n2 = w_extra.shape[1]
        in_specs.append(resident((D_MODEL, n2)))
        out_specs = [out_specs, pl.BlockSpec((tm, n2), lambda i: (i, 0))]
        out_shape = [out_shape, jax.ShapeDtypeStruct((m, n2), out_dtype)]
        args.append(w_extra)
    return pl.pallas_call(
        functools.partial(_modmm_kernel, has_extra=w_extra is not None, out_scale=out_scale),
        grid=(m // tm,),
        in_specs=in_specs,
        out_specs=out_specs,
        out_shape=out_shape,
        scratch_shapes=[pltpu.VMEM((tm, D_MODEL), BF16)],
        compiler_params=_params("parallel"),
        name="modmm",
    )(*args)


def _proj_ln_kernel(a_ref, w_ref, x_ref, gt_ref, g_ref, b_ref, o_ref):
    y = jnp.dot(a_ref[...], w_ref[...], preferred_element_type=F32)
    z = ALPHA * x_ref[...] + (1.0 + gt_ref[...]) * y
    o_ref[...] = _layer_norm(z, g_ref[...], b_ref[...])


def _proj_ln(a, w, x, gate, ln_g, ln_b, tm):
    m = x.shape[0]
    vec = pl.BlockSpec((1, D_MODEL), lambda i: (0, 0))
    return pl.pallas_call(
        _proj_ln_kernel,
        grid=(m // tm,),
        in_specs=[
            pl.BlockSpec((tm, D_MODEL), lambda i: (i, 0)),
            pl.BlockSpec((D_MODEL, D_MODEL), lambda i: (0, 0)),
            pl.BlockSpec((tm, D_MODEL), lambda i: (i, 0)),
            _mod_spec(gate.shape[0], tm),
            vec, vec,
        ],
        out_specs=pl.BlockSpec((tm, D_MODEL), lambda i: (i, 0)),
        out_shape=jax.ShapeDtypeStruct((m, D_MODEL), F32),
        compiler_params=_params("parallel"),
        name="proj_ln",
    )(a, w, x, gate, ln_g, ln_b)


def _gla_kernel(q_ref, k_ref, v_ref, g_ref, gk_ref, w2_ref, bgk_ref, gn_ref, s0_ref,
                o_ref, s_ref, *, blk, n_blk):
    @pl.when(pl.program_id(2) == 0)
    def _():
        s_ref[...] = s0_ref[...]

    causal = (lax.broadcasted_iota(jnp.int32, (blk, blk), 0)
              >= lax.broadcasted_iota(jnp.int32, (blk, blk), 1))
    tril = jnp.broadcast_to(jnp.where(causal, 1.0, 0.0).astype(BF16), (n_blk, blk, blk))
    eye = (lax.broadcasted_iota(jnp.int32, (GLA_HK, GLA_HK), 0)
           == lax.broadcasted_iota(jnp.int32, (GLA_HK, GLA_HK), 1))
    tn = (((0,), (0,)), ((), ()))
    bnn = (((2,), (1,)), ((0,), (0,)))
    bnt = (((2,), (2,)), ((0,), (0,)))

    def blocks(t):
        return t.reshape(n_blk, blk, t.shape[-1])

    z = jnp.dot(gk_ref[...], w2_ref[...], preferred_element_type=F32) + bgk_ref[...]
    log_a = blocks((jnp.minimum(z, 0.0) - jnp.log1p(jnp.exp(-jnp.abs(z)))) * (1.0 / GLA_GATE_NORMALIZER))
    hi = log_a.astype(BF16)
    lo = (log_a - hi.astype(F32)).astype(BF16)
    b = (lax.dot_general(tril, hi, bnn, preferred_element_type=F32)
         + lax.dot_general(tril, lo, bnn, preferred_element_type=F32))
    b_last = b[:, blk - 1:blk, :]
    q = blocks(q_ref[...].astype(F32)) * (GLA_HK ** -0.5)
    k = blocks(k_ref[...].astype(F32))
    v = blocks(v_ref[...])
    q_t = (q * jnp.exp(b)).astype(BF16)
    k_t = (k * jnp.exp(-b)).astype(BF16)
    k_dec = (k * jnp.exp(b_last - b)).astype(BF16)
    scores = lax.dot_general(q_t, k_t, bnt, preferred_element_type=F32)
    scores = jnp.where(causal[None], scores, 0.0).astype(BF16)
    o_intra = lax.dot_general(scores, v, bnn, preferred_element_type=F32)
    decay = jnp.exp(b_last)

    update = [lax.dot_general(k_dec[c], v[c], tn, preferred_element_type=F32) for c in range(n_blk)]
    decay_col = [jnp.sum(jnp.where(eye, jnp.broadcast_to(decay[c], (GLA_HK, GLA_HK)), 0.0),
                         axis=1, keepdims=True) for c in range(n_blk)]
    state = s_ref[0, 0]
    starts = []
    for c in range(n_blk):
        starts.append(state.astype(BF16))
        state = decay_col[c] * state + update[c]
    s_ref[0, 0] = state
    o = o_intra + jnp.stack([jnp.dot(q_t[c], starts[c], preferred_element_type=F32) for c in range(n_blk)])
    on = o * lax.rsqrt(jnp.mean(o * o, axis=-1, keepdims=True) + RMS_EPS) * gn_ref[...]
    gate = _silu(g_ref[...].astype(F32)).reshape(o.shape)
    o_ref[...] = (on * gate).reshape(o_ref.shape).astype(o_ref.dtype)


def _gla(proj, gk_low, w_gk2, b_gk, g_norm, s0, batch, seq):
    blk = min(CHUNK, seq)
    rows = min(ROW_TILE, seq)
    n_steps = seq // rows
    m = batch * seq
    k_off = GLA_DK // GLA_HK
    v_off = 2 * GLA_DK // GLA_HV
    g_off = (2 * GLA_DK + GLA_DV) // GLA_HV
    row = lambda b, h, c: b * n_steps + c
    state_spec = pl.BlockSpec((1, 1, GLA_HK, GLA_HV), lambda b, h, c: (b, h, 0, 0))
    return pl.pallas_call(
        functools.partial(_gla_kernel, blk=blk, n_blk=rows // blk),
        grid=(batch, GLA_HEADS, n_steps),
        in_specs=[
            pl.BlockSpec((rows, GLA_HK), lambda b, h, c: (row(b, h, c), h)),
            pl.BlockSpec((rows, GLA_HK), lambda b, h, c: (row(b, h, c), k_off + h)),
            pl.BlockSpec((rows, GLA_HV), lambda b, h, c: (row(b, h, c), v_off + h)),
            pl.BlockSpec((rows, GLA_HV), lambda b, h, c: (row(b, h, c), g_off + h)),
            pl.BlockSpec((rows, LANE), lambda b, h, c: (row(b, h, c), 0)),
            pl.BlockSpec((LANE, GLA_HK), lambda b, h, c: (0, h)),
            pl.BlockSpec((1, GLA_HK), lambda b, h, c: (0, h)),
            pl.BlockSpec((1, GLA_HV), lambda b, h, c: (0, 0)),
            state_spec,
        ],
        out_specs=[
            pl.BlockSpec((rows, GLA_HV), lambda b, h, c: (row(b, h, c), h)),
            state_spec,
        ],
        out_shape=[
            jax.ShapeDtypeStruct((m, GLA_DV), BF16),
            jax.ShapeDtypeStruct(s0.shape, F32),
        ],
        compiler_params=_params("parallel", "parallel", "arbitrary"),
        name="gla",
    )(proj, proj, proj, proj, gk_low, w_gk2, b_gk, g_norm, s0)


def _band_attn_kernel(q_ref, kp_ref, kc_ref, vp_ref, vc_ref, rel_ref, o_ref, bias_ref):
    i = pl.program_id(1)
    nt = (((1,), (1,)), ((), ()))
    n_keys = BAND_PAST + ATT_SUB
    n_sub = BAND_PAST // ATT_SUB

    @pl.when(i <= 1)
    def _():
        col = lax.broadcasted_iota(jnp.int32, (ATT_SUB, n_keys), 1)
        qc = lax.broadcasted_iota(jnp.int32, (ATT_SUB, n_keys), 0) // CHUNK
        kc = col // CHUNK
        in_band = (kc >= qc) & (kc <= qc + BAND_PAST // CHUNK)
        for h in range(ATT_HEADS_PER_STEP):
            rel = jnp.broadcast_to(rel_ref[h], (ATT_SUB, REL_LEN))
            toeplitz = pltpu.roll(rel, 0, 1, stride=1, stride_axis=0)[:, ATT_SUB:] * LOG2E
            for s in range(n_sub):
                key_pos = (i - 1) * BAND_PAST + s * ATT_SUB + col
                bias_ref[h, s] = jnp.where(in_band & (key_pos >= 0), toeplitz, NEG)

    units = [(h, s) for h in range(ATT_HEADS_PER_STEP) for s in range(n_sub)]

    def scores(h, s):
        lo, hi = s * ATT_SUB, (s + 1) * ATT_SUB
        cols = slice(h * ATT_HD, (h + 1) * ATT_HD)
        k = jnp.concatenate([kp_ref[lo:, cols], kc_ref[:hi, cols]], axis=0)
        return lax.dot_general(q_ref[lo:hi, cols], k, nt, preferred_element_type=F32) + bias_ref[h, s]

    def finish(h, s, sc):
        lo, hi = s * ATT_SUB, (s + 1) * ATT_SUB
        cols = slice(h * ATT_HD, (h + 1) * ATT_HD)
        v = jnp.concatenate([vp_ref[lo:, cols], vc_ref[:hi, cols]], axis=0)
        p = jnp.exp2(sc - jnp.max(sc, axis=-1, keepdims=True))
        denom = jnp.sum(p, axis=-1, keepdims=True)
        o = jnp.dot(p.astype(BF16), v, preferred_element_type=F32) / denom
        o_ref[lo:hi, cols] = o.astype(o_ref.dtype)

    sc = scores(*units[0])
    for n, unit in enumerate(units):
        sc_next = scores(*units[n + 1]) if n + 1 < len(units) else None
        finish(*unit, sc)
        sc = sc_next


def _band_rel(table):
    assert BAND_PAST + ATT_SUB - REL_LEN // 2 == MAX_REL and REL_LEN // 2 <= 2 * MAX_REL
    far = jnp.broadcast_to(table[:, 2 * MAX_REL:], (ATT_HEADS, REL_LEN // 2))
    near = table[:, 2 * MAX_REL - REL_LEN // 2 + 1:][:, ::-1]
    return jnp.concatenate([far, near], axis=1).astype(F32)[:, None, :]


def _band_attn(q, kv, table):
    t = q.shape[0]
    n_keys = BAND_PAST + ATT_SUB
    prev = lambda i: jnp.maximum(i - 1, 0)
    n_groups = ATT_HEADS // ATT_HEADS_PER_STEP
    blk = (BAND_PAST, ATT_HEADS_PER_STEP * ATT_HD)
    return pl.pallas_call(
        _band_attn_kernel,
        grid=(n_groups, t // BAND_PAST),
        in_specs=[
            pl.BlockSpec(blk, lambda h, i: (i, h)),
            pl.BlockSpec(blk, lambda h, i: (prev(i), h)),
            pl.BlockSpec(blk, lambda h, i: (i, h)),
            pl.BlockSpec(blk, lambda h, i: (prev(i), n_groups + h)),
            pl.BlockSpec(blk, lambda h, i: (i, n_groups + h)),
            pl.BlockSpec((ATT_HEADS_PER_STEP, 1, REL_LEN), lambda h, i: (h, 0, 0)),
        ],
        out_specs=pl.BlockSpec(blk, lambda h, i: (i, h)),
        out_shape=jax.ShapeDtypeStruct((t, D_MODEL), BF16),
        scratch_shapes=[pltpu.VMEM((ATT_HEADS_PER_STEP, BAND_PAST // ATT_SUB, ATT_SUB, n_keys), F32)],
        compiler_params=_params("parallel", "arbitrary"),
        name="band_attn",
    )(q, kv, kv, kv, kv, _band_rel(table))


def _step_attn_kernel(q_ref, kvn_ref, ck_ref, cv_ref, bp_ref, bn_ref, o_ref):
    nt = (((1,), (1,)), ((), ()))
    for h in range(ATT_HEADS):
        cols = slice(h * ATT_HD, (h + 1) * ATT_HD)
        q = q_ref[:, cols]
        k_new = kvn_ref[:, cols].astype(BF16)
        v_new = kvn_ref[:, D_MODEL + h * ATT_HD:D_MODEL + (h + 1) * ATT_HD].astype(BF16)
        k_past = ck_ref[0, :, h, :].astype(BF16)
        v_past = cv_ref[0, :, h, :].astype(BF16)
        sp = lax.dot_general(q, k_past, nt, preferred_element_type=F32) + bp_ref[h] * LOG2E
        sn = lax.dot_general(q, k_new, nt, preferred_element_type=F32) + bn_ref[h] * LOG2E
        mx = jnp.maximum(jnp.max(sp, axis=-1, keepdims=True), jnp.max(sn, axis=-1, keepdims=True))
        pp = jnp.exp2(sp - mx)
        pn = jnp.exp2(sn - mx)
        denom = jnp.sum(pp, axis=-1, keepdims=True) + jnp.sum(pn, axis=-1, keepdims=True)
        o = (jnp.dot(pp.astype(BF16), v_past, preferred_element_type=F32)
             + jnp.dot(pn.astype(BF16), v_new, preferred_element_type=F32)) / denom
        o_ref[:, cols] = o.astype(o_ref.dtype)


def _step_attn(q, kv_new, cache_k, cache_v, table, batch, seq):
    n_past = cache_k.shape[1]
    dist = n_past + jnp.arange(seq)[:, None] - jnp.arange(n_past + seq)[None, :]
    bias = table[:, jnp.clip(dist, -MAX_REL, MAX_REL) + MAX_REL].astype(F32)
    return pl.pallas_call(
        _step_attn_kernel,
        grid=(batch,),
        in_specs=[
            pl.BlockSpec((seq, D_MODEL), lambda b: (b, 0)),
            pl.BlockSpec((seq, 2 * D_MODEL), lambda b: (b, 0)),
            pl.BlockSpec((1, n_past, ATT_HEADS, ATT_HD), lambda b: (b, 0, 0, 0)),
            pl.BlockSpec((1, n_past, ATT_HEADS, ATT_HD), lambda b: (b, 0, 0, 0)),
            pl.BlockSpec((ATT_HEADS, seq, n_past), lambda b: (0, 0, 0)),
            pl.BlockSpec((ATT_HEADS, seq, seq), lambda b: (0, 0, 0)),
        ],
        out_specs=pl.BlockSpec((seq, D_MODEL), lambda b: (b, 0)),
        out_shape=jax.ShapeDtypeStruct((batch * seq, D_MODEL), BF16),
        compiler_params=_params("parallel"),
        name="step_attn",
    )(q, kv_new, cache_k, cache_v, bias[:, :, :n_past], bias[:, :, n_past:])


def _trunk(x, ada, ada_kv, gla_s0, past, wts):
    batch, seq, _ = x.shape
    m = batch * seq
    tm = min(ROW_TILE, m)
    x = x.reshape(m, D_MODEL)

    def rows(v):
        return v if batch == 1 else jnp.repeat(v, seq, axis=0)

    def mod(l, sub, which):
        return rows(ada[l][:, (3 * sub + which) * D_MODEL:(3 * sub + which + 1) * D_MODEL])

    def ffn(x, l, sub, idx):
        return _ffn(x, mod(l, sub, 0), mod(l, sub, 1), mod(l, sub, 2),
                    wts["ffn_a"], wts["ffn_u"], wts["ffn_down"], l, idx,
                    wts["ln_g"][l, sub][None], wts["ln_b"][l, sub][None], min(FFN_ROW_TILE, m))

    x = ffn(x, 0, 0, 0)
    proj, gk_low = _modmm(x, mod(0, 1, 0), mod(0, 1, 1), wts["a_in"], tm, BF16,
                          w_extra=wts["a_in_gk"])
    o, gla_state = _gla(proj, gk_low, wts["a_gk2"], wts["b_a_gk"], wts["g_a_norm"], gla_s0, batch, seq)
    x = _proj_ln(o, wts["a_out"], x, mod(0, 1, 2), wts["ln_g"][0, 1][None], wts["ln_b"][0, 1][None], tm)
    x = ffn(x, 0, 2, 1)

    kv_shift = rows(ada_kv[:, :D_MODEL])
    kv_scale = rows(ada_kv[:, D_MODEL:])
    if past is None:
        kv = _modmm(x, kv_shift, kv_scale, wts["kv"], tm, BF16)
        n_keep = min(BAND_PAST, seq)
        kv_out = _modmm(x[m - n_keep:], kv_shift, kv_scale, wts["kv"], n_keep, F32)
    else:
        kv_out = _modmm(x, kv_shift, kv_scale, wts["kv"], tm, F32)

    x = ffn(x, 1, 0, 0)
    q = _modmm(x, mod(1, 1, 0), mod(1, 1, 1), wts["b_q"], tm, BF16,
               out_scale=ATT_HD ** -0.5 * LOG2E)
    if past is None:
        o = _band_attn(q, kv, wts["rel_bias"])
    else:
        o = _step_attn(q, kv_out, past[0], past[1], wts["rel_bias"], batch, seq)
    x = _proj_ln(o, wts["b_out"], x, mod(1, 1, 2), wts["ln_g"][1, 1][None], wts["ln_b"][1, 1][None], tm)
    x = ffn(x, 1, 2, 1)

    n_rows = kv_out.shape[0] // batch
    k_out = kv_out[:, :D_MODEL].reshape(batch, n_rows, ATT_HEADS, ATT_HD)
    v_out = kv_out[:, D_MODEL:].reshape(batch, n_rows, ATT_HEADS, ATT_HD)
    return x.reshape(batch, seq, D_MODEL), gla_state[None], k_out, v_out


def kernel(x_prompt, x_sample, state_gla, cache_band_k, cache_band_v, c_prompt, c_sample, w_ada, b_ada, ln_g, ln_b, w_ffn_up, w_ffn_down, w_a_in, w_a_gk2, b_a_gk, g_a_norm, w_a_out, w_ada_kv, b_ada_kv, w_kv, w_b_q, rel_bias, w_b_out):
    assert DEPTH == 2 and w_a_in.shape[0] == 1 and w_b_q.shape[0] == 1
    n_prompt, n_sample = x_prompt.shape[0], x_sample.shape[0]
    assert n_prompt == 1 and n_prompt + n_sample <= ADA_ROWS

    ffn_a, ffn_u = _cast_split(w_ffn_up.reshape(DEPTH * 2 * D_MODEL, 2 * D_FF), (D_FF, D_FF), CAST_ROWS)
    ffn_down, = _cast_split(w_ffn_down.reshape(DEPTH * 2 * D_FF, D_MODEL), (D_MODEL,), FF_TILE)
    wts = {
        "ffn_a": ffn_a.reshape(DEPTH, 2, D_MODEL, D_FF),
        "ffn_u": ffn_u.reshape(DEPTH, 2, D_MODEL, D_FF),
        "ffn_down": ffn_down.reshape(DEPTH, 2, D_FF, D_MODEL),
        "a_in": _cast_split(w_a_in[0], (GLA_MAIN,), CAST_ROWS)[0],
        "a_in_gk": jnp.pad(w_a_in[0, :, GLA_MAIN:].astype(BF16), ((0, 0), (0, LANE - GLA_GATE_RANK))),
        "a_gk2": jnp.pad(w_a_gk2[0].astype(BF16), ((0, LANE - GLA_GATE_RANK), (0, 0))),
        "b_a_gk": b_a_gk[0][None],
        "g_a_norm": g_a_norm[0][None],
        "a_out": _cast_split(w_a_out[0], (D_MODEL,), CAST_ROWS)[0],
        "kv": _cast_split(w_kv, (2 * D_MODEL,), CAST_ROWS)[0],
        "b_q": _cast_split(w_b_q[0], (D_MODEL,), CAST_ROWS)[0],
        "b_out": _cast_split(w_b_out[0], (D_MODEL,), CAST_ROWS)[0],
        "rel_bias": rel_bias[0],
        "ln_g": ln_g,
        "ln_b": ln_b,
    }

    c_all = jnp.concatenate([c_prompt, c_sample], axis=0)
    c_all = jnp.pad(c_all, ((0, ADA_ROWS - c_all.shape[0]), (0, 0)))
    ada = _ada(c_all, w_ada, b_ada)
    ada_kv = _ada(c_all, w_ada_kv[None], b_ada_kv[None])[0]
    p_rows = slice(0, n_prompt)
    s_rows = slice(n_prompt, n_prompt + n_sample)

    gla_zero = jnp.zeros((n_prompt, GLA_HEADS, GLA_HK, GLA_HV), state_gla.dtype)
    y_p, s_p, k_p, v_p = _trunk(x_prompt, ada[:, p_rows], ada_kv[p_rows], gla_zero, None, wts)
    past = (cache_band_k, cache_band_v)
    y_s, s_s, k_s, v_s = _trunk(x_sample, ada[:, s_rows], ada_kv[s_rows], state_gla[0], past, wts)
    return (y_p, y_s, s_p, s_s, k_p, v_p, k_s, v_s)
```

```python
import functools
from typing import NamedTuple

import jax
import jax.numpy as jnp
from jax import lax
from jax.experimental import pallas as pl
from jax.experimental.pallas import tpu as pltpu

F32 = jnp.float32
BF16 = jnp.bfloat16

D_MODEL = 2048
DEPTH = 2
CHUNK = 64
GLA_HEADS = 4
GLA_DK = D_MODEL // 2
GLA_DV = D_MODEL
GLA_HK = GLA_DK // GLA_HEADS
GLA_HV = GLA_DV // GLA_HEADS
GLA_GATE_RANK = 16
GLA_GATE_NORMALIZER = 16.0
GLA_MAIN = 2 * GLA_DK + 2 * GLA_DV
ATT_HEADS = 16
ATT_HD = D_MODEL // ATT_HEADS
BAND_PAST = 8 * CHUNK
MAX_REL = 256
D_FF = 5504
ALPHA = (2 * DEPTH) ** 0.25
LN_EPS = 1e-5
RMS_EPS = 1e-6
NEG = -1e30
LOG2E = 1.4426950408889634

LANE = 128
FF_TILE = 512
FF_LAST = D_FF - (D_FF - 1) // FF_TILE * FF_TILE
ROW_TILE = 512
FFN_ROW_TILE = 1024
CAST_ROWS = 256
FFN_CAST_UP_ROWS = 16
FFN_CAST_DOWN_ROWS = 128
MODMM_COLS = 1024
ATT_SUB = 256
ATT_HEADS_PER_STEP = 4
REL_LEN = BAND_PAST + 2 * ATT_SUB
ADA_ROWS = 16
ADA_TILE = 1024
VMEM_LIMIT = 56 * 1024 * 1024


def _params(*sem):
    return pltpu.CompilerParams(dimension_semantics=sem, vmem_limit_bytes=VMEM_LIMIT)


def _layer_norm(z, g, b):
    mu = jnp.mean(z, axis=-1, keepdims=True)
    zc = z - mu
    var = jnp.mean(zc * zc, axis=-1, keepdims=True)
    return zc * lax.rsqrt(var + LN_EPS) * g + b


def _silu(a):
    return a * jax.nn.sigmoid(a)


def _ada_kernel(c_ref, w_ref, b_ref, o_ref):
    o_ref[...] = jnp.dot(_silu(c_ref[...]), w_ref[...], preferred_element_type=F32) + b_ref[...]


def _ada(c_all, w, b):
    n_l, _, n = w.shape
    return pl.pallas_call(
        _ada_kernel,
        grid=(n_l, n // ADA_TILE),
        in_specs=[
            pl.BlockSpec((ADA_ROWS, D_MODEL), lambda l, j: (0, 0)),
            pl.BlockSpec((None, D_MODEL, ADA_TILE), lambda l, j: (l, 0, j)),
            pl.BlockSpec((None, 1, ADA_TILE), lambda l, j: (l, 0, j)),
        ],
        out_specs=pl.BlockSpec((None, ADA_ROWS, ADA_TILE), lambda l, j: (l, 0, j)),
        out_shape=jax.ShapeDtypeStruct((n_l, ADA_ROWS, n), F32),
        compiler_params=_params("arbitrary", "arbitrary"),
        name="ada",
    )(c_all, w, b.reshape(n_l, 1, n))


def _mod_spec(mod_rows, tm):
    if mod_rows == 1:
        return pl.BlockSpec((1, D_MODEL), lambda i, *_: (0, 0))
    return pl.BlockSpec((tm, D_MODEL), lambda i, *_: (i, 0))


def _ffn_kernel(x_ref, sh_ref, sc_ref, gt_ref, wa_ref, wu_ref, wd_ref, g_ref, b_ref, o_ref, h_ref):
    j = pl.program_id(1)
    last = pl.num_programs(1) - 1

    @pl.when(j == 0)
    def _():
        h_ref[...] = (x_ref[...] * (1.0 + sc_ref[...]) + sh_ref[...]).astype(BF16)
        o_ref[...] = jnp.zeros_like(o_ref)

    def chunk(cols):
        h = h_ref[...]
        a = jnp.dot(h, wa_ref[:, :cols], preferred_element_type=F32)
        u = jnp.dot(h, wu_ref[:, :cols], preferred_element_type=F32)
        o_ref[...] += jnp.dot((_silu(a) * u).astype(BF16), wd_ref[:cols, :], preferred_element_type=F32)

    @pl.when(j < last)
    def _():
        chunk(FF_TILE)

    @pl.when(j == last)
    def _():
        chunk(FF_LAST)
        z = ALPHA * x_ref[...] + (1.0 + gt_ref[...]) * (0.5 * o_ref[...])
        o_ref[...] = _layer_norm(z, g_ref[...], b_ref[...])


def _ffn(x, shift, scale, gate, w_a, w_u, w_down, ln_g, ln_b, tm):
    m = x.shape[0]
    n_ff = pl.cdiv(D_FF, FF_TILE)
    mod = _mod_spec(shift.shape[0], tm)
    vec = pl.BlockSpec((1, D_MODEL), lambda i, j: (0, 0))
    up = pl.BlockSpec((D_MODEL, FF_TILE), lambda i, j: (0, j))
    return pl.pallas_call(
        _ffn_kernel,
        grid=(m // tm, n_ff),
        in_specs=[
            pl.BlockSpec((tm, D_MODEL), lambda i, j: (i, 0)),
            mod, mod, mod,
            up, up,
            pl.BlockSpec((FF_TILE, D_MODEL), lambda i, j: (j, 0)),
            vec, vec,
        ],
        out_specs=pl.BlockSpec((tm, D_MODEL), lambda i, j: (i, 0)),
        out_shape=jax.ShapeDtypeStruct((m, D_MODEL), F32),
        scratch_shapes=[pltpu.VMEM((tm, D_MODEL), BF16)],
        compiler_params=_params("parallel", "arbitrary"),
        name="ffn",
    )(x, shift, scale, gate, w_a, w_u, w_down, ln_g, ln_b)


def _cast_kernel(x_ref, *o_refs):
    off = 0
    for o_ref in o_refs:
        n = o_ref.shape[-1]
        o_ref[...] = x_ref[:, off:off + n].astype(o_ref.dtype)
        off += n


class _CastJob(NamedTuple):
    src: jax.Array
    layer: int
    idx: int
    rows: int
    widths: tuple

    @property
    def n_blocks(self):
        return self.src.shape[2] // self.rows

    def specs(self, step_of):
        r, n = self.src.shape[2:]
        assert r % self.rows == 0 and sum(self.widths) <= n and all(wd % LANE == 0 for wd in self.widths)
        blk = lambda *ids: jnp.minimum(step_of(*ids), self.n_blocks - 1)
        in_spec = pl.BlockSpec((None, None, self.rows, n), lambda *ids: (self.layer, self.idx, blk(*ids), 0))
        out_specs = [pl.BlockSpec((self.rows, wd), lambda *ids: (blk(*ids), 0)) for wd in self.widths]
        out_shape = [jax.ShapeDtypeStruct((r, wd), BF16) for wd in self.widths]
        return in_spec, out_specs, out_shape


def _side_casts(step, n_steps, jobs, in_refs, out_refs):
    pos = 0
    for job, x_ref in zip(jobs, in_refs):
        outs = out_refs[pos:pos + len(job.widths)]
        pos += len(job.widths)
        if job.n_blocks == n_steps:
            _cast_kernel(x_ref, *outs)
        else:
            pl.when(step < job.n_blocks)(functools.partial(_cast_kernel, x_ref, *outs))


def _job_specs(jobs, step_of, n_steps):
    in_specs, out_specs, out_shape = [], [], []
    for job in jobs:
        assert job.n_blocks <= n_steps
        i, o, s = job.specs(step_of)
        in_specs.append(i)
        out_specs += o
        out_shape += s
    return in_specs, out_specs, out_shape


def _cast_split(w, widths, rows, layer=0, idx=0):
    job = _CastJob(w, layer, idx, rows, tuple(widths))
    in_spec, out_specs, out_shape = job.specs(lambda i: i)
    return pl.pallas_call(
        _cast_kernel,
        grid=(job.n_blocks,),
        in_specs=[in_spec],
        out_specs=out_specs,
        out_shape=out_shape,
        compiler_params=_params("parallel"),
        name="cast",
    )(w)


def _ffn_cast_jobs(w_ffn_up, w_ffn_down, layer, idx):
    return (_CastJob(w_ffn_up, layer, idx, FFN_CAST_UP_ROWS, (D_FF, D_FF)),
            _CastJob(w_ffn_down, layer, idx, FFN_CAST_DOWN_ROWS, (D_MODEL,)))


def _modmm_kernel(x_ref, sh_ref, sc_ref, w_ref, *rest, has_extra, out_scale):
    if has_extra:
        w2_ref, o_ref, o2_ref, h_ref = rest
    else:
        o_ref, h_ref = rest
    h_ref[...] = (x_ref[...] * (1.0 + sc_ref[...]) + sh_ref[...]).astype(BF16)
    h = h_ref[...]
    if has_extra:
        o2_ref[...] = jnp.dot(h, w2_ref[...], preferred_element_type=F32).astype(o2_ref.dtype)
    for c in range(w_ref.shape[1] // MODMM_COLS):
        cols = slice(c * MODMM_COLS, (c + 1) * MODMM_COLS)
        y = jnp.dot(h, w_ref[:, cols], preferred_element_type=F32)
        if out_scale is not None:
            y = y * out_scale
        o_ref[:, cols] = y.astype(o_ref.dtype)


def _modmm(x, shift, scale, w, tm, out_dtype, w_extra=None, out_scale=None):
    m = x.shape[0]
    n = w.shape[1]
    mod = _mod_spec(shift.shape[0], tm)
    resident = functools.partial(pl.BlockSpec, index_map=lambda i: (0, 0), pipeline_mode=pl.Buffered(1))
    in_specs = [
        pl.BlockSpec((tm, D_MODEL), lambda i: (i, 0)),
        mod, mod,
        resident((D_MODEL, n)),
    ]
    out_specs = pl.BlockSpec((tm, n), lambda i: (i, 0))
    out_shape = jax.ShapeDtypeStruct((m, n), out_dtype)
    args = [x, shift, scale, w]
    if w_extra is not None:
        n2 = w_extra.shape[1]
        in_specs.append(resident((D_MODEL, n2)))
        out_specs = [out_specs, pl.BlockSpec((tm, n2), lambda i: (i, 0))]
        out_shape = [out_shape, jax.ShapeDtypeStruct((m, n2), out_dtype)]
        args.append(w_extra)
    return pl.pallas_call(
        functools.partial(_modmm_kernel, has_extra=w_extra is not None, out_scale=out_scale),
        grid=(m // tm,),
        in_specs=in_specs,
        out_specs=out_specs,
        out_shape=out_shape,
        scratch_shapes=[pltpu.VMEM((tm, D_MODEL), BF16)],
        compiler_params=_params("parallel"),
        name="modmm",
    )(*args)


def _proj_ln_kernel(a_ref, w_ref, x_ref, gt_ref, g_ref, b_ref, o_ref):
    y = jnp.dot(a_ref[...], w_ref[...], preferred_element_type=F32)
    z = ALPHA * x_ref[...] + (1.0 + gt_ref[...]) * y
    o_ref[...] = _layer_norm(z, g_ref[...], b_ref[...])


def _proj_ln(a, w, x, gate, ln_g, ln_b, tm):
    m = x.shape[0]
    vec = pl.BlockSpec((1, D_MODEL), lambda i: (0, 0))
    return pl.pallas_call(
        _proj_ln_kernel,
        grid=(m // tm,),
        in_specs=[
            pl.BlockSpec((tm, D_MODEL), lambda i: (i, 0)),
            pl.BlockSpec((D_MODEL, D_MODEL), lambda i: (0, 0)),
            pl.BlockSpec((tm, D_MODEL), lambda i: (i, 0)),
            _mod_spec(gate.shape[0], tm),
            vec, vec,
        ],
        out_specs=pl.BlockSpec((tm, D_MODEL), lambda i: (i, 0)),
        out_shape=jax.ShapeDtypeStruct((m, D_MODEL), F32),
        compiler_params=_params("parallel"),
        name="proj_ln",
    )(a, w, x, gate, ln_g, ln_b)


def _gla_kernel(q_ref, k_ref, v_ref, g_ref, gk_ref, w2_ref, bgk_ref, gn_ref, s0_ref, *rest,
                blk, n_blk, jobs, n_steps):
    n_jobs = len(jobs)
    o_ref, s_ref = rest[n_jobs:n_jobs + 2]
    step = (pl.program_id(0) * pl.num_programs(1) + pl.program_id(1)) * pl.num_programs(2) + pl.program_id(2)
    _side_casts(step, n_steps, jobs, rest[:n_jobs], rest[n_jobs + 2:])

    @pl.when(pl.program_id(2) == 0)
    def _():
        s_ref[...] = s0_ref[...]

    causal = (lax.broadcasted_iota(jnp.int32, (blk, blk), 0)
              >= lax.broadcasted_iota(jnp.int32, (blk, blk), 1))
    tril = jnp.broadcast_to(jnp.where(causal, 1.0, 0.0).astype(BF16), (n_blk, blk, blk))
    eye = (lax.broadcasted_iota(jnp.int32, (GLA_HK, GLA_HK), 0)
           == lax.broadcasted_iota(jnp.int32, (GLA_HK, GLA_HK), 1))
    tn = (((0,), (0,)), ((), ()))
    bnn = (((2,), (1,)), ((0,), (0,)))
    bnt = (((2,), (2,)), ((0,), (0,)))

    def blocks(t):
        return t.reshape(n_blk, blk, t.shape[-1])

    z = jnp.dot(gk_ref[...], w2_ref[...], preferred_element_type=F32) + bgk_ref[...]
    log_a = blocks((jnp.minimum(z, 0.0) - jnp.log1p(jnp.exp(-jnp.abs(z)))) * (1.0 / GLA_GATE_NORMALIZER))
    hi = log_a.astype(BF16)
    lo = (log_a - hi.astype(F32)).astype(BF16)
    b = (lax.dot_general(tril, hi, bnn, preferred_element_type=F32)
         + lax.dot_general(tril, lo, bnn, preferred_element_type=F32))
    b_last = b[:, blk - 1:blk, :]
    q = blocks(q_ref[...].astype(F32)) * (GLA_HK ** -0.5)
    k = blocks(k_ref[...].astype(F32))
    v = blocks(v_ref[...])
    q_t = (q * jnp.exp(b)).astype(BF16)
    k_t = (k * jnp.exp(-b)).astype(BF16)
    k_dec = (k * jnp.exp(b_last - b)).astype(BF16)
    scores = lax.dot_general(q_t, k_t, bnt, preferred_element_type=F32)
    scores = jnp.where(causal[None], scores, 0.0).astype(BF16)
    o_intra = lax.dot_general(scores, v, bnn, preferred_element_type=F32)
    decay = jnp.exp(b_last)

    update = [lax.dot_general(k_dec[c], v[c], tn, preferred_element_type=F32) for c in range(n_blk)]
    decay_col = [jnp.sum(jnp.where(eye, jnp.broadcast_to(decay[c], (GLA_HK, GLA_HK)), 0.0),
                         axis=1, keepdims=True) for c in range(n_blk)]
    state = s_ref[0, 0]
    starts = []
    for c in range(n_blk):
        starts.append(state.astype(BF16))
        state = decay_col[c] * state + update[c]
    s_ref[0, 0] = state
    o = o_intra + jnp.stack([jnp.dot(q_t[c], starts[c], preferred_element_type=F32) for c in range(n_blk)])
    on = o * lax.rsqrt(jnp.mean(o * o, axis=-1, keepdims=True) + RMS_EPS) * gn_ref[...]
    gate = _silu(g_ref[...].astype(F32)).reshape(o.shape)
    o_ref[...] = (on * gate).reshape(o_ref.shape).astype(o_ref.dtype)


def _gla(proj, gk_low, w_gk2, b_gk, g_norm, s0, batch, seq, jobs=()):
    blk = min(CHUNK, seq)
    rows = min(ROW_TILE, seq)
    n_steps = seq // rows
    m = batch * seq
    k_off = GLA_DK // GLA_HK
    v_off = 2 * GLA_DK // GLA_HV
    g_off = (2 * GLA_DK + GLA_DV) // GLA_HV
    row = lambda b, h, c: b * n_steps + c
    state_spec = pl.BlockSpec((1, 1, GLA_HK, GLA_HV), lambda b, h, c: (b, h, 0, 0))
    grid_steps = batch * GLA_HEADS * n_steps
    job_in, job_out, job_shape = _job_specs(jobs, lambda b, h, c: (b * GLA_HEADS + h) * n_steps + c, grid_steps)
    return pl.pallas_call(
        functools.partial(_gla_kernel, blk=blk, n_blk=rows // blk, jobs=jobs, n_steps=grid_steps),
        grid=(batch, GLA_HEADS, n_steps),
        in_specs=[
            pl.BlockSpec((rows, GLA_HK), lambda b, h, c: (row(b, h, c), h)),
            pl.BlockSpec((rows, GLA_HK), lambda b, h, c: (row(b, h, c), k_off + h)),
            pl.BlockSpec((rows, GLA_HV), lambda b, h, c: (row(b, h, c), v_off + h)),
            pl.BlockSpec((rows, GLA_HV), lambda b, h, c: (row(b, h, c), g_off + h)),
            pl.BlockSpec((rows, LANE), lambda b, h, c: (row(b, h, c), 0)),
            pl.BlockSpec((LANE, GLA_HK), lambda b, h, c: (0, h)),
            pl.BlockSpec((1, GLA_HK), lambda b, h, c: (0, h)),
            pl.BlockSpec((1, GLA_HV), lambda b, h, c: (0, 0)),
            state_spec,
            *job_in,
        ],
        out_specs=[
            pl.BlockSpec((rows, GLA_HV), lambda b, h, c: (row(b, h, c), h)),
            state_spec,
            *job_out,
        ],
        out_shape=[
            jax.ShapeDtypeStruct((m, GLA_DV), BF16),
            jax.ShapeDtypeStruct(s0.shape, F32),
            *job_shape,
        ],
        compiler_params=_params("arbitrary", "arbitrary", "arbitrary"),
        name="gla",
    )(proj, proj, proj, proj, gk_low, w_gk2, b_gk, g_norm, s0, *(job.src for job in jobs))


def _band_attn_kernel(q_ref, kp_ref, kc_ref, vp_ref, vc_ref, rel_ref, *rest, jobs, n_steps):
    n_jobs = len(jobs)
    o_ref, bias_ref = rest[n_jobs], rest[-1]
    i = pl.program_id(1)
    _side_casts(pl.program_id(0) * pl.num_programs(1) + i, n_steps, jobs, rest[:n_jobs], rest[n_jobs + 1:-1])
    nt = (((1,), (1,)), ((), ()))
    n_keys = BAND_PAST + ATT_SUB
    n_sub = BAND_PAST // ATT_SUB

    @pl.when(i <= 1)
    def _():
        col = lax.broadcasted_iota(jnp.int32, (ATT_SUB, n_keys), 1)
        qc = lax.broadcasted_iota(jnp.int32, (ATT_SUB, n_keys), 0) // CHUNK
        kc = col // CHUNK
        in_band = (kc >= qc) & (kc <= qc + BAND_PAST // CHUNK)
        for h in range(ATT_HEADS_PER_STEP):
            rel = jnp.broadcast_to(rel_ref[h], (ATT_SUB, REL_LEN))
            toeplitz = pltpu.roll(rel, 0, 1, stride=1, stride_axis=0)[:, ATT_SUB:] * LOG2E
            for s in range(n_sub):
                key_pos = (i - 1) * BAND_PAST + s * ATT_SUB + col
                bias_ref[h, s] = jnp.where(in_band & (key_pos >= 0), toeplitz, NEG)

    units = [(h, s) for h in range(ATT_HEADS_PER_STEP) for s in range(n_sub)]

    def scores(h, s):
        lo, hi = s * ATT_SUB, (s + 1) * ATT_SUB
        cols = slice(h * ATT_HD, (h + 1) * ATT_HD)
        k = jnp.concatenate([kp_ref[lo:, cols], kc_ref[:hi, cols]], axis=0)
        return lax.dot_general(q_ref[lo:hi, cols], k, nt, preferred_element_type=F32) + bias_ref[h, s]

    def finish(h, s, sc):
        lo, hi = s * ATT_SUB, (s + 1) * ATT_SUB
        cols = slice(h * ATT_HD, (h + 1) * ATT_HD)
        v = jnp.concatenate([vp_ref[lo:, cols], vc_ref[:hi, cols]], axis=0)
        p = jnp.exp2(sc - jnp.max(sc, axis=-1, keepdims=True))
        denom = jnp.sum(p, axis=-1, keepdims=True)
        o = jnp.dot(p.astype(BF16), v, preferred_element_type=F32) / denom
        o_ref[lo:hi, cols] = o.astype(o_ref.dtype)

    sc = scores(*units[0])
    for n, unit in enumerate(units):
        sc_next = scores(*units[n + 1]) if n + 1 < len(units) else None
        finish(*unit, sc)
        sc = sc_next


def _band_rel(table):
    assert BAND_PAST + ATT_SUB - REL_LEN // 2 == MAX_REL and REL_LEN // 2 <= 2 * MAX_REL
    far = jnp.broadcast_to(table[:, 2 * MAX_REL:], (ATT_HEADS, REL_LEN // 2))
    near = table[:, 2 * MAX_REL - REL_LEN // 2 + 1:][:, ::-1]
    return jnp.concatenate([far, near], axis=1).astype(F32)[:, None, :]


def _band_attn(q, kv, table, jobs=()):
    t = q.shape[0]
    n_keys = BAND_PAST + ATT_SUB
    prev = lambda i: jnp.maximum(i - 1, 0)
    n_groups = ATT_HEADS // ATT_HEADS_PER_STEP
    blk = (BAND_PAST, ATT_HEADS_PER_STEP * ATT_HD)
    n_blocks = t // BAND_PAST
    job_in, job_out, job_shape = _job_specs(jobs, lambda h, i: h * n_blocks + i, n_groups * n_blocks)
    return pl.pallas_call(
        functools.partial(_band_attn_kernel, jobs=jobs, n_steps=n_groups * n_blocks),
        grid=(n_groups, n_blocks),
        in_specs=[
            pl.BlockSpec(blk, lambda h, i: (i, h)),
            pl.BlockSpec(blk, lambda h, i: (prev(i), h)),
            pl.BlockSpec(blk, lambda h, i: (i, h)),
            pl.BlockSpec(blk, lambda h, i: (prev(i), n_groups + h)),
            pl.BlockSpec(blk, lambda h, i: (i, n_groups + h)),
            pl.BlockSpec((ATT_HEADS_PER_STEP, 1, REL_LEN), lambda h, i: (h, 0, 0)),
            *job_in,
        ],
        out_specs=[pl.BlockSpec(blk, lambda h, i: (i, h)), *job_out],
        out_shape=[jax.ShapeDtypeStruct((t, D_MODEL), BF16), *job_shape],
        scratch_shapes=[pltpu.VMEM((ATT_HEADS_PER_STEP, BAND_PAST // ATT_SUB, ATT_SUB, n_keys), F32)],
        compiler_params=_params("arbitrary", "arbitrary"),
        name="band_attn",
    )(q, kv, kv, kv, kv, _band_rel(table), *(job.src for job in jobs))


def _step_attn_kernel(q_ref, kvn_ref, ck_ref, cv_ref, bp_ref, bn_ref, o_ref):
    nt = (((1,), (1,)), ((), ()))
    for h in range(ATT_HEADS):
        cols = slice(h * ATT_HD, (h + 1) * ATT_HD)
        q = q_ref[:, cols]
        k_new = kvn_ref[:, cols].astype(BF16)
        v_new = kvn_ref[:, D_MODEL + h * ATT_HD:D_MODEL + (h + 1) * ATT_HD].astype(BF16)
        k_past = ck_ref[0, :, h, :].astype(BF16)
        v_past = cv_ref[0, :, h, :].astype(BF16)
        sp = lax.dot_general(q, k_past, nt, preferred_element_type=F32) + bp_ref[h] * LOG2E
        sn = lax.dot_general(q, k_new, nt, preferred_element_type=F32) + bn_ref[h] * LOG2E
        mx = jnp.maximum(jnp.max(sp, axis=-1, keepdims=True), jnp.max(sn, axis=-1, keepdims=True))
        pp = jnp.exp2(sp - mx)
        pn = jnp.exp2(sn - mx)
        denom = jnp.sum(pp, axis=-1, keepdims=True) + jnp.sum(pn, axis=-1, keepdims=True)
        o = (jnp.dot(pp.astype(BF16), v_past, preferred_element_type=F32)
             + jnp.dot(pn.astype(BF16), v_new, preferred_element_type=F32)) / denom
        o_ref[:, cols] = o.astype(o_ref.dtype)


def _step_attn(q, kv_new, cache_k, cache_v, table, batch, seq):
    n_past = cache_k.shape[1]
    dist = n_past + jnp.arange(seq)[:, None] - jnp.arange(n_past + seq)[None, :]
    bias = table[:, jnp.clip(dist, -MAX_REL, MAX_REL) + MAX_REL].astype(F32)
    return pl.pallas_call(
        _step_attn_kernel,
        grid=(batch,),
        in_specs=[
            pl.BlockSpec((seq, D_MODEL), lambda b: (b, 0)),
            pl.BlockSpec((seq, 2 * D_MODEL), lambda b: (b, 0)),
            pl.BlockSpec((1, n_past, ATT_HEADS, ATT_HD), lambda b: (b, 0, 0, 0)),
            pl.BlockSpec((1, n_past, ATT_HEADS, ATT_HD), lambda b: (b, 0, 0, 0)),
            pl.BlockSpec((ATT_HEADS, seq, n_past), lambda b: (0, 0, 0)),
            pl.BlockSpec((ATT_HEADS, seq, seq), lambda b: (0, 0, 0)),
        ],
        out_specs=pl.BlockSpec((seq, D_MODEL), lambda b: (b, 0)),
        out_shape=jax.ShapeDtypeStruct((batch * seq, D_MODEL), BF16),
        compiler_params=_params("parallel"),
        name="step_attn",
    )(q, kv_new, cache_k, cache_v, bias[:, :, :n_past], bias[:, :, n_past:])


def _trunk(x, ada, ada_kv, gla_s0, past, wts):
    batch, seq, _ = x.shape
    m = batch * seq
    tm = min(ROW_TILE, m)
    x = x.reshape(m, D_MODEL)

    def rows(v):
        return v if batch == 1 else jnp.repeat(v, seq, axis=0)

    def mod(l, sub, which):
        return rows(ada[l][:, (3 * sub + which) * D_MODEL:(3 * sub + which + 1) * D_MODEL])

    def ffn(x, l, sub, idx):
        return _ffn(x, mod(l, sub, 0), mod(l, sub, 1), mod(l, sub, 2), *wts["ffn"][l, idx],
                    wts["ln_g"][l, sub][None], wts["ln_b"][l, sub][None], min(FFN_ROW_TILE, m))

    def cast_pending(keys):
        pending = [key for key in keys if key not in wts["ffn"]]
        return pending, tuple(job for key in pending for job in _ffn_cast_jobs(*wts["ffn_f32"], *key))

    def keep_cast(pending, cast):
        for n, key in enumerate(pending):
            wts["ffn"][key] = tuple(cast[3 * n:3 * n + 3])

    x = ffn(x, 0, 0, 0)
    proj, gk_low = _modmm(x, mod(0, 1, 0), mod(0, 1, 1), wts["a_in"], tm, BF16,
                          w_extra=wts["a_in_gk"])
    pending, jobs = cast_pending(((0, 1), (1, 0)))
    o, gla_state, *cast = _gla(proj, gk_low, wts["a_gk2"], wts["b_a_gk"], wts["g_a_norm"], gla_s0, batch, seq, jobs)
    keep_cast(pending, cast)
    x = _proj_ln(o, wts["a_out"], x, mod(0, 1, 2), wts["ln_g"][0, 1][None], wts["ln_b"][0, 1][None], tm)
    x = ffn(x, 0, 2, 1)

    kv_shift = rows(ada_kv[:, :D_MODEL])
    kv_scale = rows(ada_kv[:, D_MODEL:])
    if past is None:
        kv = _modmm(x, kv_shift, kv_scale, wts["kv"], tm, BF16)
        n_keep = min(BAND_PAST, seq)
        kv_out = _modmm(x[m - n_keep:], kv_shift, kv_scale, wts["kv"], n_keep, F32)
    else:
        kv_out = _modmm(x, kv_shift, kv_scale, wts["kv"], tm, F32)

    x = ffn(x, 1, 0, 0)
    q = _modmm(x, mod(1, 1, 0), mod(1, 1, 1), wts["b_q"], tm, BF16,
               out_scale=ATT_HD ** -0.5 * LOG2E)
    if past is None:
        pending, jobs = cast_pending(((1, 1),))
        o, *cast = _band_attn(q, kv, wts["rel_bias"], jobs)
        keep_cast(pending, cast)
    else:
        o = _step_attn(q, kv_out, past[0], past[1], wts["rel_bias"], batch, seq)
    x = _proj_ln(o, wts["b_out"], x, mod(1, 1, 2), wts["ln_g"][1, 1][None], wts["ln_b"][1, 1][None], tm)
    x = ffn(x, 1, 2, 1)

    n_rows = kv_out.shape[0] // batch
    k_out = kv_out[:, :D_MODEL].reshape(batch, n_rows, ATT_HEADS, ATT_HD)
    v_out = kv_out[:, D_MODEL:].reshape(batch, n_rows, ATT_HEADS, ATT_HD)
    return x.reshape(batch, seq, D_MODEL), gla_state[None], k_out, v_out


def kernel(x_prompt, x_sample, state_gla, cache_band_k, cache_band_v, c_prompt, c_sample, w_ada, b_ada, ln_g, ln_b, w_ffn_up, w_ffn_down, w_a_in, w_a_gk2, b_a_gk, g_a_norm, w_a_out, w_ada_kv, b_ada_kv, w_kv, w_b_q, rel_bias, w_b_out):
    assert DEPTH == 2 and w_a_in.shape[0] == 1 and w_b_q.shape[0] == 1
    n_prompt, n_sample = x_prompt.shape[0], x_sample.shape[0]
    assert n_prompt == 1 and n_prompt + n_sample <= ADA_ROWS

    lead = lambda w: w.reshape((1, 1) + w.shape[-2:])
    wts = {
        "ffn_f32": (w_ffn_up, w_ffn_down),
        "ffn": {(0, 0): (*_cast_split(w_ffn_up, (D_FF, D_FF), CAST_ROWS),
                         *_cast_split(w_ffn_down, (D_MODEL,), D_FF // 8))},
        "a_in": _cast_split(lead(w_a_in), (GLA_MAIN,), CAST_ROWS)[0],
        "a_in_gk": jnp.pad(w_a_in[0, :, GLA_MAIN:].astype(BF16), ((0, 0), (0, LANE - GLA_GATE_RANK))),
        "a_gk2": jnp.pad(w_a_gk2[0].astype(BF16), ((0, LANE - GLA_GATE_RANK), (0, 0))),
        "b_a_gk": b_a_gk[0][None],
        "g_a_norm": g_a_norm[0][None],
        "a_out": _cast_split(lead(w_a_out), (D_MODEL,), CAST_ROWS)[0],
        "kv": _cast_split(lead(w_kv), (2 * D_MODEL,), CAST_ROWS)[0],
        "b_q": _cast_split(lead(w_b_q), (D_MODEL,), CAST_ROWS)[0],
        "b_out": _cast_split(lead(w_b_out), (D_MODEL,), CAST_ROWS)[0],
        "rel_bias": rel_bias[0],
        "ln_g": ln_g,
        "ln_b": ln_b,
    }

    c_all = jnp.concatenate([c_prompt, c_sample], axis=0)
    c_all = jnp.pad(c_all, ((0, ADA_ROWS - c_all.shape[0]), (0, 0)))
    ada = _ada(c_all, w_ada, b_ada)
    ada_kv = _ada(c_all, w_ada_kv[None], b_ada_kv[None])[0]
    p_rows = slice(0, n_prompt)
    s_rows = slice(n_prompt, n_prompt + n_sample)

    gla_zero = jnp.zeros((n_prompt, GLA_HEADS, GLA_HK, GLA_HV), state_gla.dtype)
    y_p, s_p, k_p, v_p = _trunk(x_prompt, ada[:, p_rows], ada_kv[p_rows], gla_zero, None, wts)
    past = (cache_band_k, cache_band_v)
    y_s, s_s, k_s, v_s = _trunk(x_sample, ada[:, s_rows], ada_kv[s_rows], state_gla[0], past, wts)
    return (y_p, y_s, s_p, s_s, k_p, v_p, k_s, v_s)
```

```python
import functools
from typing import NamedTuple

import jax
import jax.numpy as jnp
from jax import lax
from jax.experimental import pallas as pl
from jax.experimental.pallas import tpu as pltpu

F32 = jnp.float32
BF16 = jnp.bfloat16

D_MODEL = 2048
DEPTH = 2
CHUNK = 64
GLA_HEADS = 4
GLA_DK = D_MODEL // 2
GLA_DV = D_MODEL
GLA_HK = GLA_DK // GLA_HEADS
GLA_HV = GLA_DV // GLA_HEADS
GLA_GATE_RANK = 16
GLA_GATE_NORMALIZER = 16.0
GLA_MAIN = 2 * GLA_DK + 2 * GLA_DV
ATT_HEADS = 16
ATT_HD = D_MODEL // ATT_HEADS
BAND_PAST = 8 * CHUNK
MAX_REL = 256
D_FF = 5504
ALPHA = (2 * DEPTH) ** 0.25
LN_EPS = 1e-5
RMS_EPS = 1e-6
NEG = -1e30
LOG2E = 1.4426950408889634

LANE = 128
FF_TILE = 512
FF_LAST = D_FF - (D_FF - 1) // FF_TILE * FF_TILE
ROW_TILE = 512
FFN_ROW_TILE = 1024
A_IN_CAST_ROWS = 560
CAST_ROWS = 256
FFN_CAST_UP_ROWS = 16
FFN_CAST_DOWN_ROWS = 128
FFN_TAIL_SUB = 256
PROJ_SUB = 128
MODMM_COLS = 1024
ATT_SUB = 256
ATT_HEADS_PER_STEP = 4
REL_LEN = BAND_PAST + 2 * ATT_SUB
ADA_ROWS = 16
ADA_TILE = 1024
VMEM_LIMIT = 56 * 1024 * 1024


def _params(*sem):
    return pltpu.CompilerParams(dimension_semantics=sem, vmem_limit_bytes=VMEM_LIMIT)


def _layer_norm(z, g, b):
    mu = jnp.mean(z, axis=-1, keepdims=True)
    zc = z - mu
    var = jnp.mean(zc * zc, axis=-1, keepdims=True)
    return zc * lax.rsqrt(var + LN_EPS) * g + b


def _silu(a):
    return a * jax.nn.sigmoid(a)


def _ada_kernel(c_ref, w_ref, b_ref, o_ref):
    o_ref[...] = jnp.dot(_silu(c_ref[...]), w_ref[...], preferred_element_type=F32) + b_ref[...]


def _ada(c_all, w, b):
    n_l, _, n = w.shape
    return pl.pallas_call(
        _ada_kernel,
        grid=(n_l, n // ADA_TILE),
        in_specs=[
            pl.BlockSpec((ADA_ROWS, D_MODEL), lambda l, j: (0, 0)),
            pl.BlockSpec((None, D_MODEL, ADA_TILE), lambda l, j: (l, 0, j)),
            pl.BlockSpec((None, 1, ADA_TILE), lambda l, j: (l, 0, j)),
        ],
        out_specs=pl.BlockSpec((None, ADA_ROWS, ADA_TILE), lambda l, j: (l, 0, j)),
        out_shape=jax.ShapeDtypeStruct((n_l, ADA_ROWS, n), F32),
        compiler_params=_params("arbitrary", "arbitrary"),
        name="ada",
    )(c_all, w, b.reshape(n_l, 1, n))


def _mod_spec(mod_rows, tm):
    if mod_rows == 1:
        return pl.BlockSpec((1, D_MODEL), lambda i, *_: (0, 0))
    return pl.BlockSpec((tm, D_MODEL), lambda i, *_: (i, 0))


def _ffn_kernel(x_ref, sh_ref, sc_ref, gt_ref, wa_ref, wu_ref, wd_ref, g_ref, b_ref, o_ref, h_ref):
    j = pl.program_id(1)
    last = pl.num_programs(1) - 1

    @pl.when(j == 0)
    def _():
        h_ref[...] = (x_ref[...] * (1.0 + sc_ref[...]) + sh_ref[...]).astype(BF16)
        o_ref[...] = jnp.zeros_like(o_ref)

    def chunk(cols, rows=slice(None)):
        h = h_ref[rows, :]
        a = jnp.dot(h, wa_ref[:, :cols], preferred_element_type=F32)
        u = jnp.dot(h, wu_ref[:, :cols], preferred_element_type=F32)
        o_ref[rows, :] += jnp.dot((_silu(a) * u).astype(BF16), wd_ref[:cols, :], preferred_element_type=F32)

    @pl.when(j < last)
    def _():
        chunk(FF_TILE)

    @pl.when(j == last)
    def _():
        sub = min(FFN_TAIL_SUB, o_ref.shape[0])
        for r in range(o_ref.shape[0] // sub):
            rows = slice(r * sub, (r + 1) * sub)
            chunk(FF_LAST, rows)
            gt = gt_ref[...] if gt_ref.shape[0] == 1 else gt_ref[rows, :]
            z = ALPHA * x_ref[rows, :] + (0.5 * (1.0 + gt)) * o_ref[rows, :]
            o_ref[rows, :] = _layer_norm(z, g_ref[...], b_ref[...])


def _ffn(x, shift, scale, gate, w_a, w_u, w_down, ln_g, ln_b, tm):
    m = x.shape[0]
    n_ff = pl.cdiv(D_FF, FF_TILE)
    mod = _mod_spec(shift.shape[0], tm)
    vec = pl.BlockSpec((1, D_MODEL), lambda i, j: (0, 0))
    up = pl.BlockSpec((D_MODEL, FF_TILE), lambda i, j: (0, j))
    return pl.pallas_call(
        _ffn_kernel,
        grid=(m // tm, n_ff),
        in_specs=[
            pl.BlockSpec((tm, D_MODEL), lambda i, j: (i, 0)),
            mod, mod, mod,
            up, up,
            pl.BlockSpec((FF_TILE, D_MODEL), lambda i, j: (j, 0)),
            vec, vec,
        ],
        out_specs=pl.BlockSpec((tm, D_MODEL), lambda i, j: (i, 0)),
        out_shape=jax.ShapeDtypeStruct((m, D_MODEL), F32),
        scratch_shapes=[pltpu.VMEM((tm, D_MODEL), BF16)],
        compiler_params=_params("parallel", "arbitrary"),
        name="ffn",
    )(x, shift, scale, gate, w_a, w_u, w_down, ln_g, ln_b)


def _cast_kernel(x_ref, *o_refs):
    off = 0
    for o_ref in o_refs:
        n = o_ref.shape[-1]
        o_ref[...] = x_ref[:, off:off + n].astype(o_ref.dtype)
        off += n


class _CastJob(NamedTuple):
    src: jax.Array
    layer: int
    idx: int
    rows: int
    widths: tuple

    @property
    def n_blocks(self):
        return self.src.shape[2] // self.rows

    def specs(self, step_of):
        r, n = self.src.shape[2:]
        assert r % self.rows == 0 and sum(self.widths) <= n and all(wd % LANE == 0 for wd in self.widths)
        blk = lambda *ids: jnp.minimum(step_of(*ids), self.n_blocks - 1)
        in_spec = pl.BlockSpec((None, None, self.rows, n), lambda *ids: (self.layer, self.idx, blk(*ids), 0))
        out_specs = [pl.BlockSpec((self.rows, wd), lambda *ids: (blk(*ids), 0)) for wd in self.widths]
        out_shape = [jax.ShapeDtypeStruct((r, wd), BF16) for wd in self.widths]
        return in_spec, out_specs, out_shape


def _side_casts(step, n_steps, jobs, in_refs, out_refs):
    pos = 0
    for job, x_ref in zip(jobs, in_refs):
        outs = out_refs[pos:pos + len(job.widths)]
        pos += len(job.widths)
        if job.n_blocks == n_steps:
            _cast_kernel(x_ref, *outs)
        else:
            pl.when(step < job.n_blocks)(functools.partial(_cast_kernel, x_ref, *outs))


def _job_specs(jobs, step_of, n_steps):
    in_specs, out_specs, out_shape = [], [], []
    for job in jobs:
        assert job.n_blocks <= n_steps
        i, o, s = job.specs(step_of)
        in_specs.append(i)
        out_specs += o
        out_shape += s
    return in_specs, out_specs, out_shape


def _cast_split(w, widths, rows, layer=0, idx=0):
    job = _CastJob(w, layer, idx, rows, tuple(widths))
    in_spec, out_specs, out_shape = job.specs(lambda i: i)
    return pl.pallas_call(
        _cast_kernel,
        grid=(job.n_blocks,),
        in_specs=[in_spec],
        out_specs=out_specs,
        out_shape=out_shape,
        compiler_params=_params("parallel"),
        name="cast",
    )(w)


def _ffn_cast_jobs(w_ffn_up, w_ffn_down, layer, idx):
    return (_CastJob(w_ffn_up, layer, idx, FFN_CAST_UP_ROWS, (D_FF, D_FF)),
            _CastJob(w_ffn_down, layer, idx, FFN_CAST_DOWN_ROWS, (D_MODEL,)))


def _modmm_kernel(x_ref, sh_ref, sc_ref, w_ref, o_ref, *rest, w_is_t, out_scale):
    h_ref = rest[-1]
    h_ref[...] = (x_ref[...] * (1.0 + sc_ref[...]) + sh_ref[...]).astype(BF16)
    h = h_ref[...]
    nt = (((1,), (1,)), ((), ()))

    def project(lo, hi):
        if w_is_t:
            y = lax.dot_general(h, w_ref[lo:hi, :], nt, preferred_element_type=F32)
        else:
            y = jnp.dot(h, w_ref[:, lo:hi], preferred_element_type=F32)
        return y if out_scale is None else y * out_scale

    n = o_ref.shape[1]
    for lo in range(0, n, MODMM_COLS):
        o_ref[:, lo:lo + MODMM_COLS] = project(lo, lo + MODMM_COLS).astype(o_ref.dtype)
    for extra_ref in rest[:-1]:
        extra_ref[...] = project(n, n + extra_ref.shape[1]).astype(extra_ref.dtype)
        n += extra_ref.shape[1]


def _modmm(x, shift, scale, w, tm, out_dtype, widths=None, w_is_t=False, out_scale=None):
    m = x.shape[0]
    widths = widths or (w.shape[0] if w_is_t else w.shape[1],)
    assert widths[0] % MODMM_COLS == 0
    mod = _mod_spec(shift.shape[0], tm)
    out = pl.pallas_call(
        functools.partial(_modmm_kernel, w_is_t=w_is_t, out_scale=out_scale),
        grid=(m // tm,),
        in_specs=[
            pl.BlockSpec((tm, D_MODEL), lambda i: (i, 0)),
            mod, mod,
            pl.BlockSpec(w.shape, lambda i: (0, 0), pipeline_mode=pl.Buffered(1)),
        ],
        out_specs=[pl.BlockSpec((tm, wd), lambda i: (i, 0)) for wd in widths],
        out_shape=[jax.ShapeDtypeStruct((m, wd), out_dtype) for wd in widths],
        scratch_shapes=[pltpu.VMEM((tm, D_MODEL), BF16)],
        compiler_params=_params("parallel"),
        name="modmm",
    )(x, shift, scale, w)
    return out if len(widths) > 1 else out[0]


def _proj_ln_kernel(a_ref, w_ref, x_ref, gt_ref, g_ref, b_ref, o_ref):
    sub = min(PROJ_SUB, a_ref.shape[0])
    n_sub = a_ref.shape[0] // sub
    dot = lambda r: jnp.dot(a_ref[r * sub:(r + 1) * sub, :], w_ref[...], preferred_element_type=F32)
    y = dot(0)
    for r in range(n_sub):
        y_next = dot(r + 1) if r + 1 < n_sub else None
        rows = slice(r * sub, (r + 1) * sub)
        gt = gt_ref[...] if gt_ref.shape[0] == 1 else gt_ref[rows, :]
        z = ALPHA * x_ref[rows, :] + (1.0 + gt) * y
        o_ref[rows, :] = _layer_norm(z, g_ref[...], b_ref[...])
        y = y_next


def _proj_ln(a, w, x, gate, ln_g, ln_b, tm):
    m = x.shape[0]
    vec = pl.BlockSpec((1, D_MODEL), lambda i: (0, 0))
    return pl.pallas_call(
        _proj_ln_kernel,
        grid=(m // tm,),
        in_specs=[
            pl.BlockSpec((tm, D_MODEL), lambda i: (i, 0)),
            pl.BlockSpec((D_MODEL, D_MODEL), lambda i: (0, 0)),
            pl.BlockSpec((tm, D_MODEL), lambda i: (i, 0)),
            _mod_spec(gate.shape[0], tm),
            vec, vec,
        ],
        out_specs=pl.BlockSpec((tm, D_MODEL), lambda i: (i, 0)),
        out_shape=jax.ShapeDtypeStruct((m, D_MODEL), F32),
        compiler_params=_params("parallel"),
        name="proj_ln",
    )(a, w, x, gate, ln_g, ln_b)


def _gla_kernel(q_ref, k_ref, v_ref, g_ref, gk_ref, w2_ref, bgk_ref, gn_ref, s0_ref, *rest,
                blk, n_blk, jobs, n_steps):
    n_jobs = len(jobs)
    o_ref, s_ref = rest[n_jobs:n_jobs + 2]
    step = (pl.program_id(0) * pl.num_programs(1) + pl.program_id(1)) * pl.num_programs(2) + pl.program_id(2)
    _side_casts(step, n_steps, jobs, rest[:n_jobs], rest[n_jobs + 2:])

    @pl.when(pl.program_id(2) == 0)
    def _():
        s_ref[...] = s0_ref[...]

    causal = (lax.broadcasted_iota(jnp.int32, (blk, blk), 0)
              >= lax.broadcasted_iota(jnp.int32, (blk, blk), 1))
    tril = jnp.broadcast_to(jnp.where(causal, 1.0, 0.0).astype(BF16), (n_blk, blk, blk))
    eye = (lax.broadcasted_iota(jnp.int32, (GLA_HK, GLA_HK), 0)
           == lax.broadcasted_iota(jnp.int32, (GLA_HK, GLA_HK), 1))
    tn = (((0,), (0,)), ((), ()))
    bnn = (((2,), (1,)), ((0,), (0,)))
    bnt = (((2,), (2,)), ((0,), (0,)))

    def blocks(t):
        return t.reshape(n_blk, blk, t.shape[-1])

    z = jnp.dot(gk_ref[...], w2_ref[...], preferred_element_type=F32) + bgk_ref[...]
    log_a = blocks((jnp.minimum(z, 0.0) - jnp.log1p(jnp.exp(-jnp.abs(z)))) * (1.0 / GLA_GATE_NORMALIZER))
    hi = log_a.astype(BF16)
    lo = (log_a - hi.astype(F32)).astype(BF16)
    b = (lax.dot_general(tril, hi, bnn, preferred_element_type=F32)
         + lax.dot_general(tril, lo, bnn, preferred_element_type=F32))
    b_last = b[:, blk - 1:blk, :]
    q = blocks(q_ref[...].astype(F32)) * (GLA_HK ** -0.5)
    k = blocks(k_ref[...].astype(F32))
    v = blocks(v_ref[...])
    q_t = (q * jnp.exp(b)).astype(BF16)
    k_t = (k * jnp.exp(-b)).astype(BF16)
    k_dec = (k * jnp.exp(b_last - b)).astype(BF16)
    scores = lax.dot_general(q_t, k_t, bnt, preferred_element_type=F32)
    scores = jnp.where(causal[None], scores, 0.0).astype(BF16)
    o_intra = lax.dot_general(scores, v, bnn, preferred_element_type=F32)
    decay = jnp.exp(b_last)

    update = [lax.dot_general(k_dec[c], v[c], tn, preferred_element_type=F32) for c in range(n_blk)]
    decay_col = [jnp.sum(jnp.where(eye, jnp.broadcast_to(decay[c], (GLA_HK, GLA_HK)), 0.0),
                         axis=1, keepdims=True) for c in range(n_blk)]
    state = s_ref[0, 0]
    starts = []
    for c in range(n_blk):
        starts.append(state.astype(BF16))
        state = decay_col[c] * state + update[c]
    s_ref[0, 0] = state
    o = o_intra + jnp.stack([jnp.dot(q_t[c], starts[c], preferred_element_type=F32) for c in range(n_blk)])
    on = o * lax.rsqrt(jnp.mean(o * o, axis=-1, keepdims=True) + RMS_EPS) * gn_ref[...]
    gate = _silu(g_ref[...].astype(F32)).reshape(o.shape)
    o_ref[...] = (on * gate).reshape(o_ref.shape).astype(o_ref.dtype)


def _gla(proj, gk_low, w_gk2, b_gk, g_norm, s0, batch, seq, jobs=()):
    blk = min(CHUNK, seq)
    rows = min(ROW_TILE, seq)
    n_steps = seq // rows
    m = batch * seq
    k_off = GLA_DK // GLA_HK
    v_off = 2 * GLA_DK // GLA_HV
    g_off = (2 * GLA_DK + GLA_DV) // GLA_HV
    row = lambda b, h, c: b * n_steps + c
    state_spec = pl.BlockSpec((1, 1, GLA_HK, GLA_HV), lambda b, h, c: (b, h, 0, 0))
    grid_steps = batch * GLA_HEADS * n_steps
    job_in, job_out, job_shape = _job_specs(jobs, lambda b, h, c: (b * GLA_HEADS + h) * n_steps + c, grid_steps)
    return pl.pallas_call(
        functools.partial(_gla_kernel, blk=blk, n_blk=rows // blk, jobs=jobs, n_steps=grid_steps),
        grid=(batch, GLA_HEADS, n_steps),
        in_specs=[
            pl.BlockSpec((rows, GLA_HK), lambda b, h, c: (row(b, h, c), h)),
            pl.BlockSpec((rows, GLA_HK), lambda b, h, c: (row(b, h, c), k_off + h)),
            pl.BlockSpec((rows, GLA_HV), lambda b, h, c: (row(b, h, c), v_off + h)),
            pl.BlockSpec((rows, GLA_HV), lambda b, h, c: (row(b, h, c), g_off + h)),
            pl.BlockSpec((rows, GLA_GATE_RANK), lambda b, h, c: (row(b, h, c), 0)),
            pl.BlockSpec((GLA_GATE_RANK, GLA_HK), lambda b, h, c: (0, h)),
            pl.BlockSpec((1, GLA_HK), lambda b, h, c: (0, h)),
            pl.BlockSpec((1, GLA_HV), lambda b, h, c: (0, 0)),
            state_spec,
            *job_in,
        ],
        out_specs=[
            pl.BlockSpec((rows, GLA_HV), lambda b, h, c: (row(b, h, c), h)),
            state_spec,
            *job_out,
        ],
        out_shape=[
            jax.ShapeDtypeStruct((m, GLA_DV), BF16),
            jax.ShapeDtypeStruct(s0.shape, F32),
            *job_shape,
        ],
        compiler_params=_params("arbitrary", "arbitrary", "arbitrary"),
        name="gla",
    )(proj, proj, proj, proj, gk_low, w_gk2, b_gk, g_norm, s0, *(job.src for job in jobs))


def _band_attn_kernel(q_ref, kp_ref, kc_ref, vp_ref, vc_ref, rel_ref, *rest, jobs, n_steps):
    n_jobs = len(jobs)
    o_ref, bias_ref = rest[n_jobs], rest[-1]
    i = pl.program_id(1)
    _side_casts(pl.program_id(0) * pl.num_programs(1) + i, n_steps, jobs, rest[:n_jobs], rest[n_jobs + 1:-1])
    nt = (((1,), (1,)), ((), ()))
    n_keys = BAND_PAST + ATT_SUB
    n_sub = BAND_PAST // ATT_SUB

    @pl.when(i <= 1)
    def _():
        col = lax.broadcasted_iota(jnp.int32, (ATT_SUB, n_keys), 1)
        qc = lax.broadcasted_iota(jnp.int32, (ATT_SUB, n_keys), 0) // CHUNK
        kc = col // CHUNK
        in_band = (kc >= qc) & (kc <= qc + BAND_PAST // CHUNK)
        for h in range(ATT_HEADS_PER_STEP):
            rel = jnp.broadcast_to(rel_ref[h], (ATT_SUB, REL_LEN))
            toeplitz = pltpu.roll(rel, 0, 1, stride=1, stride_axis=0)[:, ATT_SUB:] * LOG2E
            for s in range(n_sub):
                key_pos = (i - 1) * BAND_PAST + s * ATT_SUB + col
                bias_ref[h, s] = jnp.where(in_band & (key_pos >= 0), toeplitz, NEG)

    units = [(h, s) for h in range(ATT_HEADS_PER_STEP) for s in range(n_sub)]

    def scores(h, s):
        lo, hi = s * ATT_SUB, (s + 1) * ATT_SUB
        cols = slice(h * ATT_HD, (h + 1) * ATT_HD)
        k = jnp.concatenate([kp_ref[lo:, cols], kc_ref[:hi, cols]], axis=0)
        return lax.dot_general(q_ref[lo:hi, cols], k, nt, preferred_element_type=F32) + bias_ref[h, s]

    def finish(h, s, sc):
        lo, hi = s * ATT_SUB, (s + 1) * ATT_SUB
        cols = slice(h * ATT_HD, (h + 1) * ATT_HD)
        v = jnp.concatenate([vp_ref[lo:, cols], vc_ref[:hi, cols]], axis=0)
        p = jnp.exp2(sc - jnp.max(sc, axis=-1, keepdims=True))
        denom = jnp.sum(p, axis=-1, keepdims=True)
        o = jnp.dot(p.astype(BF16), v, preferred_element_type=F32) / denom
        o_ref[lo:hi, cols] = o.astype(o_ref.dtype)

    sc = scores(*units[0])
    for n, unit in enumerate(units):
        sc_next = scores(*units[n + 1]) if n + 1 < len(units) else None
        finish(*unit, sc)
        sc = sc_next


def _band_rel(table):
    assert BAND_PAST + ATT_SUB - REL_LEN // 2 == MAX_REL and REL_LEN // 2 <= 2 * MAX_REL
    far = jnp.broadcast_to(table[:, 2 * MAX_REL:], (ATT_HEADS, REL_LEN // 2))
    near = table[:, 2 * MAX_REL - REL_LEN // 2 + 1:][:, ::-1]
    return jnp.concatenate([far, near], axis=1).astype(F32)[:, None, :]


def _band_attn(q, kv, table, jobs=()):
    t = q.shape[0]
    n_keys = BAND_PAST + ATT_SUB
    prev = lambda i: jnp.maximum(i - 1, 0)
    n_groups = ATT_HEADS // ATT_HEADS_PER_STEP
    blk = (BAND_PAST, ATT_HEADS_PER_STEP * ATT_HD)
    n_blocks = t // BAND_PAST
    job_in, job_out, job_shape = _job_specs(jobs, lambda h, i: h * n_blocks + i, n_groups * n_blocks)
    return pl.pallas_call(
        functools.partial(_band_attn_kernel, jobs=jobs, n_steps=n_groups * n_blocks),
        grid=(n_groups, n_blocks),
        in_specs=[
            pl.BlockSpec(blk, lambda h, i: (i, h)),
            pl.BlockSpec(blk, lambda h, i: (prev(i), h)),
            pl.BlockSpec(blk, lambda h, i: (i, h)),
            pl.BlockSpec(blk, lambda h, i: (prev(i), n_groups + h)),
            pl.BlockSpec(blk, lambda h, i: (i, n_groups + h)),
            pl.BlockSpec((ATT_HEADS_PER_STEP, 1, REL_LEN), lambda h, i: (h, 0, 0)),
            *job_in,
        ],
        out_specs=[pl.BlockSpec(blk, lambda h, i: (i, h)), *job_out],
        out_shape=[jax.ShapeDtypeStruct((t, D_MODEL), BF16), *job_shape],
        scratch_shapes=[pltpu.VMEM((ATT_HEADS_PER_STEP, BAND_PAST // ATT_SUB, ATT_SUB, n_keys), F32)],
        compiler_params=_params("arbitrary", "arbitrary"),
        name="band_attn",
    )(q, kv, kv, kv, kv, _band_rel(table), *(job.src for job in jobs))


def _step_attn_kernel(q_ref, kvn_ref, ck_ref, cv_ref, bp_ref, bn_ref, o_ref):
    nt = (((1,), (1,)), ((), ()))
    for h in range(ATT_HEADS):
        cols = slice(h * ATT_HD, (h + 1) * ATT_HD)
        q = q_ref[:, cols]
        k_new = kvn_ref[:, cols].astype(BF16)
        v_new = kvn_ref[:, D_MODEL + h * ATT_HD:D_MODEL + (h + 1) * ATT_HD].astype(BF16)
        k_past = ck_ref[0, :, h, :].astype(BF16)
        v_past = cv_ref[0, :, h, :].astype(BF16)
        sp = lax.dot_general(q, k_past, nt, preferred_element_type=F32) + bp_ref[h] * LOG2E
        sn = lax.dot_general(q, k_new, nt, preferred_element_type=F32) + bn_ref[h] * LOG2E
        mx = jnp.maximum(jnp.max(sp, axis=-1, keepdims=True), jnp.max(sn, axis=-1, keepdims=True))
        pp = jnp.exp2(sp - mx)
        pn = jnp.exp2(sn - mx)
        denom = jnp.sum(pp, axis=-1, keepdims=True) + jnp.sum(pn, axis=-1, keepdims=True)
        o = (jnp.dot(pp.astype(BF16), v_past, preferred_element_type=F32)
             + jnp.dot(pn.astype(BF16), v_new, preferred_element_type=F32)) / denom
        o_ref[:, cols] = o.astype(o_ref.dtype)


def _step_attn(q, kv_new, cache_k, cache_v, table, batch, seq):
    n_past = cache_k.shape[1]
    dist = n_past + jnp.arange(seq)[:, None] - jnp.arange(n_past + seq)[None, :]
    bias = table[:, jnp.clip(dist, -MAX_REL, MAX_REL) + MAX_REL].astype(F32)
    return pl.pallas_call(
        _step_attn_kernel,
        grid=(batch,),
        in_specs=[
            pl.BlockSpec((seq, D_MODEL), lambda b: (b, 0)),
            pl.BlockSpec((seq, 2 * D_MODEL), lambda b: (b, 0)),
            pl.BlockSpec((1, n_past, ATT_HEADS, ATT_HD), lambda b: (b, 0, 0, 0)),
            pl.BlockSpec((1, n_past, ATT_HEADS, ATT_HD), lambda b: (b, 0, 0, 0)),
            pl.BlockSpec((ATT_HEADS, seq, n_past), lambda b: (0, 0, 0)),
            pl.BlockSpec((ATT_HEADS, seq, seq), lambda b: (0, 0, 0)),
        ],
        out_specs=pl.BlockSpec((seq, D_MODEL), lambda b: (b, 0)),
        out_shape=jax.ShapeDtypeStruct((batch * seq, D_MODEL), BF16),
        compiler_params=_params("parallel"),
        name="step_attn",
    )(q, kv_new, cache_k, cache_v, bias[:, :, :n_past], bias[:, :, n_past:])


def _trunk(x, ada, ada_kv, gla_s0, past, wts):
    batch, seq, _ = x.shape
    m = batch * seq
    tm = min(ROW_TILE, m)
    x = x.reshape(m, D_MODEL)

    def rows(v):
        return v if batch == 1 else jnp.repeat(v, seq, axis=0)

    def mod(l, sub, which):
        return rows(ada[l][:, (3 * sub + which) * D_MODEL:(3 * sub + which + 1) * D_MODEL])

    def ffn(x, l, sub, idx):
        return _ffn(x, mod(l, sub, 0), mod(l, sub, 1), mod(l, sub, 2), *wts["ffn"][l, idx],
                    wts["ln_g"][l, sub][None], wts["ln_b"][l, sub][None], min(FFN_ROW_TILE, m))

    def cast_pending(keys):
        pending = [key for key in keys if key not in wts["ffn"]]
        return pending, tuple(job for key in pending for job in _ffn_cast_jobs(*wts["ffn_f32"], *key))

    def keep_cast(pending, cast):
        for n, key in enumerate(pending):
            wts["ffn"][key] = tuple(cast[3 * n:3 * n + 3])

    x = ffn(x, 0, 0, 0)
    proj, gk_low = _modmm(x, mod(0, 1, 0), mod(0, 1, 1), wts["a_in_t"], tm, BF16,
                          widths=(GLA_MAIN, GLA_GATE_RANK), w_is_t=True)
    pending, jobs = cast_pending(((0, 1), (1, 0)))
    o, gla_state, *cast = _gla(proj, gk_low, wts["a_gk2"], wts["b_a_gk"], wts["g_a_norm"], gla_s0, batch, seq, jobs)
    keep_cast(pending, cast)
    x = _proj_ln(o, wts["a_out"], x, mod(0, 1, 2), wts["ln_g"][0, 1][None], wts["ln_b"][0, 1][None], tm)
    x = ffn(x, 0, 2, 1)

    kv_shift = rows(ada_kv[:, :D_MODEL])
    kv_scale = rows(ada_kv[:, D_MODEL:])
    if past is None:
        kv = _modmm(x, kv_shift, kv_scale, wts["kv"], tm, BF16)
        n_keep = min(BAND_PAST, seq)
        kv_out = _modmm(x[m - n_keep:], kv_shift, kv_scale, wts["kv"], n_keep, F32)
    else:
        kv_out = _modmm(x, kv_shift, kv_scale, wts["kv"], tm, F32)

    x = ffn(x, 1, 0, 0)
    q = _modmm(x, mod(1, 1, 0), mod(1, 1, 1), wts["b_q"], tm, BF16,
               out_scale=ATT_HD ** -0.5 * LOG2E)
    if past is None:
        pending, jobs = cast_pending(((1, 1),))
        o, *cast = _band_attn(q, kv, wts["rel_bias"], jobs)
        keep_cast(pending, cast)
    else:
        o = _step_attn(q, kv_out, past[0], past[1], wts["rel_bias"], batch, seq)
    x = _proj_ln(o, wts["b_out"], x, mod(1, 1, 2), wts["ln_g"][1, 1][None], wts["ln_b"][1, 1][None], tm)
    x = ffn(x, 1, 2, 1)

    n_rows = kv_out.shape[0] // batch
    k_out = kv_out[:, :D_MODEL].reshape(batch, n_rows, ATT_HEADS, ATT_HD)
    v_out = kv_out[:, D_MODEL:].reshape(batch, n_rows, ATT_HEADS, ATT_HD)
    return x.reshape(batch, seq, D_MODEL), gla_state[None], k_out, v_out


def kernel(x_prompt, x_sample, state_gla, cache_band_k, cache_band_v, c_prompt, c_sample, w_ada, b_ada, ln_g, ln_b, w_ffn_up, w_ffn_down, w_a_in, w_a_gk2, b_a_gk, g_a_norm, w_a_out, w_ada_kv, b_ada_kv, w_kv, w_b_q, rel_bias, w_b_out):
    assert DEPTH == 2 and w_a_in.shape[0] == 1 and w_b_q.shape[0] == 1
    n_prompt, n_sample = x_prompt.shape[0], x_sample.shape[0]
    assert n_prompt == 1 and n_prompt + n_sample <= ADA_ROWS

    lead = lambda w: w.reshape((1, 1) + w.shape[-2:])
    wts = {
        "ffn_f32": (w_ffn_up, w_ffn_down),
        "ffn": {(0, 0): (*_cast_split(w_ffn_up, (D_FF, D_FF), CAST_ROWS),
                         *_cast_split(w_ffn_down, (D_MODEL,), D_FF // 8))},
        "a_in_t": _cast_split(lead(jnp.swapaxes(w_a_in, 1, 2)), (D_MODEL,), A_IN_CAST_ROWS)[0],
        "a_gk2": w_a_gk2[0].astype(BF16),
        "b_a_gk": b_a_gk[0][None],
        "g_a_norm": g_a_norm[0][None],
        "a_out": _cast_split(lead(w_a_out), (D_MODEL,), CAST_ROWS)[0],
        "kv": _cast_split(lead(w_kv), (2 * D_MODEL,), CAST_ROWS)[0],
        "b_q": _cast_split(lead(w_b_q), (D_MODEL,), CAST_ROWS)[0],
        "b_out": _cast_split(lead(w_b_out), (D_MODEL,), CAST_ROWS)[0],
        "rel_bias": rel_bias[0],
        "ln_g": ln_g,
        "ln_b": ln_b,
    }

    c_all = jnp.concatenate([c_prompt, c_sample], axis=0)
    c_all = jnp.pad(c_all, ((0, ADA_ROWS - c_all.shape[0]), (0, 0)))
    ada = _ada(c_all, w_ada, b_ada)
    ada_kv = _ada(c_all, w_ada_kv[None], b_ada_kv[None])[0]
    p_rows = slice(0, n_prompt)
    s_rows = slice(n_prompt, n_prompt + n_sample)

    gla_zero = jnp.zeros((n_prompt, GLA_HEADS, GLA_HK, GLA_HV), state_gla.dtype)
    y_p, s_p, k_p, v_p = _trunk(x_prompt, ada[:, p_rows], ada_kv[p_rows], gla_zero, None, wts)
    past = (cache_band_k, cache_band_v)
    y_s, s_s, k_s, v_s = _trunk(x_sample, ada[:, s_rows], ada_kv[s_rows], state_gla[0], past, wts)
    return (y_p, y_s, s_p, s_s, k_p, v_p, k_s, v_s)
```

```python
import functools
from typing import NamedTuple

import jax
import jax.numpy as jnp
from jax import lax
from jax.experimental import pallas as pl
from jax.experimental.pallas import tpu as pltpu

F32 = jnp.float32
BF16 = jnp.bfloat16

D_MODEL = 2048
DEPTH = 2
CHUNK = 64
GLA_HEADS = 4
GLA_DK = D_MODEL // 2
GLA_DV = D_MODEL
GLA_HK = GLA_DK // GLA_HEADS
GLA_HV = GLA_DV // GLA_HEADS
GLA_GATE_RANK = 16
GLA_GATE_NORMALIZER = 16.0
GLA_MAIN = 2 * GLA_DK + 2 * GLA_DV
ATT_HEADS = 16
ATT_HD = D_MODEL // ATT_HEADS
BAND_PAST = 8 * CHUNK
MAX_REL = 256
D_FF = 5504
ALPHA = (2 * DEPTH) ** 0.25
LN_EPS = 1e-5
RMS_EPS = 1e-6
NEG = -1e30
LOG2E = 1.4426950408889634

LANE = 128
FF_TILE = 512
FF_LAST = D_FF - (D_FF - 1) // FF_TILE * FF_TILE
ROW_TILE = 512
FFN_ROW_TILE = 1024
A_IN_CAST_ROWS = 560
CAST_ROWS = 256
FFN_CAST_UP_ROWS = 16
FFN_CAST_DOWN_ROWS = 128
PROJ_SUB = 128
MODMM_COLS = 1024
ATT_SUB = 256
ATT_HEADS_PER_STEP = 4
REL_LEN = BAND_PAST + 2 * ATT_SUB
ADA_ROWS = 16
ADA_TILE = 1024
VMEM_LIMIT = 56 * 1024 * 1024


def _params(*sem):
    return pltpu.CompilerParams(dimension_semantics=sem, vmem_limit_bytes=VMEM_LIMIT)


def _layer_norm(z, g, b):
    mu = jnp.mean(z, axis=-1, keepdims=True)
    zc = z - mu
    var = jnp.mean(zc * zc, axis=-1, keepdims=True)
    return zc * lax.rsqrt(var + LN_EPS) * g + b


def _silu(a):
    half = 0.5 * a
    return half + half * jnp.tanh(half)


def _ada_kernel(c_ref, w_ref, b_ref, o_ref):
    o_ref[...] = jnp.dot(_silu(c_ref[...]), w_ref[...], preferred_element_type=F32) + b_ref[...]


def _ada(c_all, w, b):
    n_l, _, n = w.shape
    return pl.pallas_call(
        _ada_kernel,
        grid=(n_l, n // ADA_TILE),
        in_specs=[
            pl.BlockSpec((ADA_ROWS, D_MODEL), lambda l, j: (0, 0)),
            pl.BlockSpec((None, D_MODEL, ADA_TILE), lambda l, j: (l, 0, j)),
            pl.BlockSpec((None, 1, ADA_TILE), lambda l, j: (l, 0, j)),
        ],
        out_specs=pl.BlockSpec((None, ADA_ROWS, ADA_TILE), lambda l, j: (l, 0, j)),
        out_shape=jax.ShapeDtypeStruct((n_l, ADA_ROWS, n), F32),
        compiler_params=_params("arbitrary", "arbitrary"),
        name="ada",
    )(c_all, w, b.reshape(n_l, 1, n))


def _mod_spec(mod_rows, tm):
    if mod_rows == 1:
        return pl.BlockSpec((1, D_MODEL), lambda i, *_: (0, 0))
    return pl.BlockSpec((tm, D_MODEL), lambda i, *_: (i, 0))


def _ffn_kernel(x_ref, sh_ref, sc_ref, gt_ref, wa_ref, wu_ref, wd_ref, g_ref, b_ref, o_ref, h_ref):
    j = pl.program_id(1)
    last = pl.num_programs(1) - 1

    @pl.when(j == 0)
    def _():
        h_ref[...] = (x_ref[...] * (1.0 + sc_ref[...]) + sh_ref[...]).astype(BF16)
        o_ref[...] = jnp.zeros_like(o_ref)

    def chunk(cols):
        h = h_ref[...]
        a = jnp.dot(h, wa_ref[:, :cols], preferred_element_type=F32)
        u = jnp.dot(h, wu_ref[:, :cols], preferred_element_type=F32)
        o_ref[...] += jnp.dot((_silu(a) * u).astype(BF16), wd_ref[:cols, :], preferred_element_type=F32)

    @pl.when(j < last)
    def _():
        chunk(FF_TILE)

    @pl.when(j == last)
    def _():
        chunk(FF_LAST)
        z = ALPHA * x_ref[...] + (0.5 * (1.0 + gt_ref[...])) * o_ref[...]
        o_ref[...] = _layer_norm(z, g_ref[...], b_ref[...])


def _ffn(x, shift, scale, gate, w_a, w_u, w_down, ln_g, ln_b, tm):
    m = x.shape[0]
    n_ff = pl.cdiv(D_FF, FF_TILE)
    mod = _mod_spec(shift.shape[0], tm)
    vec = pl.BlockSpec((1, D_MODEL), lambda i, j: (0, 0))
    up = pl.BlockSpec((D_MODEL, FF_TILE), lambda i, j: (0, j))
    return pl.pallas_call(
        _ffn_kernel,
        grid=(m // tm, n_ff),
        in_specs=[
            pl.BlockSpec((tm, D_MODEL), lambda i, j: (i, 0)),
            mod, mod, mod,
            up, up,
            pl.BlockSpec((FF_TILE, D_MODEL), lambda i, j: (j, 0)),
            vec, vec,
        ],
        out_specs=pl.BlockSpec((tm, D_MODEL), lambda i, j: (i, 0)),
        out_shape=jax.ShapeDtypeStruct((m, D_MODEL), F32),
        scratch_shapes=[pltpu.VMEM((tm, D_MODEL), BF16)],
        compiler_params=_params("parallel", "arbitrary"),
        name="ffn",
    )(x, shift, scale, gate, w_a, w_u, w_down, ln_g, ln_b)


def _cast_kernel(x_ref, *o_refs):
    off = 0
    for o_ref in o_refs:
        n = o_ref.shape[-1]
        o_ref[...] = x_ref[:, off:off + n].astype(o_ref.dtype)
        off += n


class _CastJob(NamedTuple):
    src: jax.Array
    layer: int
    idx: int
    rows: int
    widths: tuple

    @property
    def n_blocks(self):
        return self.src.shape[2] // self.rows

    def specs(self, step_of):
        r, n = self.src.shape[2:]
        assert r % self.rows == 0 and sum(self.widths) <= n and all(wd % LANE == 0 for wd in self.widths)
        blk = lambda *ids: jnp.minimum(step_of(*ids), self.n_blocks - 1)
        in_spec = pl.BlockSpec((None, None, self.rows, n), lambda *ids: (self.layer, self.idx, blk(*ids), 0))
        out_specs = [pl.BlockSpec((self.rows, wd), lambda *ids: (blk(*ids), 0)) for wd in self.widths]
        out_shape = [jax.ShapeDtypeStruct((r, wd), BF16) for wd in self.widths]
        return in_spec, out_specs, out_shape


def _side_casts(step, n_steps, jobs, in_refs, out_refs):
    pos = 0
    for job, x_ref in zip(jobs, in_refs):
        outs = out_refs[pos:pos + len(job.widths)]
        pos += len(job.widths)
        if job.n_blocks == n_steps:
            _cast_kernel(x_ref, *outs)
        else:
            pl.when(step < job.n_blocks)(functools.partial(_cast_kernel, x_ref, *outs))


def _job_specs(jobs, step_of, n_steps):
    in_specs, out_specs, out_shape = [], [], []
    for job in jobs:
        assert job.n_blocks <= n_steps
        i, o, s = job.specs(step_of)
        in_specs.append(i)
        out_specs += o
        out_shape += s
    return in_specs, out_specs, out_shape


def _cast_split(w, widths, rows, layer=0, idx=0):
    job = _CastJob(w, layer, idx, rows, tuple(widths))
    in_spec, out_specs, out_shape = job.specs(lambda i: i)
    return pl.pallas_call(
        _cast_kernel,
        grid=(job.n_blocks,),
        in_specs=[in_spec],
        out_specs=out_specs,
        out_shape=out_shape,
        compiler_params=_params("parallel"),
        name="cast",
    )(w)


def _ffn_cast_jobs(w_ffn_up, w_ffn_down, layer, idx):
    return (_CastJob(w_ffn_up, layer, idx, FFN_CAST_UP_ROWS, (D_FF, D_FF)),
            _CastJob(w_ffn_down, layer, idx, FFN_CAST_DOWN_ROWS, (D_MODEL,)))


def _modmm_kernel(x_ref, sh_ref, sc_ref, w_ref, o_ref, *rest, w_is_t, out_scale, tail_f32):
    h_ref = rest[-1]
    h_ref[...] = (x_ref[...] * (1.0 + sc_ref[...]) + sh_ref[...]).astype(BF16)
    h = h_ref[...]
    nt = (((1,), (1,)), ((), ()))

    def project(lo, hi):
        if w_is_t:
            y = lax.dot_general(h, w_ref[lo:hi, :], nt, preferred_element_type=F32)
        else:
            y = jnp.dot(h, w_ref[:, lo:hi], preferred_element_type=F32)
        return y if out_scale is None else y * out_scale

    n = o_ref.shape[1]
    if tail_f32:
        tail_ref, = rest[:-1]
        is_tail = pl.program_id(0) == pl.num_programs(0) - 1
        for lo in range(0, n, MODMM_COLS):
            y = project(lo, lo + MODMM_COLS)
            o_ref[:, lo:lo + MODMM_COLS] = y.astype(o_ref.dtype)

            @pl.when(is_tail)
            def _(lo=lo, y=y):
                tail_ref[:, lo:lo + MODMM_COLS] = y
        return
    for lo in range(0, n, MODMM_COLS):
        o_ref[:, lo:lo + MODMM_COLS] = project(lo, lo + MODMM_COLS).astype(o_ref.dtype)
    for extra_ref in rest[:-1]:
        extra_ref[...] = project(n, n + extra_ref.shape[1]).astype(extra_ref.dtype)
        n += extra_ref.shape[1]


def _modmm(x, shift, scale, w, tm, out_dtype, widths=None, w_is_t=False, out_scale=None, tail_f32=False):
    m = x.shape[0]
    widths = widths or (w.shape[0] if w_is_t else w.shape[1],)
    assert widths[0] % MODMM_COLS == 0 and not (tail_f32 and len(widths) > 1)
    mod = _mod_spec(shift.shape[0], tm)
    out_specs = [pl.BlockSpec((tm, wd), lambda i: (i, 0)) for wd in widths]
    out_shape = [jax.ShapeDtypeStruct((m, wd), out_dtype) for wd in widths]
    if tail_f32:
        out_specs.append(pl.BlockSpec((tm, widths[0]), lambda i: (0, 0)))
        out_shape.append(jax.ShapeDtypeStruct((tm, widths[0]), F32))
    out = pl.pallas_call(
        functools.partial(_modmm_kernel, w_is_t=w_is_t, out_scale=out_scale, tail_f32=tail_f32),
        grid=(m // tm,),
        in_specs=[
            pl.BlockSpec((tm, D_MODEL), lambda i: (i, 0)),
            mod, mod,
            pl.BlockSpec(w.shape, lambda i: (0, 0), pipeline_mode=pl.Buffered(1)),
        ],
        out_specs=out_specs,
        out_shape=out_shape,
        scratch_shapes=[pltpu.VMEM((tm, D_MODEL), BF16)],
        compiler_params=_params("arbitrary"),
        name="modmm",
    )(x, shift, scale, w)
    return out if len(out) > 1 else out[0]


def _proj_ln_kernel(a_ref, w_ref, x_ref, gt_ref, g_ref, b_ref, o_ref):
    sub = min(PROJ_SUB, a_ref.shape[0])
    n_sub = a_ref.shape[0] // sub
    dot = lambda r: jnp.dot(a_ref[r * sub:(r + 1) * sub, :], w_ref[...], preferred_element_type=F32)
    y = dot(0)
    for r in range(n_sub):
        y_next = dot(r + 1) if r + 1 < n_sub else None
        rows = slice(r * sub, (r + 1) * sub)
        gt = gt_ref[...] if gt_ref.shape[0] == 1 else gt_ref[rows, :]
        z = ALPHA * x_ref[rows, :] + (1.0 + gt) * y
        o_ref[rows, :] = _layer_norm(z, g_ref[...], b_ref[...])
        y = y_next


def _proj_ln(a, w, x, gate, ln_g, ln_b, tm):
    m = x.shape[0]
    vec = pl.BlockSpec((1, D_MODEL), lambda i: (0, 0))
    return pl.pallas_call(
        _proj_ln_kernel,
        grid=(m // tm,),
        in_specs=[
            pl.BlockSpec((tm, D_MODEL), lambda i: (i, 0)),
            pl.BlockSpec((D_MODEL, D_MODEL), lambda i: (0, 0)),
            pl.BlockSpec((tm, D_MODEL), lambda i: (i, 0)),
            _mod_spec(gate.shape[0], tm),
            vec, vec,
        ],
        out_specs=pl.BlockSpec((tm, D_MODEL), lambda i: (i, 0)),
        out_shape=jax.ShapeDtypeStruct((m, D_MODEL), F32),
        compiler_params=_params("parallel"),
        name="proj_ln",
    )(a, w, x, gate, ln_g, ln_b)


def _gla_kernel(q_ref, k_ref, v_ref, g_ref, gk_ref, w2_ref, bgk_ref, gn_ref, s0_ref, *rest,
                blk, n_blk, jobs, n_steps):
    n_jobs = len(jobs)
    o_ref, s_ref = rest[n_jobs:n_jobs + 2]
    step = (pl.program_id(0) * pl.num_programs(1) + pl.program_id(1)) * pl.num_programs(2) + pl.program_id(2)
    _side_casts(step, n_steps, jobs, rest[:n_jobs], rest[n_jobs + 2:])

    @pl.when(pl.program_id(2) == 0)
    def _():
        s_ref[...] = s0_ref[...]

    causal = (lax.broadcasted_iota(jnp.int32, (blk, blk), 0)
              >= lax.broadcasted_iota(jnp.int32, (blk, blk), 1))
    tril = jnp.broadcast_to(jnp.where(causal, 1.0, 0.0).astype(BF16), (n_blk, blk, blk))
    eye = (lax.broadcasted_iota(jnp.int32, (GLA_HK, GLA_HK), 0)
           == lax.broadcasted_iota(jnp.int32, (GLA_HK, GLA_HK), 1))
    tn = (((0,), (0,)), ((), ()))
    bnn = (((2,), (1,)), ((0,), (0,)))
    bnt = (((2,), (2,)), ((0,), (0,)))

    def blocks(t):
        return t.reshape(n_blk, blk, t.shape[-1])

    z = jnp.dot(gk_ref[...], w2_ref[...], preferred_element_type=F32) + bgk_ref[...]
    log_a = blocks((jnp.minimum(z, 0.0) - jnp.log(1.0 + jnp.exp(-jnp.abs(z)))) * (1.0 / GLA_GATE_NORMALIZER))
    hi = log_a.astype(BF16)
    lo = (log_a - hi.astype(F32)).astype(BF16)
    b = (lax.dot_general(tril, hi, bnn, preferred_element_type=F32)
         + lax.dot_general(tril, lo, bnn, preferred_element_type=F32))
    b_last = b[:, blk - 1:blk, :]
    q = blocks(q_ref[...].astype(F32)) * (GLA_HK ** -0.5)
    k = blocks(k_ref[...].astype(F32))
    v = blocks(v_ref[...])
    q_t = (q * jnp.exp(b)).astype(BF16)
    k_t = (k * jnp.exp(-b)).astype(BF16)
    k_dec = (k * jnp.exp(b_last - b)).astype(BF16)
    scores = lax.dot_general(q_t, k_t, bnt, preferred_element_type=F32)
    scores = jnp.where(causal[None], scores, 0.0).astype(BF16)
    o_intra = lax.dot_general(scores, v, bnn, preferred_element_type=F32)
    decay = jnp.exp(b_last)

    update = [lax.dot_general(k_dec[c], v[c], tn, preferred_element_type=F32) for c in range(n_blk)]
    decay_col = [jnp.sum(jnp.where(eye, jnp.broadcast_to(decay[c], (GLA_HK, GLA_HK)), 0.0),
                         axis=1, keepdims=True) for c in range(n_blk)]
    state = s_ref[0, 0]
    starts = []
    for c in range(n_blk):
        starts.append(state.astype(BF16))
        state = decay_col[c] * state + update[c]
    s_ref[0, 0] = state
    o = o_intra + jnp.stack([jnp.dot(q_t[c], starts[c], preferred_element_type=F32) for c in range(n_blk)])
    on = o * lax.rsqrt(jnp.mean(o * o, axis=-1, keepdims=True) + RMS_EPS) * gn_ref[...]
    gate = _silu(g_ref[...].astype(F32)).reshape(o.shape)
    o_ref[...] = (on * gate).reshape(o_ref.shape).astype(o_ref.dtype)


def _gla(proj, gk_low, w_gk2, b_gk, g_norm, s0, batch, seq, jobs=()):
    blk = min(CHUNK, seq)
    rows = min(ROW_TILE, seq)
    n_steps = seq // rows
    m = batch * seq
    k_off = GLA_DK // GLA_HK
    v_off = 2 * GLA_DK // GLA_HV
    g_off = (2 * GLA_DK + GLA_DV) // GLA_HV
    row = lambda b, h, c: b * n_steps + c
    state_spec = pl.BlockSpec((1, 1, GLA_HK, GLA_HV), lambda b, h, c: (b, h, 0, 0))
    grid_steps = batch * GLA_HEADS * n_steps
    job_in, job_out, job_shape = _job_specs(jobs, lambda b, h, c: (b * GLA_HEADS + h) * n_steps + c, grid_steps)
    return pl.pallas_call(
        functools.partial(_gla_kernel, blk=blk, n_blk=rows // blk, jobs=jobs, n_steps=grid_steps),
        grid=(batch, GLA_HEADS, n_steps),
        in_specs=[
            pl.BlockSpec((rows, GLA_HK), lambda b, h, c: (row(b, h, c), h)),
            pl.BlockSpec((rows, GLA_HK), lambda b, h, c: (row(b, h, c), k_off + h)),
            pl.BlockSpec((rows, GLA_HV), lambda b, h, c: (row(b, h, c), v_off + h)),
            pl.BlockSpec((rows, GLA_HV), lambda b, h, c: (row(b, h, c), g_off + h)),
            pl.BlockSpec((rows, GLA_GATE_RANK), lambda b, h, c: (row(b, h, c), 0)),
            pl.BlockSpec((GLA_GATE_RANK, GLA_HK), lambda b, h, c: (0, h)),
            pl.BlockSpec((1, GLA_HK), lambda b, h, c: (0, h)),
            pl.BlockSpec((1, GLA_HV), lambda b, h, c: (0, 0)),
            state_spec,
            *job_in,
        ],
        out_specs=[
            pl.BlockSpec((rows, GLA_HV), lambda b, h, c: (row(b, h, c), h)),
            state_spec,
            *job_out,
        ],
        out_shape=[
            jax.ShapeDtypeStruct((m, GLA_DV), BF16),
            jax.ShapeDtypeStruct(s0.shape, F32),
            *job_shape,
        ],
        compiler_params=_params("arbitrary", "arbitrary", "arbitrary"),
        name="gla",
    )(proj, proj, proj, proj, gk_low, w_gk2, b_gk, g_norm, s0, *(job.src for job in jobs))


def _band_attn_kernel(q_ref, kp_ref, kc_ref, vp_ref, vc_ref, rel_ref, *rest, jobs, n_steps):
    n_jobs = len(jobs)
    o_ref, bias_ref = rest[n_jobs], rest[-1]
    i = pl.program_id(1)
    _side_casts(pl.program_id(0) * pl.num_programs(1) + i, n_steps, jobs, rest[:n_jobs], rest[n_jobs + 1:-1])
    nt = (((1,), (1,)), ((), ()))
    n_keys = BAND_PAST + ATT_SUB
    n_sub = BAND_PAST // ATT_SUB

    @pl.when(i <= 1)
    def _():
        col = lax.broadcasted_iota(jnp.int32, (ATT_SUB, n_keys), 1)
        qc = lax.broadcasted_iota(jnp.int32, (ATT_SUB, n_keys), 0) // CHUNK
        kc = col // CHUNK
        in_band = (kc >= qc) & (kc <= qc + BAND_PAST // CHUNK)
        for h in range(ATT_HEADS_PER_STEP):
            rel = jnp.broadcast_to(rel_ref[h], (ATT_SUB, REL_LEN))
            toeplitz = pltpu.roll(rel, 0, 1, stride=1, stride_axis=0)[:, ATT_SUB:] * LOG2E
            for s in range(n_sub):
                key_pos = (i - 1) * BAND_PAST + s * ATT_SUB + col
                bias_ref[h, s] = jnp.where(in_band & (key_pos >= 0), toeplitz, NEG)

    units = [(h, s) for h in range(ATT_HEADS_PER_STEP) for s in range(n_sub)]

    def scores(h, s):
        lo, hi = s * ATT_SUB, (s + 1) * ATT_SUB
        cols = slice(h * ATT_HD, (h + 1) * ATT_HD)
        k = jnp.concatenate([kp_ref[lo:, cols], kc_ref[:hi, cols]], axis=0)
        return lax.dot_general(q_ref[lo:hi, cols], k, nt, preferred_element_type=F32) + bias_ref[h, s]

    def finish(h, s, sc):
        lo, hi = s * ATT_SUB, (s + 1) * ATT_SUB
        cols = slice(h * ATT_HD, (h + 1) * ATT_HD)
        v = jnp.concatenate([vp_ref[lo:, cols], vc_ref[:hi, cols]], axis=0)
        p = jnp.exp2(sc - jnp.max(sc, axis=-1, keepdims=True))
        denom = jnp.sum(p, axis=-1, keepdims=True)
        o = jnp.dot(p.astype(BF16), v, preferred_element_type=F32) / denom
        o_ref[lo:hi, cols] = o.astype(o_ref.dtype)

    sc = scores(*units[0])
    for n, unit in enumerate(units):
        sc_next = scores(*units[n + 1]) if n + 1 < len(units) else None
        finish(*unit, sc)
        sc = sc_next


def _band_rel(table):
    assert BAND_PAST + ATT_SUB - REL_LEN // 2 == MAX_REL and REL_LEN // 2 <= 2 * MAX_REL
    far = jnp.broadcast_to(table[:, 2 * MAX_REL:], (ATT_HEADS, REL_LEN // 2))
    near = table[:, 2 * MAX_REL - REL_LEN // 2 + 1:][:, ::-1]
    return jnp.concatenate([far, near], axis=1).astype(F32)[:, None, :]


def _band_attn(q, kv, table, jobs=()):
    t = q.shape[0]
    n_keys = BAND_PAST + ATT_SUB
    prev = lambda i: jnp.maximum(i - 1, 0)
    n_groups = ATT_HEADS // ATT_HEADS_PER_STEP
    blk = (BAND_PAST, ATT_HEADS_PER_STEP * ATT_HD)
    n_blocks = t // BAND_PAST
    job_in, job_out, job_shape = _job_specs(jobs, lambda h, i: h * n_blocks + i, n_groups * n_blocks)
    return pl.pallas_call(
        functools.partial(_band_attn_kernel, jobs=jobs, n_steps=n_groups * n_blocks),
        grid=(n_groups, n_blocks),
        in_specs=[
            pl.BlockSpec(blk, lambda h, i: (i, h)),
            pl.BlockSpec(blk, lambda h, i: (prev(i), h)),
            pl.BlockSpec(blk, lambda h, i: (i, h)),
            pl.BlockSpec(blk, lambda h, i: (prev(i), n_groups + h)),
            pl.BlockSpec(blk, lambda h, i: (i, n_groups + h)),
            pl.BlockSpec((ATT_HEADS_PER_STEP, 1, REL_LEN), lambda h, i: (h, 0, 0)),
            *job_in,
        ],
        out_specs=[pl.BlockSpec(blk, lambda h, i: (i, h)), *job_out],
        out_shape=[jax.ShapeDtypeStruct((t, D_MODEL), BF16), *job_shape],
        scratch_shapes=[pltpu.VMEM((ATT_HEADS_PER_STEP, BAND_PAST // ATT_SUB, ATT_SUB, n_keys), F32)],
        compiler_params=_params("arbitrary", "arbitrary"),
        name="band_attn",
    )(q, kv, kv, kv, kv, _band_rel(table), *(job.src for job in jobs))


def _step_attn_kernel(q_ref, kvn_ref, ck_ref, cv_ref, bp_ref, bn_ref, o_ref):
    nt = (((1,), (1,)), ((), ()))
    heads = lambda ref, off: jnp.concatenate(
        [ref[:, off + h * ATT_HD:off + (h + 1) * ATT_HD] for h in range(ATT_HEADS)], axis=0)
    q = heads(q_ref, 0)
    k_new = heads(kvn_ref, 0).astype(BF16)
    v_new = heads(kvn_ref, D_MODEL).astype(BF16)
    n_keys = ck_ref.shape[1] * ATT_HEADS
    k_past = ck_ref[0].reshape(n_keys, ATT_HD).astype(BF16)
    v_past = cv_ref[0].reshape(n_keys, ATT_HD).astype(BF16)
    sp = lax.dot_general(q, k_past, nt, preferred_element_type=F32) + bp_ref[...]
    sn = lax.dot_general(q, k_new, nt, preferred_element_type=F32) + bn_ref[...]
    mx = jnp.maximum(jnp.max(sp, axis=-1, keepdims=True), jnp.max(sn, axis=-1, keepdims=True))
    pp = jnp.exp2(sp - mx)
    pn = jnp.exp2(sn - mx)
    denom = jnp.sum(pp, axis=-1, keepdims=True) + jnp.sum(pn, axis=-1, keepdims=True)
    o = (jnp.dot(pp.astype(BF16), v_past, preferred_element_type=F32)
         + jnp.dot(pn.astype(BF16), v_new, preferred_element_type=F32)) / denom
    seq = q_ref.shape[0]
    for h in range(ATT_HEADS):
        o_ref[:, h * ATT_HD:(h + 1) * ATT_HD] = o[h * seq:(h + 1) * seq, :].astype(o_ref.dtype)


def _step_attn(q, kv_new, cache_k, cache_v, table, batch, seq):
    n_past = cache_k.shape[1]
    dist = n_past + jnp.arange(seq)[:, None] - jnp.arange(n_past + seq)[None, :]
    bias = table[:, jnp.clip(dist, -MAX_REL, MAX_REL) + MAX_REL].astype(F32) * LOG2E
    same_head = jnp.eye(ATT_HEADS, dtype=bool)
    bias_past = jnp.where(same_head[:, None, None, :], bias[:, :, :n_past, None], NEG)
    bias_new = jnp.where(same_head[:, None, :, None], bias[:, :, None, n_past:], NEG)
    rows = ATT_HEADS * seq
    return pl.pallas_call(
        _step_attn_kernel,
        grid=(batch,),
        in_specs=[
            pl.BlockSpec((seq, D_MODEL), lambda b: (b, 0)),
            pl.BlockSpec((seq, 2 * D_MODEL), lambda b: (b, 0)),
            pl.BlockSpec((1, n_past, ATT_HEADS, ATT_HD), lambda b: (b, 0, 0, 0)),
            pl.BlockSpec((1, n_past, ATT_HEADS, ATT_HD), lambda b: (b, 0, 0, 0)),
            pl.BlockSpec((rows, n_past * ATT_HEADS), lambda b: (0, 0)),
            pl.BlockSpec((rows, rows), lambda b: (0, 0)),
        ],
        out_specs=pl.BlockSpec((seq, D_MODEL), lambda b: (b, 0)),
        out_shape=jax.ShapeDtypeStruct((batch * seq, D_MODEL), BF16),
        compiler_params=_params("parallel"),
        name="step_attn",
    )(q, kv_new, cache_k, cache_v, bias_past.reshape(rows, n_past * ATT_HEADS), bias_new.reshape(rows, rows))


def _trunk(x, ada, ada_kv, gla_s0, past, wts):
    batch, seq, _ = x.shape
    m = batch * seq
    tm = min(ROW_TILE, m)
    x = x.reshape(m, D_MODEL)

    def rows(v):
        return v if batch == 1 else jnp.repeat(v, seq, axis=0)

    def mod(l, sub, which):
        return rows(ada[l][:, (3 * sub + which) * D_MODEL:(3 * sub + which + 1) * D_MODEL])

    def ffn(x, l, sub, idx):
        return _ffn(x, mod(l, sub, 0), mod(l, sub, 1), mod(l, sub, 2), *wts["ffn"][l, idx],
                    wts["ln_g"][l, sub][None], wts["ln_b"][l, sub][None], min(FFN_ROW_TILE, m))

    def cast_pending(keys):
        pending = [key for key in keys if key not in wts["ffn"]]
        return pending, tuple(job for key in pending for job in _ffn_cast_jobs(*wts["ffn_f32"], *key))

    def keep_cast(pending, cast):
        for n, key in enumerate(pending):
            wts["ffn"][key] = tuple(cast[3 * n:3 * n + 3])

    x = ffn(x, 0, 0, 0)
    proj, gk_low = _modmm(x, mod(0, 1, 0), mod(0, 1, 1), wts["a_in_t"], tm, BF16,
                          widths=(GLA_MAIN, GLA_GATE_RANK), w_is_t=True)
    pending, jobs = cast_pending(((0, 1), (1, 0)))
    o, gla_state, *cast = _gla(proj, gk_low, wts["a_gk2"], wts["b_a_gk"], wts["g_a_norm"], gla_s0, batch, seq, jobs)
    keep_cast(pending, cast)
    x = _proj_ln(o, wts["a_out"], x, mod(0, 1, 2), wts["ln_g"][0, 1][None], wts["ln_b"][0, 1][None], tm)
    x = ffn(x, 0, 2, 1)

    kv_shift = rows(ada_kv[:, :D_MODEL])
    kv_scale = rows(ada_kv[:, D_MODEL:])
    if past is None:
        assert tm == min(BAND_PAST, seq)
        kv, kv_out = _modmm(x, kv_shift, kv_scale, wts["kv"], tm, BF16, tail_f32=True)
    else:
        kv_out = _modmm(x, kv_shift, kv_scale, wts["kv"], tm, F32)

    x = ffn(x, 1, 0, 0)
    q = _modmm(x, mod(1, 1, 0), mod(1, 1, 1), wts["b_q"], tm, BF16,
               out_scale=ATT_HD ** -0.5 * LOG2E)
    if past is None:
        pending, jobs = cast_pending(((1, 1),))
        o, *cast = _band_attn(q, kv, wts["rel_bias"], jobs)
        keep_cast(pending, cast)
    else:
        o = _step_attn(q, kv_out, past[0], past[1], wts["rel_bias"], batch, seq)
    x = _proj_ln(o, wts["b_out"], x, mod(1, 1, 2), wts["ln_g"][1, 1][None], wts["ln_b"][1, 1][None], tm)
    x = ffn(x, 1, 2, 1)

    n_rows = kv_out.shape[0] // batch
    k_out = kv_out[:, :D_MODEL].reshape(batch, n_rows, ATT_HEADS, ATT_HD)
    v_out = kv_out[:, D_MODEL:].reshape(batch, n_rows, ATT_HEADS, ATT_HD)
    return x.reshape(batch, seq, D_MODEL), gla_state[None], k_out, v_out


def kernel(x_prompt, x_sample, state_gla, cache_band_k, cache_band_v, c_prompt, c_sample, w_ada, b_ada, ln_g, ln_b, w_ffn_up, w_ffn_down, w_a_in, w_a_gk2, b_a_gk, g_a_norm, w_a_out, w_ada_kv, b_ada_kv, w_kv, w_b_q, rel_bias, w_b_out):
    assert DEPTH == 2 and w_a_in.shape[0] == 1 and w_b_q.shape[0] == 1
    n_prompt, n_sample = x_prompt.shape[0], x_sample.shape[0]
    assert n_prompt == 1 and n_prompt + n_sample <= ADA_ROWS

    lead = lambda w: w.reshape((1, 1) + w.shape[-2:])
    wts = {
        "ffn_f32": (w_ffn_up, w_ffn_down),
        "ffn": {(0, 0): (*_cast_split(w_ffn_up, (D_FF, D_FF), CAST_ROWS),
                         *_cast_split(w_ffn_down, (D_MODEL,), D_FF // 8))},
        "a_in_t": _cast_split(lead(jnp.swapaxes(w_a_in, 1, 2)), (D_MODEL,), A_IN_CAST_ROWS)[0],
        "a_gk2": w_a_gk2[0].astype(BF16),
        "b_a_gk": b_a_gk[0][None],
        "g_a_norm": g_a_norm[0][None],
        "a_out": _cast_split(lead(w_a_out), (D_MODEL,), CAST_ROWS)[0],
        "kv": _cast_split(lead(w_kv), (2 * D_MODEL,), CAST_ROWS)[0],
        "b_q": _cast_split(lead(w_b_q), (D_MODEL,), CAST_ROWS)[0],
        "b_out": _cast_split(lead(w_b_out), (D_MODEL,), CAST_ROWS)[0],
        "rel_bias": rel_bias[0],
        "ln_g": ln_g,
        "ln_b": ln_b,
    }

    c_all = jnp.concatenate([c_prompt, c_sample], axis=0)
    c_all = jnp.pad(c_all, ((0, ADA_ROWS - c_all.shape[0]), (0, 0)))
    ada = _ada(c_all, w_ada, b_ada)
    ada_kv = _ada(c_all, w_ada_kv[None], b_ada_kv[None])[0]
    p_rows = slice(0, n_prompt)
    s_rows = slice(n_prompt, n_prompt + n_sample)

    gla_zero = jnp.zeros((n_prompt, GLA_HEADS, GLA_HK, GLA_HV), state_gla.dtype)
    y_p, s_p, k_p, v_p = _trunk(x_prompt, ada[:, p_rows], ada_kv[p_rows], gla_zero, None, wts)
    past = (cache_band_k, cache_band_v)
    y_s, s_s, k_s, v_s = _trunk(x_sample, ada[:, s_rows], ada_kv[s_rows], state_gla[0], past, wts)
    return (y_p, y_s, s_p, s_s, k_p, v_p, k_s, v_s)
```

```python
import functools
from typing import NamedTuple

import jax
import jax.numpy as jnp
from jax import lax
from jax.experimental import pallas as pl
from jax.experimental.pallas import tpu as pltpu

F32 = jnp.float32
BF16 = jnp.bfloat16

D_MODEL = 2048
DEPTH = 2
CHUNK = 64
GLA_HEADS = 4
GLA_DK = D_MODEL // 2
GLA_DV = D_MODEL
GLA_HK = GLA_DK // GLA_HEADS
GLA_HV = GLA_DV // GLA_HEADS
GLA_GATE_RANK = 16
GLA_GATE_NORMALIZER = 16.0
GLA_MAIN = 2 * GLA_DK + 2 * GLA_DV
ATT_HEADS = 16
ATT_HD = D_MODEL // ATT_HEADS
BAND_PAST = 8 * CHUNK
MAX_REL = 256
D_FF = 5504
ALPHA = (2 * DEPTH) ** 0.25
LN_EPS = 1e-5
RMS_EPS = 1e-6
NEG = -1e30
LOG2E = 1.4426950408889634

LANE = 128
FF_TILE = 512
FF_LAST = D_FF - (D_FF - 1) // FF_TILE * FF_TILE
ROW_TILE = 512
FFN_ROW_TILE = 1024
A_IN_CAST_ROWS = 560
CAST_ROWS = 256
FFN_CAST_UP_ROWS = 16
FFN_CAST_DOWN_ROWS = 128
PROJ_SUB = 128
MODMM_COLS = 1024
ATT_SUB = 256
ATT_HEADS_PER_STEP = 4
REL_LEN = BAND_PAST + 2 * ATT_SUB
ADA_ROWS = 16
ADA_TILE = 1024
VMEM_LIMIT = 56 * 1024 * 1024


def _params(*sem):
    return pltpu.CompilerParams(dimension_semantics=sem, vmem_limit_bytes=VMEM_LIMIT)


def _layer_norm(z, g, b):
    mu = jnp.mean(z, axis=-1, keepdims=True)
    zc = z - mu
    var = jnp.mean(zc * zc, axis=-1, keepdims=True)
    return zc * lax.rsqrt(var + LN_EPS) * g + b


def _silu(a):
    half = 0.5 * a
    return half + half * jnp.tanh(half)


def _ada_kernel(c_ref, w_ref, b_ref, o_ref):
    o_ref[...] = jnp.dot(_silu(c_ref[...]), w_ref[...], preferred_element_type=F32) + b_ref[...]


def _ada(c_all, w, b):
    n_l, _, n = w.shape
    return pl.pallas_call(
        _ada_kernel,
        grid=(n_l, n // ADA_TILE),
        in_specs=[
            pl.BlockSpec((ADA_ROWS, D_MODEL), lambda l, j: (0, 0)),
            pl.BlockSpec((None, D_MODEL, ADA_TILE), lambda l, j: (l, 0, j)),
            pl.BlockSpec((None, 1, ADA_TILE), lambda l, j: (l, 0, j)),
        ],
        out_specs=pl.BlockSpec((None, ADA_ROWS, ADA_TILE), lambda l, j: (l, 0, j)),
        out_shape=jax.ShapeDtypeStruct((n_l, ADA_ROWS, n), F32),
        compiler_params=_params("arbitrary", "arbitrary"),
        name="ada",
    )(c_all, w, b.reshape(n_l, 1, n))


def _mod_spec(mod_rows, tm):
    if mod_rows == 1:
        return pl.BlockSpec((1, D_MODEL), lambda i, *_: (0, 0))
    return pl.BlockSpec((tm, D_MODEL), lambda i, *_: (i, 0))


def _ffn_kernel(x_ref, sh_ref, sc_ref, gt_ref, wa_ref, wu_ref, wd_ref, g_ref, b_ref, o_ref, h_ref):
    j = pl.program_id(1)
    last = pl.num_programs(1) - 1

    @pl.when(j == 0)
    def _():
        h_ref[...] = (x_ref[...] * (1.0 + sc_ref[...]) + sh_ref[...]).astype(BF16)
        o_ref[...] = jnp.zeros_like(o_ref)

    def chunk(cols):
        h = h_ref[...]
        a = jnp.dot(h, wa_ref[:, :cols], preferred_element_type=F32)
        u = jnp.dot(h, wu_ref[:, :cols], preferred_element_type=F32)
        o_ref[...] += jnp.dot((_silu(a) * u).astype(BF16), wd_ref[:cols, :], preferred_element_type=F32)

    @pl.when(j < last)
    def _():
        chunk(FF_TILE)

    @pl.when(j == last)
    def _():
        chunk(FF_LAST)
        z = ALPHA * x_ref[...] + (0.5 * (1.0 + gt_ref[...])) * o_ref[...]
        o_ref[...] = _layer_norm(z, g_ref[...], b_ref[...])


def _ffn(x, shift, scale, gate, w_a, w_u, w_down, ln_g, ln_b, tm):
    m = x.shape[0]
    n_ff = pl.cdiv(D_FF, FF_TILE)
    mod = _mod_spec(shift.shape[0], tm)
    vec = pl.BlockSpec((1, D_MODEL), lambda i, j: (0, 0))
    up = pl.BlockSpec((D_MODEL, FF_TILE), lambda i, j: (0, j))
    return pl.pallas_call(
        _ffn_kernel,
        grid=(m // tm, n_ff),
        in_specs=[
            pl.BlockSpec((tm, D_MODEL), lambda i, j: (i, 0)),
            mod, mod, mod,
            up, up,
            pl.BlockSpec((FF_TILE, D_MODEL), lambda i, j: (j, 0)),
            vec, vec,
        ],
        out_specs=pl.BlockSpec((tm, D_MODEL), lambda i, j: (i, 0)),
        out_shape=jax.ShapeDtypeStruct((m, D_MODEL), F32),
        scratch_shapes=[pltpu.VMEM((tm, D_MODEL), BF16)],
        compiler_params=_params("parallel", "arbitrary"),
        name="ffn",
    )(x, shift, scale, gate, w_a, w_u, w_down, ln_g, ln_b)


def _cast_kernel(x_ref, *o_refs):
    off = 0
    for o_ref in o_refs:
        n = o_ref.shape[-1]
        o_ref[...] = x_ref[:, off:off + n].astype(o_ref.dtype)
        off += n


class _CastJob(NamedTuple):
    src: jax.Array
    layer: int
    idx: int
    rows: int
    widths: tuple

    @property
    def n_blocks(self):
        return self.src.shape[2] // self.rows

    def specs(self, step_of):
        r, n = self.src.shape[2:]
        assert r % self.rows == 0 and sum(self.widths) <= n and all(wd % LANE == 0 for wd in self.widths)
        blk = lambda *ids: jnp.minimum(step_of(*ids), self.n_blocks - 1)
        in_spec = pl.BlockSpec((None, None, self.rows, n), lambda *ids: (self.layer, self.idx, blk(*ids), 0))
        out_specs = [pl.BlockSpec((self.rows, wd), lambda *ids: (blk(*ids), 0)) for wd in self.widths]
        out_shape = [jax.ShapeDtypeStruct((r, wd), BF16) for wd in self.widths]
        return in_spec, out_specs, out_shape


def _side_casts(step, n_steps, jobs, in_refs, out_refs):
    pos = 0
    for job, x_ref in zip(jobs, in_refs):
        outs = out_refs[pos:pos + len(job.widths)]
        pos += len(job.widths)
        if job.n_blocks == n_steps:
            _cast_kernel(x_ref, *outs)
        else:
            pl.when(step < job.n_blocks)(functools.partial(_cast_kernel, x_ref, *outs))


def _job_specs(jobs, step_of, n_steps):
    in_specs, out_specs, out_shape = [], [], []
    for job in jobs:
        assert job.n_blocks <= n_steps
        i, o, s = job.specs(step_of)
        in_specs.append(i)
        out_specs += o
        out_shape += s
    return in_specs, out_specs, out_shape


def _cast_split(w, widths, rows, layer=0, idx=0):
    job = _CastJob(w, layer, idx, rows, tuple(widths))
    in_spec, out_specs, out_shape = job.specs(lambda i: i)
    return pl.pallas_call(
        _cast_kernel,
        grid=(job.n_blocks,),
        in_specs=[in_spec],
        out_specs=out_specs,
        out_shape=out_shape,
        compiler_params=_params("parallel"),
        name="cast",
    )(w)


def _ffn_cast_jobs(w_ffn_up, w_ffn_down, layer, idx):
    return (_CastJob(w_ffn_up, layer, idx, FFN_CAST_UP_ROWS, (D_FF, D_FF)),
            _CastJob(w_ffn_down, layer, idx, FFN_CAST_DOWN_ROWS, (D_MODEL,)))


def _modmm_kernel(x_ref, sh_ref, sc_ref, w_ref, o_ref, *rest, w_is_t, out_scale, tail_f32):
    h_ref = rest[-1]
    h_ref[...] = (x_ref[...] * (1.0 + sc_ref[...]) + sh_ref[...]).astype(BF16)
    h = h_ref[...]
    nt = (((1,), (1,)), ((), ()))

    def project(lo, hi):
        if w_is_t:
            y = lax.dot_general(h, w_ref[lo:hi, :], nt, preferred_element_type=F32)
        else:
            y = jnp.dot(h, w_ref[:, lo:hi], preferred_element_type=F32)
        return y if out_scale is None else y * out_scale

    n = o_ref.shape[1]
    if tail_f32:
        tail_ref, = rest[:-1]
        for lo in range(0, n, MODMM_COLS):
            y = project(lo, lo + MODMM_COLS)
            o_ref[:, lo:lo + MODMM_COLS] = y.astype(o_ref.dtype)
            tail_ref[:, lo:lo + MODMM_COLS] = y
        return
    for lo in range(0, n, MODMM_COLS):
        o_ref[:, lo:lo + MODMM_COLS] = project(lo, lo + MODMM_COLS).astype(o_ref.dtype)
    for extra_ref in rest[:-1]:
        extra_ref[...] = project(n, n + extra_ref.shape[1]).astype(extra_ref.dtype)
        n += extra_ref.shape[1]


def _modmm(x, shift, scale, w, tm, out_dtype, widths=None, w_is_t=False, out_scale=None, tail_f32=False):
    m = x.shape[0]
    widths = widths or (w.shape[0] if w_is_t else w.shape[1],)
    assert widths[0] % MODMM_COLS == 0 and not (tail_f32 and len(widths) > 1)
    mod = _mod_spec(shift.shape[0], tm)
    out_specs = [pl.BlockSpec((tm, wd), lambda i: (i, 0)) for wd in widths]
    out_shape = [jax.ShapeDtypeStruct((m, wd), out_dtype) for wd in widths]
    if tail_f32:
        out_specs.append(pl.BlockSpec((tm, widths[0]), lambda i: (0, 0)))
        out_shape.append(jax.ShapeDtypeStruct((tm, widths[0]), F32))
    out = pl.pallas_call(
        functools.partial(_modmm_kernel, w_is_t=w_is_t, out_scale=out_scale, tail_f32=tail_f32),
        grid=(m // tm,),
        in_specs=[
            pl.BlockSpec((tm, D_MODEL), lambda i: (i, 0)),
            mod, mod,
            pl.BlockSpec(w.shape, lambda i: (0, 0), pipeline_mode=pl.Buffered(1)),
        ],
        out_specs=out_specs,
        out_shape=out_shape,
        scratch_shapes=[pltpu.VMEM((tm, D_MODEL), BF16)],
        compiler_params=_params("arbitrary"),
        name="modmm",
    )(x, shift, scale, w)
    return out if len(out) > 1 else out[0]


def _proj_ln_kernel(a_ref, w_ref, x_ref, gt_ref, g_ref, b_ref, o_ref):
    sub = min(PROJ_SUB, a_ref.shape[0])
    n_sub = a_ref.shape[0] // sub
    dot = lambda r: jnp.dot(a_ref[r * sub:(r + 1) * sub, :], w_ref[...], preferred_element_type=F32)
    y = dot(0)
    for r in range(n_sub):
        y_next = dot(r + 1) if r + 1 < n_sub else None
        rows = slice(r * sub, (r + 1) * sub)
        gt = gt_ref[...] if gt_ref.shape[0] == 1 else gt_ref[rows, :]
        z = ALPHA * x_ref[rows, :] + (1.0 + gt) * y
        o_ref[rows, :] = _layer_norm(z, g_ref[...], b_ref[...])
        y = y_next


def _proj_ln(a, w, x, gate, ln_g, ln_b, tm):
    m = x.shape[0]
    vec = pl.BlockSpec((1, D_MODEL), lambda i: (0, 0))
    return pl.pallas_call(
        _proj_ln_kernel,
        grid=(m // tm,),
        in_specs=[
            pl.BlockSpec((tm, D_MODEL), lambda i: (i, 0)),
            pl.BlockSpec((D_MODEL, D_MODEL), lambda i: (0, 0)),
            pl.BlockSpec((tm, D_MODEL), lambda i: (i, 0)),
            _mod_spec(gate.shape[0], tm),
            vec, vec,
        ],
        out_specs=pl.BlockSpec((tm, D_MODEL), lambda i: (i, 0)),
        out_shape=jax.ShapeDtypeStruct((m, D_MODEL), F32),
        compiler_params=_params("parallel"),
        name="proj_ln",
    )(a, w, x, gate, ln_g, ln_b)


def _gla_kernel(q_ref, k_ref, v_ref, g_ref, gk_ref, w2_ref, bgk_ref, gn_ref, s0_ref, *rest,
                blk, n_blk, jobs, n_steps):
    n_jobs = len(jobs)
    o_ref, s_ref = rest[n_jobs:n_jobs + 2]
    step = (pl.program_id(0) * pl.num_programs(1) + pl.program_id(1)) * pl.num_programs(2) + pl.program_id(2)
    _side_casts(step, n_steps, jobs, rest[:n_jobs], rest[n_jobs + 2:])

    @pl.when(pl.program_id(2) == 0)
    def _():
        s_ref[...] = s0_ref[...]

    causal = (lax.broadcasted_iota(jnp.int32, (blk, blk), 0)
              >= lax.broadcasted_iota(jnp.int32, (blk, blk), 1))
    tril = jnp.broadcast_to(jnp.where(causal, 1.0, 0.0).astype(BF16), (n_blk, blk, blk))
    eye = (lax.broadcasted_iota(jnp.int32, (GLA_HK, GLA_HK), 0)
           == lax.broadcasted_iota(jnp.int32, (GLA_HK, GLA_HK), 1))
    tn = (((0,), (0,)), ((), ()))
    bnn = (((2,), (1,)), ((0,), (0,)))
    bnt = (((2,), (2,)), ((0,), (0,)))

    def blocks(t):
        return t.reshape(n_blk, blk, t.shape[-1])

    z = jnp.dot(gk_ref[...], w2_ref[...], preferred_element_type=F32) + bgk_ref[...]
    log_a = blocks((jnp.minimum(z, 0.0) - jnp.log(1.0 + jnp.exp(-jnp.abs(z)))) * (1.0 / GLA_GATE_NORMALIZER))
    hi = log_a.astype(BF16)
    lo = (log_a - hi.astype(F32)).astype(BF16)
    b = (lax.dot_general(tril, hi, bnn, preferred_element_type=F32)
         + lax.dot_general(tril, lo, bnn, preferred_element_type=F32))
    b_last = b[:, blk - 1:blk, :]
    q = blocks(q_ref[...].astype(F32)) * (GLA_HK ** -0.5)
    k = blocks(k_ref[...].astype(F32))
    v = blocks(v_ref[...])
    q_t = (q * jnp.exp(b)).astype(BF16)
    k_t = (k * jnp.exp(-b)).astype(BF16)
    k_dec = (k * jnp.exp(b_last - b)).astype(BF16)
    scores = lax.dot_general(q_t, k_t, bnt, preferred_element_type=F32)
    scores = jnp.where(causal[None], scores, 0.0).astype(BF16)
    o_intra = lax.dot_general(scores, v, bnn, preferred_element_type=F32)
    decay = jnp.exp(b_last)

    update = [lax.dot_general(k_dec[c], v[c], tn, preferred_element_type=F32) for c in range(n_blk)]
    decay_col = [jnp.sum(jnp.where(eye, jnp.broadcast_to(decay[c], (GLA_HK, GLA_HK)), 0.0),
                         axis=1, keepdims=True) for c in range(n_blk)]
    state = s_ref[0, 0]
    starts = []
    for c in range(n_blk):
        starts.append(state.astype(BF16))
        state = decay_col[c] * state + update[c]
    s_ref[0, 0] = state
    o = o_intra + jnp.stack([jnp.dot(q_t[c], starts[c], preferred_element_type=F32) for c in range(n_blk)])
    on = o * lax.rsqrt(jnp.mean(o * o, axis=-1, keepdims=True) + RMS_EPS) * gn_ref[...]
    gate = _silu(g_ref[...].astype(F32)).reshape(o.shape)
    o_ref[...] = (on * gate).reshape(o_ref.shape).astype(o_ref.dtype)


def _gla(proj, gk_low, w_gk2, b_gk, g_norm, s0, batch, seq, jobs=()):
    blk = min(CHUNK, seq)
    rows = min(ROW_TILE, seq)
    n_steps = seq // rows
    m = batch * seq
    k_off = GLA_DK // GLA_HK
    v_off = 2 * GLA_DK // GLA_HV
    g_off = (2 * GLA_DK + GLA_DV) // GLA_HV
    row = lambda b, h, c: b * n_steps + c
    state_spec = pl.BlockSpec((1, 1, GLA_HK, GLA_HV), lambda b, h, c: (b, h, 0, 0))
    grid_steps = batch * GLA_HEADS * n_steps
    job_in, job_out, job_shape = _job_specs(jobs, lambda b, h, c: (b * GLA_HEADS + h) * n_steps + c, grid_steps)
    return pl.pallas_call(
        functools.partial(_gla_kernel, blk=blk, n_blk=rows // blk, jobs=jobs, n_steps=grid_steps),
        grid=(batch, GLA_HEADS, n_steps),
        in_specs=[
            pl.BlockSpec((rows, GLA_HK), lambda b, h, c: (row(b, h, c), h)),
            pl.BlockSpec((rows, GLA_HK), lambda b, h, c: (row(b, h, c), k_off + h)),
            pl.BlockSpec((rows, GLA_HV), lambda b, h, c: (row(b, h, c), v_off + h)),
            pl.BlockSpec((rows, GLA_HV), lambda b, h, c: (row(b, h, c), g_off + h)),
            pl.BlockSpec((rows, GLA_GATE_RANK), lambda b, h, c: (row(b, h, c), 0)),
            pl.BlockSpec((GLA_GATE_RANK, GLA_HK), lambda b, h, c: (0, h)),
            pl.BlockSpec((1, GLA_HK), lambda b, h, c: (0, h)),
            pl.BlockSpec((1, GLA_HV), lambda b, h, c: (0, 0)),
            state_spec,
            *job_in,
        ],
        out_specs=[
            pl.BlockSpec((rows, GLA_HV), lambda b, h, c: (row(b, h, c), h)),
            state_spec,
            *job_out,
        ],
        out_shape=[
            jax.ShapeDtypeStruct((m, GLA_DV), BF16),
            jax.ShapeDtypeStruct(s0.shape, F32),
            *job_shape,
        ],
        compiler_params=_params("arbitrary", "arbitrary", "arbitrary"),
        name="gla",
    )(proj, proj, proj, proj, gk_low, w_gk2, b_gk, g_norm, s0, *(job.src for job in jobs))


def _band_attn_kernel(q_ref, kp_ref, kc_ref, vp_ref, vc_ref, rel_ref, *rest, jobs, n_steps):
    n_jobs = len(jobs)
    o_ref, bias_ref = rest[n_jobs], rest[-1]
    i = pl.program_id(1)
    _side_casts(pl.program_id(0) * pl.num_programs(1) + i, n_steps, jobs, rest[:n_jobs], rest[n_jobs + 1:-1])
    nt = (((1,), (1,)), ((), ()))
    n_keys = BAND_PAST + ATT_SUB
    n_sub = BAND_PAST // ATT_SUB

    @pl.when(i <= 1)
    def _():
        col = lax.broadcasted_iota(jnp.int32, (ATT_SUB, n_keys), 1)
        qc = lax.broadcasted_iota(jnp.int32, (ATT_SUB, n_keys), 0) // CHUNK
        kc = col // CHUNK
        in_band = (kc >= qc) & (kc <= qc + BAND_PAST // CHUNK)
        for h in range(ATT_HEADS_PER_STEP):
            rel = jnp.broadcast_to(rel_ref[h], (ATT_SUB, REL_LEN))
            toeplitz = pltpu.roll(rel, 0, 1, stride=1, stride_axis=0)[:, ATT_SUB:] * LOG2E
            for s in range(n_sub):
                key_pos = (i - 1) * BAND_PAST + s * ATT_SUB + col
                bias_ref[h, s] = jnp.where(in_band & (key_pos >= 0), toeplitz, NEG)

    units = [(h, s) for h in range(ATT_HEADS_PER_STEP) for s in range(n_sub)]

    def scores(h, s):
        lo, hi = s * ATT_SUB, (s + 1) * ATT_SUB
        cols = slice(h * ATT_HD, (h + 1) * ATT_HD)
        k = jnp.concatenate([kp_ref[lo:, cols], kc_ref[:hi, cols]], axis=0)
        return lax.dot_general(q_ref[lo:hi, cols], k, nt, preferred_element_type=F32) + bias_ref[h, s]

    def finish(h, s, sc):
        lo, hi = s * ATT_SUB, (s + 1) * ATT_SUB
        cols = slice(h * ATT_HD, (h + 1) * ATT_HD)
        v = jnp.concatenate([vp_ref[lo:, cols], vc_ref[:hi, cols]], axis=0)
        p = jnp.exp2(sc - jnp.max(sc, axis=-1, keepdims=True))
        denom = jnp.sum(p, axis=-1, keepdims=True)
        o = jnp.dot(p.astype(BF16), v, preferred_element_type=F32) / denom
        o_ref[lo:hi, cols] = o.astype(o_ref.dtype)

    sc = scores(*units[0])
    for n, unit in enumerate(units):
        sc_next = scores(*units[n + 1]) if n + 1 < len(units) else None
        finish(*unit, sc)
        sc = sc_next


def _band_rel(table):
    assert BAND_PAST + ATT_SUB - REL_LEN // 2 == MAX_REL and REL_LEN // 2 <= 2 * MAX_REL
    far = jnp.broadcast_to(table[:, 2 * MAX_REL:], (ATT_HEADS, REL_LEN // 2))
    near = table[:, 2 * MAX_REL - REL_LEN // 2 + 1:][:, ::-1]
    return jnp.concatenate([far, near], axis=1).astype(F32)[:, None, :]


def _band_attn(q, kv, table, jobs=()):
    t = q.shape[0]
    n_keys = BAND_PAST + ATT_SUB
    prev = lambda i: jnp.maximum(i - 1, 0)
    n_groups = ATT_HEADS // ATT_HEADS_PER_STEP
    blk = (BAND_PAST, ATT_HEADS_PER_STEP * ATT_HD)
    n_blocks = t // BAND_PAST
    job_in, job_out, job_shape = _job_specs(jobs, lambda h, i: h * n_blocks + i, n_groups * n_blocks)
    return pl.pallas_call(
        functools.partial(_band_attn_kernel, jobs=jobs, n_steps=n_groups * n_blocks),
        grid=(n_groups, n_blocks),
        in_specs=[
            pl.BlockSpec(blk, lambda h, i: (i, h)),
            pl.BlockSpec(blk, lambda h, i: (prev(i), h)),
            pl.BlockSpec(blk, lambda h, i: (i, h)),
            pl.BlockSpec(blk, lambda h, i: (prev(i), n_groups + h)),
            pl.BlockSpec(blk, lambda h, i: (i, n_groups + h)),
            pl.BlockSpec((ATT_HEADS_PER_STEP, 1, REL_LEN), lambda h, i: (h, 0, 0)),
            *job_in,
        ],
        out_specs=[pl.BlockSpec(blk, lambda h, i: (i, h)), *job_out],
        out_shape=[jax.ShapeDtypeStruct((t, D_MODEL), BF16), *job_shape],
        scratch_shapes=[pltpu.VMEM((ATT_HEADS_PER_STEP, BAND_PAST // ATT_SUB, ATT_SUB, n_keys), F32)],
        compiler_params=_params("arbitrary", "arbitrary"),
        name="band_attn",
    )(q, kv, kv, kv, kv, _band_rel(table), *(job.src for job in jobs))


def _step_attn_kernel(q_ref, kvn_ref, ck_ref, cv_ref, bp_ref, bn_ref, o_ref, bias_ref):
    nt = (((1,), (1,)), ((), ()))
    n_past = ck_ref.shape[1]
    n_keys = n_past * ATT_HEADS

    @pl.when(pl.program_id(0) == 0)
    def _():
        frame = lax.broadcasted_iota(jnp.int32, (n_past, n_keys), 0)
        key = lax.broadcasted_iota(jnp.int32, (n_past, n_keys), 1)
        spread = jnp.where(key // ATT_HEADS == frame, 1.0, 0.0).astype(BF16)
        bp = bp_ref[...]
        hi = bp.astype(BF16)
        lo = (bp - hi.astype(F32)).astype(BF16)
        wide = (jnp.dot(hi, spread, preferred_element_type=F32)
                + jnp.dot(lo, spread, preferred_element_type=F32))
        q_head = lax.broadcasted_iota(jnp.int32, wide.shape, 0) // q_ref.shape[0]
        k_head = lax.broadcasted_iota(jnp.int32, wide.shape, 1) % ATT_HEADS
        bias_ref[...] = jnp.where(q_head == k_head, wide, NEG)

    heads = lambda ref, off: jnp.concatenate(
        [ref[:, off + h * ATT_HD:off + (h + 1) * ATT_HD] for h in range(ATT_HEADS)], axis=0)
    q = heads(q_ref, 0)
    k_new = heads(kvn_ref, 0).astype(BF16)
    v_new = heads(kvn_ref, D_MODEL).astype(BF16)
    k_past = ck_ref[0].reshape(n_keys, ATT_HD).astype(BF16)
    v_past = cv_ref[0].reshape(n_keys, ATT_HD).astype(BF16)
    sp = lax.dot_general(q, k_past, nt, preferred_element_type=F32) + bias_ref[...]
    sn = lax.dot_general(q, k_new, nt, preferred_element_type=F32) + bn_ref[...]
    mx = jnp.maximum(jnp.max(sp, axis=-1, keepdims=True), jnp.max(sn, axis=-1, keepdims=True))
    pp = jnp.exp2(sp - mx)
    pn = jnp.exp2(sn - mx)
    denom = jnp.sum(pp, axis=-1, keepdims=True) + jnp.sum(pn, axis=-1, keepdims=True)
    o = (jnp.dot(pp.astype(BF16), v_past, preferred_element_type=F32)
         + jnp.dot(pn.astype(BF16), v_new, preferred_element_type=F32)) / denom
    seq = q_ref.shape[0]
    for h in range(ATT_HEADS):
        o_ref[:, h * ATT_HD:(h + 1) * ATT_HD] = o[h * seq:(h + 1) * seq, :].astype(o_ref.dtype)


def _step_attn(q, kv_new, cache_k, cache_v, table, batch, seq):
    n_past = cache_k.shape[1]
    dist = n_past + jnp.arange(seq)[:, None] - jnp.arange(n_past + seq)[None, :]
    bias = table[:, jnp.clip(dist, -MAX_REL, MAX_REL) + MAX_REL].astype(F32) * LOG2E
    rows = ATT_HEADS * seq
    bias_past = bias[:, :, :n_past].reshape(rows, n_past)
    same_head = jnp.eye(ATT_HEADS, dtype=bool)[:, None, :, None]
    bias_new = jnp.where(same_head, bias[:, :, None, n_past:], NEG).reshape(rows, rows)
    return pl.pallas_call(
        _step_attn_kernel,
        grid=(batch,),
        in_specs=[
            pl.BlockSpec((seq, D_MODEL), lambda b: (b, 0)),
            pl.BlockSpec((seq, 2 * D_MODEL), lambda b: (b, 0)),
            pl.BlockSpec((1, n_past, ATT_HEADS, ATT_HD), lambda b: (b, 0, 0, 0)),
            pl.BlockSpec((1, n_past, ATT_HEADS, ATT_HD), lambda b: (b, 0, 0, 0)),
            pl.BlockSpec((rows, n_past), lambda b: (0, 0)),
            pl.BlockSpec((rows, rows), lambda b: (0, 0)),
        ],
        out_specs=pl.BlockSpec((seq, D_MODEL), lambda b: (b, 0)),
        out_shape=jax.ShapeDtypeStruct((batch * seq, D_MODEL), BF16),
        scratch_shapes=[pltpu.VMEM((rows, n_past * ATT_HEADS), F32)],
        compiler_params=_params("arbitrary"),
        name="step_attn",
    )(q, kv_new, cache_k, cache_v, bias_past, bias_new)


def _trunk(x, ada, ada_kv, gla_s0, past, wts):
    batch, seq, _ = x.shape
    m = batch * seq
    tm = min(ROW_TILE, m)
    x = x.reshape(m, D_MODEL)

    def rows(v):
        return v if batch == 1 else jnp.repeat(v, seq, axis=0)

    def mod(l, sub, which):
        return rows(ada[l][:, (3 * sub + which) * D_MODEL:(3 * sub + which + 1) * D_MODEL])

    def ffn(x, l, sub, idx):
        return _ffn(x, mod(l, sub, 0), mod(l, sub, 1), mod(l, sub, 2), *wts["ffn"][l, idx],
                    wts["ln_g"][l, sub][None], wts["ln_b"][l, sub][None], min(FFN_ROW_TILE, m))

    def cast_pending(keys):
        pending = [key for key in keys if key not in wts["ffn"]]
        return pending, tuple(job for key in pending for job in _ffn_cast_jobs(*wts["ffn_f32"], *key))

    def keep_cast(pending, cast):
        for n, key in enumerate(pending):
            wts["ffn"][key] = tuple(cast[3 * n:3 * n + 3])

    x = ffn(x, 0, 0, 0)
    proj, gk_low = _modmm(x, mod(0, 1, 0), mod(0, 1, 1), wts["a_in_t"], tm, BF16,
                          widths=(GLA_MAIN, GLA_GATE_RANK), w_is_t=True)
    pending, jobs = cast_pending(((0, 1), (1, 0)))
    o, gla_state, *cast = _gla(proj, gk_low, wts["a_gk2"], wts["b_a_gk"], wts["g_a_norm"], gla_s0, batch, seq, jobs)
    keep_cast(pending, cast)
    x = _proj_ln(o, wts["a_out"], x, mod(0, 1, 2), wts["ln_g"][0, 1][None], wts["ln_b"][0, 1][None], tm)
    x = ffn(x, 0, 2, 1)

    kv_shift = rows(ada_kv[:, :D_MODEL])
    kv_scale = rows(ada_kv[:, D_MODEL:])
    if past is None:
        assert tm == min(BAND_PAST, seq)
        kv, kv_out = _modmm(x, kv_shift, kv_scale, wts["kv"], tm, BF16, tail_f32=True)
    else:
        kv_out = _modmm(x, kv_shift, kv_scale, wts["kv"], tm, F32)

    x = ffn(x, 1, 0, 0)
    q = _modmm(x, mod(1, 1, 0), mod(1, 1, 1), wts["b_q"], tm, BF16,
               out_scale=ATT_HD ** -0.5 * LOG2E)
    if past is None:
        pending, jobs = cast_pending(((1, 1),))
        o, *cast = _band_attn(q, kv, wts["rel_bias"], jobs)
        keep_cast(pending, cast)
    else:
        o = _step_attn(q, kv_out, past[0], past[1], wts["rel_bias"], batch, seq)
    x = _proj_ln(o, wts["b_out"], x, mod(1, 1, 2), wts["ln_g"][1, 1][None], wts["ln_b"][1, 1][None], tm)
    x = ffn(x, 1, 2, 1)

    n_rows = kv_out.shape[0] // batch
    k_out = kv_out[:, :D_MODEL].reshape(batch, n_rows, ATT_HEADS, ATT_HD)
    v_out = kv_out[:, D_MODEL:].reshape(batch, n_rows, ATT_HEADS, ATT_HD)
    return x.reshape(batch, seq, D_MODEL), gla_state[None], k_out, v_out


def kernel(x_prompt, x_sample, state_gla, cache_band_k, cache_band_v, c_prompt, c_sample, w_ada, b_ada, ln_g, ln_b, w_ffn_up, w_ffn_down, w_a_in, w_a_gk2, b_a_gk, g_a_norm, w_a_out, w_ada_kv, b_ada_kv, w_kv, w_b_q, rel_bias, w_b_out):
    assert DEPTH == 2 and w_a_in.shape[0] == 1 and w_b_q.shape[0] == 1
    n_prompt, n_sample = x_prompt.shape[0], x_sample.shape[0]
    assert n_prompt == 1 and n_prompt + n_sample <= ADA_ROWS

    lead = lambda w: w.reshape((1, 1) + w.shape[-2:])
    wts = {
        "ffn_f32": (w_ffn_up, w_ffn_down),
        "ffn": {(0, 0): (*_cast_split(w_ffn_up, (D_FF, D_FF), CAST_ROWS),
                         *_cast_split(w_ffn_down, (D_MODEL,), D_FF // 8))},
        "a_in_t": _cast_split(lead(jnp.swapaxes(w_a_in, 1, 2)), (D_MODEL,), A_IN_CAST_ROWS)[0],
        "a_gk2": w_a_gk2[0].astype(BF16),
        "b_a_gk": b_a_gk[0][None],
        "g_a_norm": g_a_norm[0][None],
        "a_out": _cast_split(lead(w_a_out), (D_MODEL,), CAST_ROWS)[0],
        "kv": _cast_split(lead(w_kv), (2 * D_MODEL,), CAST_ROWS)[0],
        "b_q": _cast_split(lead(w_b_q), (D_MODEL,), CAST_ROWS)[0],
        "b_out": _cast_split(lead(w_b_out), (D_MODEL,), CAST_ROWS)[0],
        "rel_bias": rel_bias[0],
        "ln_g": ln_g,
        "ln_b": ln_b,
    }

    c_all = jnp.concatenate([c_prompt, c_sample], axis=0)
    c_all = jnp.pad(c_all, ((0, ADA_ROWS - c_all.shape[0]), (0, 0)))
    ada = _ada(c_all, w_ada, b_ada)
    ada_kv = _ada(c_all, w_ada_kv[None], b_ada_kv[None])[0]
    p_rows = slice(0, n_prompt)
    s_rows = slice(n_prompt, n_prompt + n_sample)

    gla_zero = jnp.zeros((n_prompt, GLA_HEADS, GLA_HK, GLA_HV), state_gla.dtype)
    y_p, s_p, k_p, v_p = _trunk(x_prompt, ada[:, p_rows], ada_kv[p_rows], gla_zero, None, wts)
    past = (cache_band_k, cache_band_v)
    y_s, s_s, k_s, v_s = _trunk(x_sample, ada[:, s_rows], ada_kv[s_rows], state_gla[0], past, wts)
    return (y_p, y_s, s_p, s_s, k_p, v_p, k_s, v_s)
```

```python
import functools
from typing import NamedTuple

import jax
import jax.numpy as jnp
from jax import lax
from jax.experimental import pallas as pl
from jax.experimental.pallas import tpu as pltpu

F32 = jnp.float32
BF16 = jnp.bfloat16

D_MODEL = 2048
DEPTH = 2
CHUNK = 64
GLA_HEADS = 4
GLA_DK = D_MODEL // 2
GLA_DV = D_MODEL
GLA_HK = GLA_DK // GLA_HEADS
GLA_HV = GLA_DV // GLA_HEADS
GLA_GATE_RANK = 16
GLA_GATE_NORMALIZER = 16.0
GLA_MAIN = 2 * GLA_DK + 2 * GLA_DV
ATT_HEADS = 16
ATT_HD = D_MODEL // ATT_HEADS
BAND_PAST = 8 * CHUNK
MAX_REL = 256
D_FF = 5504
ALPHA = (2 * DEPTH) ** 0.25
LN_EPS = 1e-5
RMS_EPS = 1e-6
NEG = -1e30
LOG2E = 1.4426950408889634

LANE = 128
FF_TILE = 512
FF_LAST = D_FF - (D_FF - 1) // FF_TILE * FF_TILE
ROW_TILE = 512
FFN_ROW_TILE = 1024
A_IN_CAST_ROWS = 560
CAST_ROWS = 256
FFN_CAST_UP_ROWS = 16
PROJ_CAST_ROWS = 64
FFN_CAST_DOWN_ROWS = 128
PROJ_SUB = 128
MODMM_COLS = 1024
ATT_SUB = 256
ATT_HEADS_PER_STEP = 4
REL_LEN = BAND_PAST + 2 * ATT_SUB
MOD_ROWS = 8
ADA_ROWS = 2 * MOD_ROWS
ADA_TILE = 1024
VMEM_LIMIT = 56 * 1024 * 1024


def _params(*sem):
    return pltpu.CompilerParams(dimension_semantics=sem, vmem_limit_bytes=VMEM_LIMIT)


def _layer_norm(z, g, b):
    mu = jnp.mean(z, axis=-1, keepdims=True)
    zc = z - mu
    var = jnp.mean(zc * zc, axis=-1, keepdims=True)
    return zc * lax.rsqrt(var + LN_EPS) * g + b


def _silu(a):
    half = 0.5 * a
    return half + half * jnp.tanh(half)


def _ada_kernel(c_ref, w_ref, b_ref, o_ref):
    o_ref[...] = jnp.dot(_silu(c_ref[...]), w_ref[...], preferred_element_type=F32) + b_ref[...]


def _ada(c_all, w, b):
    n_l, _, n = w.shape
    return pl.pallas_call(
        _ada_kernel,
        grid=(n_l, n // ADA_TILE),
        in_specs=[
            pl.BlockSpec((ADA_ROWS, D_MODEL), lambda l, j: (0, 0)),
            pl.BlockSpec((None, D_MODEL, ADA_TILE), lambda l, j: (l, 0, j)),
            pl.BlockSpec((None, 1, ADA_TILE), lambda l, j: (l, 0, j)),
        ],
        out_specs=pl.BlockSpec((None, ADA_ROWS, ADA_TILE), lambda l, j: (l, 0, j)),
        out_shape=jax.ShapeDtypeStruct((n_l, ADA_ROWS, n), F32),
        compiler_params=_params("arbitrary", "arbitrary"),
        name="ada",
    )(c_all, w, b.reshape(n_l, 1, n))


class _Mod(NamedTuple):
    table: jax.Array
    layer: int
    row_block: int
    col: int
    seq: int | None

    @property
    def spec(self):
        return pl.BlockSpec((None, MOD_ROWS, D_MODEL), lambda *_: (self.layer, self.row_block, self.col))


def _mod_rows(ref, n_rows, seq):
    if seq is None:
        return ref[0:1, :]
    n_seq = n_rows // seq
    return jnp.broadcast_to(ref[0:n_seq, :][:, None, :], (n_seq, seq, D_MODEL)).reshape(n_rows, D_MODEL)


def _ffn_kernel(x_ref, sh_ref, sc_ref, gt_ref, wa_ref, wu_ref, wd_ref, g_ref, b_ref, o_ref, h_ref, *, seq):
    j = pl.program_id(1)
    last = pl.num_programs(1) - 1
    mod = lambda ref: _mod_rows(ref, x_ref.shape[0], seq)

    @pl.when(j == 0)
    def _():
        h_ref[...] = (x_ref[...] * (1.0 + mod(sc_ref)) + mod(sh_ref)).astype(BF16)
        o_ref[...] = jnp.zeros_like(o_ref)

    def chunk(cols):
        h = h_ref[...]
        a = jnp.dot(h, wa_ref[:, :cols], preferred_element_type=F32)
        u = jnp.dot(h, wu_ref[:, :cols], preferred_element_type=F32)
        o_ref[...] += jnp.dot((_silu(a) * u).astype(BF16), wd_ref[:cols, :], preferred_element_type=F32)

    @pl.when(j < last)
    def _():
        chunk(FF_TILE)

    @pl.when(j == last)
    def _():
        chunk(FF_LAST)
        z = ALPHA * x_ref[...] + (0.5 * (1.0 + mod(gt_ref))) * o_ref[...]
        o_ref[...] = _layer_norm(z, g_ref[...], b_ref[...])


def _ffn(x, shift, scale, gate, w_a, w_u, w_down, ln_g, ln_b, tm):
    m = x.shape[0]
    n_ff = pl.cdiv(D_FF, FF_TILE)
    vec = pl.BlockSpec((1, D_MODEL), lambda i, j: (0, 0))
    up = pl.BlockSpec((D_MODEL, FF_TILE), lambda i, j: (0, j))
    return pl.pallas_call(
        functools.partial(_ffn_kernel, seq=shift.seq),
        grid=(m // tm, n_ff),
        in_specs=[
            pl.BlockSpec((tm, D_MODEL), lambda i, j: (i, 0)),
            shift.spec, scale.spec, gate.spec,
            up, up,
            pl.BlockSpec((FF_TILE, D_MODEL), lambda i, j: (j, 0)),
            vec, vec,
        ],
        out_specs=pl.BlockSpec((tm, D_MODEL), lambda i, j: (i, 0)),
        out_shape=jax.ShapeDtypeStruct((m, D_MODEL), F32),
        scratch_shapes=[pltpu.VMEM((tm, D_MODEL), BF16)],
        compiler_params=_params("parallel", "arbitrary"),
        name="ffn",
    )(x, shift.table, scale.table, gate.table, w_a, w_u, w_down, ln_g, ln_b)


def _cast_kernel(x_ref, *o_refs):
    off = 0
    for o_ref in o_refs:
        n = o_ref.shape[-1]
        o_ref[...] = x_ref[:, off:off + n].astype(o_ref.dtype)
        off += n


class _CastJob(NamedTuple):
    src: jax.Array
    layer: int
    idx: int
    rows: int
    widths: tuple

    @property
    def n_blocks(self):
        return self.src.shape[2] // self.rows

    def specs(self, step_of):
        r, n = self.src.shape[2:]
        assert r % self.rows == 0 and sum(self.widths) <= n and all(wd % LANE == 0 for wd in self.widths)
        blk = lambda *ids: jnp.minimum(step_of(*ids), self.n_blocks - 1)
        in_spec = pl.BlockSpec((None, None, self.rows, n), lambda *ids: (self.layer, self.idx, blk(*ids), 0))
        out_specs = [pl.BlockSpec((self.rows, wd), lambda *ids: (blk(*ids), 0)) for wd in self.widths]
        out_shape = [jax.ShapeDtypeStruct((r, wd), BF16) for wd in self.widths]
        return in_spec, out_specs, out_shape


def _side_casts(step, n_steps, jobs, in_refs, out_refs):
    pos = 0
    for job, x_ref in zip(jobs, in_refs):
        outs = out_refs[pos:pos + len(job.widths)]
        pos += len(job.widths)
        if job.n_blocks == n_steps:
            _cast_kernel(x_ref, *outs)
        else:
            pl.when(step < job.n_blocks)(functools.partial(_cast_kernel, x_ref, *outs))


def _job_specs(jobs, step_of, n_steps):
    in_specs, out_specs, out_shape = [], [], []
    for job in jobs:
        assert job.n_blocks <= n_steps
        i, o, s = job.specs(step_of)
        in_specs.append(i)
        out_specs += o
        out_shape += s
    return in_specs, out_specs, out_shape


def _cast_split(w, widths, rows, layer=0, idx=0):
    job = _CastJob(w, layer, idx, rows, tuple(widths))
    in_spec, out_specs, out_shape = job.specs(lambda i: i)
    return pl.pallas_call(
        _cast_kernel,
        grid=(job.n_blocks,),
        in_specs=[in_spec],
        out_specs=out_specs,
        out_shape=out_shape,
        compiler_params=_params("parallel"),
        name="cast",
    )(w)


def _ffn_cast_jobs(w_ffn_up, w_ffn_down, layer, idx):
    return (_CastJob(w_ffn_up, layer, idx, FFN_CAST_UP_ROWS, (D_FF, D_FF)),
            _CastJob(w_ffn_down, layer, idx, FFN_CAST_DOWN_ROWS, (D_MODEL,)))


def _modmm_kernel(x_ref, sh_ref, sc_ref, w_ref, o_ref, *rest, seq, w_is_t, out_scale, tail_f32):
    h_ref = rest[-1]
    mod = lambda ref: _mod_rows(ref, x_ref.shape[0], seq)
    h_ref[...] = (x_ref[...] * (1.0 + mod(sc_ref)) + mod(sh_ref)).astype(BF16)
    h = h_ref[...]
    nt = (((1,), (1,)), ((), ()))

    def project(lo, hi):
        if w_is_t:
            y = lax.dot_general(h, w_ref[lo:hi, :], nt, preferred_element_type=F32)
        else:
            y = jnp.dot(h, w_ref[:, lo:hi], preferred_element_type=F32)
        return y if out_scale is None else y * out_scale

    n = o_ref.shape[1]
    if tail_f32:
        tail_ref, = rest[:-1]
        for lo in range(0, n, MODMM_COLS):
            y = project(lo, lo + MODMM_COLS)
            o_ref[:, lo:lo + MODMM_COLS] = y.astype(o_ref.dtype)
            tail_ref[:, lo:lo + MODMM_COLS] = y
        return
    for lo in range(0, n, MODMM_COLS):
        o_ref[:, lo:lo + MODMM_COLS] = project(lo, lo + MODMM_COLS).astype(o_ref.dtype)
    for extra_ref in rest[:-1]:
        extra_ref[...] = project(n, n + extra_ref.shape[1]).astype(extra_ref.dtype)
        n += extra_ref.shape[1]


def _modmm(x, shift, scale, w, tm, out_dtype, widths=None, w_is_t=False, out_scale=None, tail_f32=False):
    m = x.shape[0]
    widths = widths or (w.shape[0] if w_is_t else w.shape[1],)
    assert widths[0] % MODMM_COLS == 0 and not (tail_f32 and len(widths) > 1)
    out_specs = [pl.BlockSpec((tm, wd), lambda i: (i, 0)) for wd in widths]
    out_shape = [jax.ShapeDtypeStruct((m, wd), out_dtype) for wd in widths]
    if tail_f32:
        out_specs.append(pl.BlockSpec((tm, widths[0]), lambda i: (0, 0)))
        out_shape.append(jax.ShapeDtypeStruct((tm, widths[0]), F32))
    out = pl.pallas_call(
        functools.partial(_modmm_kernel, seq=shift.seq, w_is_t=w_is_t, out_scale=out_scale, tail_f32=tail_f32),
        grid=(m // tm,),
        in_specs=[
            pl.BlockSpec((tm, D_MODEL), lambda i: (i, 0)),
            shift.spec, scale.spec,
            pl.BlockSpec(w.shape, lambda i: (0, 0), pipeline_mode=pl.Buffered(1)),
        ],
        out_specs=out_specs,
        out_shape=out_shape,
        scratch_shapes=[pltpu.VMEM((tm, D_MODEL), BF16)],
        compiler_params=_params("arbitrary"),
        name="modmm",
    )(x, shift.table, scale.table, w)
    return out if len(out) > 1 else out[0]


def _proj_ln_kernel(a_ref, w_ref, x_ref, gt_ref, g_ref, b_ref, *rest, seq, jobs, n_steps):
    n_jobs = len(jobs)
    o_ref = rest[n_jobs]
    _side_casts(pl.program_id(0), n_steps, jobs, rest[:n_jobs], rest[n_jobs + 1:])
    sub = min(PROJ_SUB, a_ref.shape[0])
    n_sub = a_ref.shape[0] // sub
    dot = lambda r: jnp.dot(a_ref[r * sub:(r + 1) * sub, :], w_ref[...], preferred_element_type=F32)
    gate = _mod_rows(gt_ref, a_ref.shape[0], seq)
    y = dot(0)
    for r in range(n_sub):
        y_next = dot(r + 1) if r + 1 < n_sub else None
        rows = slice(r * sub, (r + 1) * sub)
        z = ALPHA * x_ref[rows, :] + (1.0 + gate[rows if gate.shape[0] > 1 else slice(None), :]) * y
        o_ref[rows, :] = _layer_norm(z, g_ref[...], b_ref[...])
        y = y_next


def _proj_ln(a, w, x, gate, ln_g, ln_b, tm, jobs=()):
    m = x.shape[0]
    vec = pl.BlockSpec((1, D_MODEL), lambda i: (0, 0))
    job_in, job_out, job_shape = _job_specs(jobs, lambda i: i, m // tm)
    return pl.pallas_call(
        functools.partial(_proj_ln_kernel, seq=gate.seq, jobs=jobs, n_steps=m // tm),
        grid=(m // tm,),
        in_specs=[
            pl.BlockSpec((tm, D_MODEL), lambda i: (i, 0)),
            pl.BlockSpec((D_MODEL, D_MODEL), lambda i: (0, 0)),
            pl.BlockSpec((tm, D_MODEL), lambda i: (i, 0)),
            gate.spec,
            vec, vec,
            *job_in,
        ],
        out_specs=[pl.BlockSpec((tm, D_MODEL), lambda i: (i, 0)), *job_out],
        out_shape=[jax.ShapeDtypeStruct((m, D_MODEL), F32), *job_shape],
        compiler_params=_params("arbitrary"),
        name="proj_ln",
    )(a, w, x, gate.table, ln_g, ln_b, *(job.src for job in jobs))


def _gla_kernel(q_ref, k_ref, v_ref, g_ref, gk_ref, w2_ref, bgk_ref, gn_ref, s0_ref, *rest,
                blk, n_blk, jobs, n_steps):
    n_jobs = len(jobs)
    o_ref, s_ref = rest[n_jobs:n_jobs + 2]
    step = (pl.program_id(0) * pl.num_programs(1) + pl.program_id(1)) * pl.num_programs(2) + pl.program_id(2)
    _side_casts(step, n_steps, jobs, rest[:n_jobs], rest[n_jobs + 2:])

    @pl.when(pl.program_id(2) == 0)
    def _():
        s_ref[...] = s0_ref[...]

    causal = (lax.broadcasted_iota(jnp.int32, (blk, blk), 0)
              >= lax.broadcasted_iota(jnp.int32, (blk, blk), 1))
    tril = jnp.broadcast_to(jnp.where(causal, 1.0, 0.0).astype(BF16), (n_blk, blk, blk))
    eye = (lax.broadcasted_iota(jnp.int32, (GLA_HK, GLA_HK), 0)
           == lax.broadcasted_iota(jnp.int32, (GLA_HK, GLA_HK), 1))
    tn = (((0,), (0,)), ((), ()))
    bnn = (((2,), (1,)), ((0,), (0,)))
    bnt = (((2,), (2,)), ((0,), (0,)))

    def blocks(t):
        return t.reshape(n_blk, blk, t.shape[-1])

    z = jnp.dot(gk_ref[...], w2_ref[...], preferred_element_type=F32) + bgk_ref[...]
    log_a = blocks((jnp.minimum(z, 0.0) - jnp.log(1.0 + jnp.exp(-jnp.abs(z)))) * (1.0 / GLA_GATE_NORMALIZER))
    hi = log_a.astype(BF16)
    lo = (log_a - hi.astype(F32)).astype(BF16)
    b = (lax.dot_general(tril, hi, bnn, preferred_element_type=F32)
         + lax.dot_general(tril, lo, bnn, preferred_element_type=F32))
    b_last = b[:, blk - 1:blk, :]
    q = blocks(q_ref[...].astype(F32)) * (GLA_HK ** -0.5)
    k = blocks(k_ref[...].astype(F32))
    v = blocks(v_ref[...])
    q_t = (q * jnp.exp(b)).astype(BF16)
    k_t = (k * jnp.exp(-b)).astype(BF16)
    k_dec = (k * jnp.exp(b_last - b)).astype(BF16)
    scores = lax.dot_general(q_t, k_t, bnt, preferred_element_type=F32)
    scores = jnp.where(causal[None], scores, 0.0).astype(BF16)
    o_intra = lax.dot_general(scores, v, bnn, preferred_element_type=F32)
    decay = jnp.exp(b_last)

    update = [lax.dot_general(k_dec[c], v[c], tn, preferred_element_type=F32) for c in range(n_blk)]
    decay_col = [jnp.sum(jnp.where(eye, jnp.broadcast_to(decay[c], (GLA_HK, GLA_HK)), 0.0),
                         axis=1, keepdims=True) for c in range(n_blk)]
    state = s_ref[0, 0]
    starts = []
    for c in range(n_blk):
        starts.append(state.astype(BF16))
        state = decay_col[c] * state + update[c]
    s_ref[0, 0] = state
    o = o_intra + jnp.stack([jnp.dot(q_t[c], starts[c], preferred_element_type=F32) for c in range(n_blk)])
    on = o * lax.rsqrt(jnp.mean(o * o, axis=-1, keepdims=True) + RMS_EPS) * gn_ref[...]
    gate = _silu(g_ref[...].astype(F32)).reshape(o.shape)
    o_ref[...] = (on * gate).reshape(o_ref.shape).astype(o_ref.dtype)


def _gla(proj, gk_low, w_gk2, b_gk, g_norm, s0, batch, seq, jobs=()):
    blk = min(CHUNK, seq)
    rows = min(ROW_TILE, seq)
    n_steps = seq // rows
    m = batch * seq
    k_off = GLA_DK // GLA_HK
    v_off = 2 * GLA_DK // GLA_HV
    g_off = (2 * GLA_DK + GLA_DV) // GLA_HV
    row = lambda b, h, c: b * n_steps + c
    state_spec = pl.BlockSpec((1, 1, GLA_HK, GLA_HV), lambda b, h, c: (b, h, 0, 0))
    grid_steps = batch * GLA_HEADS * n_steps
    job_in, job_out, job_shape = _job_specs(jobs, lambda b, h, c: (b * GLA_HEADS + h) * n_steps + c, grid_steps)
    return pl.pallas_call(
        functools.partial(_gla_kernel, blk=blk, n_blk=rows // blk, jobs=jobs, n_steps=grid_steps),
        grid=(batch, GLA_HEADS, n_steps),
        in_specs=[
            pl.BlockSpec((rows, GLA_HK), lambda b, h, c: (row(b, h, c), h)),
            pl.BlockSpec((rows, GLA_HK), lambda b, h, c: (row(b, h, c), k_off + h)),
            pl.BlockSpec((rows, GLA_HV), lambda b, h, c: (row(b, h, c), v_off + h)),
            pl.BlockSpec((rows, GLA_HV), lambda b, h, c: (row(b, h, c), g_off + h)),
            pl.BlockSpec((rows, GLA_GATE_RANK), lambda b, h, c: (row(b, h, c), 0)),
            pl.BlockSpec((GLA_GATE_RANK, GLA_HK), lambda b, h, c: (0, h)),
            pl.BlockSpec((1, GLA_HK), lambda b, h, c: (0, h)),
            pl.BlockSpec((1, GLA_HV), lambda b, h, c: (0, 0)),
            state_spec,
            *job_in,
        ],
        out_specs=[
            pl.BlockSpec((rows, GLA_HV), lambda b, h, c: (row(b, h, c), h)),
            state_spec,
            *job_out,
        ],
        out_shape=[
            jax.ShapeDtypeStruct((m, GLA_DV), BF16),
            jax.ShapeDtypeStruct(s0.shape, F32),
            *job_shape,
        ],
        compiler_params=_params("arbitrary", "arbitrary", "arbitrary"),
        name="gla",
    )(proj, proj, proj, proj, gk_low, w_gk2, b_gk, g_norm, s0, *(job.src for job in jobs))


def _band_attn_kernel(q_ref, kp_ref, kc_ref, vp_ref, vc_ref, rel_ref, *rest, jobs, n_steps):
    n_jobs = len(jobs)
    o_ref, bias_ref = rest[n_jobs], rest[-1]
    i = pl.program_id(1)
    _side_casts(pl.program_id(0) * pl.num_programs(1) + i, n_steps, jobs, rest[:n_jobs], rest[n_jobs + 1:-1])
    nt = (((1,), (1,)), ((), ()))
    n_keys = BAND_PAST + ATT_SUB
    n_sub = BAND_PAST // ATT_SUB

    @pl.when(i <= 1)
    def _():
        col = lax.broadcasted_iota(jnp.int32, (ATT_SUB, n_keys), 1)
        qc = lax.broadcasted_iota(jnp.int32, (ATT_SUB, n_keys), 0) // CHUNK
        kc = col // CHUNK
        in_band = (kc >= qc) & (kc <= qc + BAND_PAST // CHUNK)
        for h in range(ATT_HEADS_PER_STEP):
            rel = jnp.broadcast_to(rel_ref[h], (ATT_SUB, REL_LEN))
            toeplitz = pltpu.roll(rel, 0, 1, stride=1, stride_axis=0)[:, ATT_SUB:] * LOG2E
            for s in range(n_sub):
                key_pos = (i - 1) * BAND_PAST + s * ATT_SUB + col
                bias_ref[h, s] = jnp.where(in_band & (key_pos >= 0), toeplitz, NEG)

    units = [(h, s) for h in range(ATT_HEADS_PER_STEP) for s in range(n_sub)]

    def scores(h, s):
        lo, hi = s * ATT_SUB, (s + 1) * ATT_SUB
        cols = slice(h * ATT_HD, (h + 1) * ATT_HD)
        k = jnp.concatenate([kp_ref[lo:, cols], kc_ref[:hi, cols]], axis=0)
        return lax.dot_general(q_ref[lo:hi, cols], k, nt, preferred_element_type=F32) + bias_ref[h, s]

    def finish(h, s, sc):
        lo, hi = s * ATT_SUB, (s + 1) * ATT_SUB
        cols = slice(h * ATT_HD, (h + 1) * ATT_HD)
        v = jnp.concatenate([vp_ref[lo:, cols], vc_ref[:hi, cols]], axis=0)
        p = jnp.exp2(sc - jnp.max(sc, axis=-1, keepdims=True))
        denom = jnp.sum(p, axis=-1, keepdims=True)
        o = jnp.dot(p.astype(BF16), v, preferred_element_type=F32) / denom
        o_ref[lo:hi, cols] = o.astype(o_ref.dtype)

    sc = scores(*units[0])
    for n, unit in enumerate(units):
        sc_next = scores(*units[n + 1]) if n + 1 < len(units) else None
        finish(*unit, sc)
        sc = sc_next


def _band_rel(table):
    assert BAND_PAST + ATT_SUB - REL_LEN // 2 == MAX_REL and REL_LEN // 2 <= 2 * MAX_REL
    far = jnp.broadcast_to(table[:, 2 * MAX_REL:], (ATT_HEADS, REL_LEN // 2))
    near = table[:, 2 * MAX_REL - REL_LEN // 2 + 1:][:, ::-1]
    return jnp.concatenate([far, near], axis=1).astype(F32)[:, None, :]


def _band_attn(q, kv, table, jobs=()):
    t = q.shape[0]
    n_keys = BAND_PAST + ATT_SUB
    prev = lambda i: jnp.maximum(i - 1, 0)
    n_groups = ATT_HEADS // ATT_HEADS_PER_STEP
    blk = (BAND_PAST, ATT_HEADS_PER_STEP * ATT_HD)
    n_blocks = t // BAND_PAST
    job_in, job_out, job_shape = _job_specs(jobs, lambda h, i: h * n_blocks + i, n_groups * n_blocks)
    return pl.pallas_call(
        functools.partial(_band_attn_kernel, jobs=jobs, n_steps=n_groups * n_blocks),
        grid=(n_groups, n_blocks),
        in_specs=[
            pl.BlockSpec(blk, lambda h, i: (i, h)),
            pl.BlockSpec(blk, lambda h, i: (prev(i), h)),
            pl.BlockSpec(blk, lambda h, i: (i, h)),
            pl.BlockSpec(blk, lambda h, i: (prev(i), n_groups + h)),
            pl.BlockSpec(blk, lambda h, i: (i, n_groups + h)),
            pl.BlockSpec((ATT_HEADS_PER_STEP, 1, REL_LEN), lambda h, i: (h, 0, 0)),
            *job_in,
        ],
        out_specs=[pl.BlockSpec(blk, lambda h, i: (i, h)), *job_out],
        out_shape=[jax.ShapeDtypeStruct((t, D_MODEL), BF16), *job_shape],
        scratch_shapes=[pltpu.VMEM((ATT_HEADS_PER_STEP, BAND_PAST // ATT_SUB, ATT_SUB, n_keys), F32)],
        compiler_params=_params("arbitrary", "arbitrary"),
        name="band_attn",
    )(q, kv, kv, kv, kv, _band_rel(table), *(job.src for job in jobs))


def _step_attn_kernel(q_ref, kvn_ref, ck_ref, cv_ref, bp_ref, bn_ref, o_ref, bias_ref):
    nt = (((1,), (1,)), ((), ()))
    n_past = ck_ref.shape[1]
    n_keys = n_past * ATT_HEADS

    @pl.when(pl.program_id(0) == 0)
    def _():
        frame = lax.broadcasted_iota(jnp.int32, (n_past, n_keys), 0)
        key = lax.broadcasted_iota(jnp.int32, (n_past, n_keys), 1)
        spread = jnp.where(key // ATT_HEADS == frame, 1.0, 0.0).astype(BF16)
        bp = bp_ref[...]
        hi = bp.astype(BF16)
        lo = (bp - hi.astype(F32)).astype(BF16)
        wide = (jnp.dot(hi, spread, preferred_element_type=F32)
                + jnp.dot(lo, spread, preferred_element_type=F32))
        q_head = lax.broadcasted_iota(jnp.int32, wide.shape, 0) // q_ref.shape[0]
        k_head = lax.broadcasted_iota(jnp.int32, wide.shape, 1) % ATT_HEADS
        bias_ref[...] = jnp.where(q_head == k_head, wide, NEG)

    heads = lambda ref, off: jnp.concatenate(
        [ref[:, off + h * ATT_HD:off + (h + 1) * ATT_HD] for h in range(ATT_HEADS)], axis=0)
    q = heads(q_ref, 0)
    k_new = heads(kvn_ref, 0).astype(BF16)
    v_new = heads(kvn_ref, D_MODEL).astype(BF16)
    k_past = ck_ref[0].reshape(n_keys, ATT_HD).astype(BF16)
    v_past = cv_ref[0].reshape(n_keys, ATT_HD).astype(BF16)
    sp = lax.dot_general(q, k_past, nt, preferred_element_type=F32) + bias_ref[...]
    sn = lax.dot_general(q, k_new, nt, preferred_element_type=F32) + bn_ref[...]
    mx = jnp.maximum(jnp.max(sp, axis=-1, keepdims=True), jnp.max(sn, axis=-1, keepdims=True))
    pp = jnp.exp2(sp - mx)
    pn = jnp.exp2(sn - mx)
    denom = jnp.sum(pp, axis=-1, keepdims=True) + jnp.sum(pn, axis=-1, keepdims=True)
    o = (jnp.dot(pp.astype(BF16), v_past, preferred_element_type=F32)
         + jnp.dot(pn.astype(BF16), v_new, preferred_element_type=F32)) / denom
    seq = q_ref.shape[0]
    for h in range(ATT_HEADS):
        o_ref[:, h * ATT_HD:(h + 1) * ATT_HD] = o[h * seq:(h + 1) * seq, :].astype(o_ref.dtype)


def _step_attn(q, kv_new, cache_k, cache_v, table, batch, seq):
    n_past = cache_k.shape[1]
    dist = n_past + jnp.arange(seq)[:, None] - jnp.arange(n_past + seq)[None, :]
    bias = table[:, jnp.clip(dist, -MAX_REL, MAX_REL) + MAX_REL].astype(F32) * LOG2E
    rows = ATT_HEADS * seq
    bias_past = bias[:, :, :n_past].reshape(rows, n_past)
    same_head = jnp.eye(ATT_HEADS, dtype=bool)[:, None, :, None]
    bias_new = jnp.where(same_head, bias[:, :, None, n_past:], NEG).reshape(rows, rows)
    return pl.pallas_call(
        _step_attn_kernel,
        grid=(batch,),
        in_specs=[
            pl.BlockSpec((seq, D_MODEL), lambda b: (b, 0)),
            pl.BlockSpec((seq, 2 * D_MODEL), lambda b: (b, 0)),
            pl.BlockSpec((1, n_past, ATT_HEADS, ATT_HD), lambda b: (b, 0, 0, 0)),
            pl.BlockSpec((1, n_past, ATT_HEADS, ATT_HD), lambda b: (b, 0, 0, 0)),
            pl.BlockSpec((rows, n_past), lambda b: (0, 0)),
            pl.BlockSpec((rows, rows), lambda b: (0, 0)),
        ],
        out_specs=pl.BlockSpec((seq, D_MODEL), lambda b: (b, 0)),
        out_shape=jax.ShapeDtypeStruct((batch * seq, D_MODEL), BF16),
        scratch_shapes=[pltpu.VMEM((rows, n_past * ATT_HEADS), F32)],
        compiler_params=_params("arbitrary"),
        name="step_attn",
    )(q, kv_new, cache_k, cache_v, bias_past, bias_new)


def _trunk(x, ada, ada_kv, row_block, gla_s0, past, wts):
    batch, seq, _ = x.shape
    m = batch * seq
    tm = min(ROW_TILE, m)
    x = x.reshape(m, D_MODEL)
    assert batch <= MOD_ROWS and (batch == 1 or m == tm)
    per_seq = None if batch == 1 else seq

    def mod(l, sub, which):
        return _Mod(ada, l, row_block, 3 * sub + which, per_seq)

    def ffn(x, l, sub, idx):
        return _ffn(x, mod(l, sub, 0), mod(l, sub, 1), mod(l, sub, 2), *wts["ffn"][l, idx],
                    wts["ln_g"][l, sub][None], wts["ln_b"][l, sub][None], min(FFN_ROW_TILE, m))

    def side_jobs(ffn_keys, names, rows):
        ffn_keys = [key for key in ffn_keys if key not in wts["ffn"]]
        names = [name for name in names if name not in wts]
        jobs = tuple(job for key in ffn_keys for job in _ffn_cast_jobs(*wts["ffn_f32"], *key))
        jobs += tuple(_CastJob(wts["f32"][name], 0, 0, rows, (wts["f32"][name].shape[-1],)) for name in names)

        def keep(cast):
            for n, key in enumerate(ffn_keys):
                wts["ffn"][key] = tuple(cast[3 * n:3 * n + 3])
            for n, name in enumerate(names):
                wts[name] = cast[3 * len(ffn_keys) + n]
        return jobs, keep

    x = ffn(x, 0, 0, 0)
    proj, gk_low = _modmm(x, mod(0, 1, 0), mod(0, 1, 1), wts["a_in_t"], tm, BF16,
                          widths=(GLA_MAIN, GLA_GATE_RANK), w_is_t=True)
    jobs, keep = side_jobs(((0, 1), (1, 0)), ("a_out",), FFN_CAST_UP_ROWS)
    o, gla_state, *cast = _gla(proj, gk_low, wts["a_gk2"], wts["b_a_gk"], wts["g_a_norm"], gla_s0, batch, seq, jobs)
    keep(cast)
    jobs, keep = side_jobs((), ("kv", "b_q", "b_out"), PROJ_CAST_ROWS)
    x, *cast = _proj_ln(o, wts["a_out"], x, mod(0, 1, 2), wts["ln_g"][0, 1][None], wts["ln_b"][0, 1][None], tm, jobs)
    keep(cast)
    x = ffn(x, 0, 2, 1)

    kv_shift = _Mod(ada_kv, 0, row_block, 0, per_seq)
    kv_scale = _Mod(ada_kv, 0, row_block, 1, per_seq)
    if past is None:
        assert tm == min(BAND_PAST, seq)
        kv, kv_out = _modmm(x, kv_shift, kv_scale, wts["kv"], tm, BF16, tail_f32=True)
    else:
        kv_out = _modmm(x, kv_shift, kv_scale, wts["kv"], tm, F32)

    x = ffn(x, 1, 0, 0)
    q = _modmm(x, mod(1, 1, 0), mod(1, 1, 1), wts["b_q"], tm, BF16,
               out_scale=ATT_HD ** -0.5 * LOG2E)
    if past is None:
        jobs, keep = side_jobs(((1, 1),), (), None)
        o, *cast = _band_attn(q, kv, wts["rel_bias"], jobs)
        keep(cast)
    else:
        o = _step_attn(q, kv_out, past[0], past[1], wts["rel_bias"], batch, seq)
    x, = _proj_ln(o, wts["b_out"], x, mod(1, 1, 2), wts["ln_g"][1, 1][None], wts["ln_b"][1, 1][None], tm)
    x = ffn(x, 1, 2, 1)

    n_rows = kv_out.shape[0] // batch
    k_out = kv_out[:, :D_MODEL].reshape(batch, n_rows, ATT_HEADS, ATT_HD)
    v_out = kv_out[:, D_MODEL:].reshape(batch, n_rows, ATT_HEADS, ATT_HD)
    return x.reshape(batch, seq, D_MODEL), gla_state[None], k_out, v_out


def kernel(x_prompt, x_sample, state_gla, cache_band_k, cache_band_v, c_prompt, c_sample, w_ada, b_ada, ln_g, ln_b, w_ffn_up, w_ffn_down, w_a_in, w_a_gk2, b_a_gk, g_a_norm, w_a_out, w_ada_kv, b_ada_kv, w_kv, w_b_q, rel_bias, w_b_out):
    assert DEPTH == 2 and w_a_in.shape[0] == 1 and w_b_q.shape[0] == 1
    n_prompt, n_sample = x_prompt.shape[0], x_sample.shape[0]
    assert n_prompt == 1 and n_sample <= MOD_ROWS

    lead = lambda w: w.reshape((1, 1) + w.shape[-2:])
    wts = {
        "ffn_f32": (w_ffn_up, w_ffn_down),
        "ffn": {(0, 0): (*_cast_split(w_ffn_up, (D_FF, D_FF), CAST_ROWS),
                         *_cast_split(w_ffn_down, (D_MODEL,), D_FF // 8))},
        "a_in_t": _cast_split(lead(jnp.swapaxes(w_a_in, 1, 2)), (D_MODEL,), A_IN_CAST_ROWS)[0],
        "a_gk2": w_a_gk2[0].astype(BF16),
        "b_a_gk": b_a_gk[0][None],
        "g_a_norm": g_a_norm[0][None],
        "f32": {"a_out": lead(w_a_out), "kv": lead(w_kv), "b_q": lead(w_b_q), "b_out": lead(w_b_out)},
        "rel_bias": rel_bias[0],
        "ln_g": ln_g,
        "ln_b": ln_b,
    }

    c_all = jnp.concatenate([jnp.pad(c, ((0, MOD_ROWS - c.shape[0]), (0, 0))) for c in (c_prompt, c_sample)], axis=0)
    ada = _ada(c_all, w_ada, b_ada)
    ada_kv = _ada(c_all, w_ada_kv[None], b_ada_kv[None])

    gla_zero = jnp.zeros((n_prompt, GLA_HEADS, GLA_HK, GLA_HV), state_gla.dtype)
    y_p, s_p, k_p, v_p = _trunk(x_prompt, ada, ada_kv, 0, gla_zero, None, wts)
    past = (cache_band_k, cache_band_v)
    y_s, s_s, k_s, v_s = _trunk(x_sample, ada, ada_kv, 1, state_gla[0], past, wts)
    return (y_p, y_s, s_p, s_s, k_p, v_p, k_s, v_s)
```

```python
import functools
from typing import NamedTuple

import jax
import jax.numpy as jnp
from jax import lax
from jax.experimental import pallas as pl
from jax.experimental.pallas import tpu as pltpu

F32 = jnp.float32
BF16 = jnp.bfloat16

D_MODEL = 2048
DEPTH = 2
CHUNK = 64
GLA_HEADS = 4
GLA_DK = D_MODEL // 2
GLA_DV = D_MODEL
GLA_HK = GLA_DK // GLA_HEADS
GLA_HV = GLA_DV // GLA_HEADS
GLA_GATE_RANK = 16
GLA_GATE_NORMALIZER = 16.0
GLA_MAIN = 2 * GLA_DK + 2 * GLA_DV
ATT_HEADS = 16
ATT_HD = D_MODEL // ATT_HEADS
BAND_PAST = 8 * CHUNK
MAX_REL = 256
D_FF = 5504
ALPHA = (2 * DEPTH) ** 0.25
LN_EPS = 1e-5
RMS_EPS = 1e-6
NEG = -1e30
LOG2E = 1.4426950408889634

LANE = 128
FF_TILE = 512
FF_LAST = D_FF - (D_FF - 1) // FF_TILE * FF_TILE
ROW_TILE = 512
FFN_ROW_TILE = 1024
A_IN_CAST_ROWS = 560
CAST_ROWS = 256
FFN_CAST_DOWN_ROWS = 128
GLA_HEADS_PER_STEP = 2
PROJ_SUB = 128
MODMM_COLS = 1024
ATT_SUB = 256
ATT_HEADS_PER_STEP = 8
REL_LEN = BAND_PAST + 2 * ATT_SUB
MOD_ROWS = 8
ADA_ROWS = 2 * MOD_ROWS
ADA_TILE = 1024
VMEM_LIMIT = 56 * 1024 * 1024


def _params(*sem):
    return pltpu.CompilerParams(dimension_semantics=sem, vmem_limit_bytes=VMEM_LIMIT)


def _layer_norm(z, g, b):
    mu = jnp.mean(z, axis=-1, keepdims=True)
    zc = z - mu
    var = jnp.mean(zc * zc, axis=-1, keepdims=True)
    return zc * lax.rsqrt(var + LN_EPS) * g + b


def _silu(a):
    half = 0.5 * a
    return half + half * jnp.tanh(half)


def _ada_kernel(c_ref, w_ref, b_ref, o_ref):
    o_ref[...] = jnp.dot(_silu(c_ref[...]), w_ref[...], preferred_element_type=F32) + b_ref[...]


def _ada(c_all, w, b):
    n_l, _, n = w.shape
    return pl.pallas_call(
        _ada_kernel,
        grid=(n_l, n // ADA_TILE),
        in_specs=[
            pl.BlockSpec((ADA_ROWS, D_MODEL), lambda l, j: (0, 0)),
            pl.BlockSpec((None, D_MODEL, ADA_TILE), lambda l, j: (l, 0, j)),
            pl.BlockSpec((None, 1, ADA_TILE), lambda l, j: (l, 0, j)),
        ],
        out_specs=pl.BlockSpec((None, ADA_ROWS, ADA_TILE), lambda l, j: (l, 0, j)),
        out_shape=jax.ShapeDtypeStruct((n_l, ADA_ROWS, n), F32),
        compiler_params=_params("arbitrary", "arbitrary"),
        name="ada",
    )(c_all, w, b.reshape(n_l, 1, n))


class _Mod(NamedTuple):
    table: jax.Array
    layer: int
    row_block: int
    col: int
    seq: int | None

    @property
    def spec(self):
        return pl.BlockSpec((None, MOD_ROWS, D_MODEL), lambda *_: (self.layer, self.row_block, self.col))


def _mod_rows(ref, n_rows, seq):
    if seq is None:
        return ref[0:1, :]
    n_seq = n_rows // seq
    return jnp.broadcast_to(ref[0:n_seq, :][:, None, :], (n_seq, seq, D_MODEL)).reshape(n_rows, D_MODEL)


def _ffn_kernel(x_ref, sh_ref, sc_ref, gt_ref, wa_ref, wu_ref, wd_ref, g_ref, b_ref, o_ref, h_ref, *, seq):
    j = pl.program_id(1)
    last = pl.num_programs(1) - 1
    mod = lambda ref: _mod_rows(ref, x_ref.shape[0], seq)

    @pl.when(j == 0)
    def _():
        h_ref[...] = (x_ref[...] * (1.0 + mod(sc_ref)) + mod(sh_ref)).astype(BF16)
        o_ref[...] = jnp.zeros_like(o_ref)

    def chunk(cols):
        h = h_ref[...]
        a = jnp.dot(h, wa_ref[:, :cols], preferred_element_type=F32)
        u = jnp.dot(h, wu_ref[:, :cols], preferred_element_type=F32)
        o_ref[...] += jnp.dot((_silu(a) * u).astype(BF16), wd_ref[:cols, :], preferred_element_type=F32)

    @pl.when(j < last)
    def _():
        chunk(FF_TILE)

    @pl.when(j == last)
    def _():
        chunk(FF_LAST)
        z = ALPHA * x_ref[...] + (0.5 * (1.0 + mod(gt_ref))) * o_ref[...]
        o_ref[...] = _layer_norm(z, g_ref[...], b_ref[...])


def _ffn(x, shift, scale, gate, w_a, w_u, w_down, ln_g, ln_b, tm):
    m = x.shape[0]
    n_ff = pl.cdiv(D_FF, FF_TILE)
    vec = pl.BlockSpec((1, D_MODEL), lambda i, j: (0, 0))
    up = pl.BlockSpec((D_MODEL, FF_TILE), lambda i, j: (0, j))
    return pl.pallas_call(
        functools.partial(_ffn_kernel, seq=shift.seq),
        grid=(m // tm, n_ff),
        in_specs=[
            pl.BlockSpec((tm, D_MODEL), lambda i, j: (i, 0)),
            shift.spec, scale.spec, gate.spec,
            up, up,
            pl.BlockSpec((FF_TILE, D_MODEL), lambda i, j: (j, 0)),
            vec, vec,
        ],
        out_specs=pl.BlockSpec((tm, D_MODEL), lambda i, j: (i, 0)),
        out_shape=jax.ShapeDtypeStruct((m, D_MODEL), F32),
        scratch_shapes=[pltpu.VMEM((tm, D_MODEL), BF16)],
        compiler_params=_params("parallel", "arbitrary"),
        name="ffn",
    )(x, shift.table, scale.table, gate.table, w_a, w_u, w_down, ln_g, ln_b)


def _cast_kernel(x_ref, *o_refs):
    off = 0
    for o_ref in o_refs:
        n = o_ref.shape[-1]
        o_ref[...] = x_ref[:, off:off + n].astype(o_ref.dtype)
        off += n


class _CastJob(NamedTuple):
    src: jax.Array
    layer: int
    idx: int
    rows: int
    widths: tuple

    @property
    def n_blocks(self):
        return self.src.shape[2] // self.rows

    def specs(self, step_of):
        r, n = self.src.shape[2:]
        assert r % self.rows == 0 and sum(self.widths) <= n and all(wd % LANE == 0 for wd in self.widths)
        blk = lambda *ids: jnp.minimum(step_of(*ids), self.n_blocks - 1)
        in_spec = pl.BlockSpec((None, None, self.rows, n), lambda *ids: (self.layer, self.idx, blk(*ids), 0))
        out_specs = [pl.BlockSpec((self.rows, wd), lambda *ids: (blk(*ids), 0)) for wd in self.widths]
        out_shape = [jax.ShapeDtypeStruct((r, wd), BF16) for wd in self.widths]
        return in_spec, out_specs, out_shape


def _side_casts(step, n_steps, jobs, in_refs, out_refs):
    pos = 0
    for job, x_ref in zip(jobs, in_refs):
        outs = out_refs[pos:pos + len(job.widths)]
        pos += len(job.widths)
        if job.n_blocks == n_steps:
            _cast_kernel(x_ref, *outs)
        else:
            pl.when(step < job.n_blocks)(functools.partial(_cast_kernel, x_ref, *outs))


def _job_specs(jobs, step_of, n_steps):
    in_specs, out_specs, out_shape = [], [], []
    for job in jobs:
        assert job.n_blocks <= n_steps
        i, o, s = job.specs(step_of)
        in_specs.append(i)
        out_specs += o
        out_shape += s
    return in_specs, out_specs, out_shape


def _cast_split(w, widths, rows, layer=0, idx=0):
    job = _CastJob(w, layer, idx, rows, tuple(widths))
    in_spec, out_specs, out_shape = job.specs(lambda i: i)
    return pl.pallas_call(
        _cast_kernel,
        grid=(job.n_blocks,),
        in_specs=[in_spec],
        out_specs=out_specs,
        out_shape=out_shape,
        compiler_params=_params("parallel"),
        name="cast",
    )(w)


def _ffn_cast_jobs(w_ffn_up, w_ffn_down, layer, idx, n_steps):
    return (_CastJob(w_ffn_up, layer, idx, D_MODEL // n_steps, (D_FF, D_FF)),
            _CastJob(w_ffn_down, layer, idx, FFN_CAST_DOWN_ROWS, (D_MODEL,)))


def _modmm_kernel(x_ref, sh_ref, sc_ref, w_ref, o_ref, *rest, seq, w_is_t, out_scale, tail_f32):
    h_ref = rest[-1]
    mod = lambda ref: _mod_rows(ref, x_ref.shape[0], seq)
    h_ref[...] = (x_ref[...] * (1.0 + mod(sc_ref)) + mod(sh_ref)).astype(BF16)
    h = h_ref[...]
    nt = (((1,), (1,)), ((), ()))

    def project(lo, hi):
        if w_is_t:
            y = lax.dot_general(h, w_ref[lo:hi, :], nt, preferred_element_type=F32)
        else:
            y = jnp.dot(h, w_ref[:, lo:hi], preferred_element_type=F32)
        return y if out_scale is None else y * out_scale

    n = o_ref.shape[1]
    if tail_f32:
        tail_ref, = rest[:-1]
        for lo in range(0, n, MODMM_COLS):
            y = project(lo, lo + MODMM_COLS)
            o_ref[:, lo:lo + MODMM_COLS] = y.astype(o_ref.dtype)
            tail_ref[:, lo:lo + MODMM_COLS] = y
        return
    for lo in range(0, n, MODMM_COLS):
        o_ref[:, lo:lo + MODMM_COLS] = project(lo, lo + MODMM_COLS).astype(o_ref.dtype)
    for extra_ref in rest[:-1]:
        extra_ref[...] = project(n, n + extra_ref.shape[1]).astype(extra_ref.dtype)
        n += extra_ref.shape[1]


def _modmm(x, shift, scale, w, tm, out_dtype, widths=None, w_is_t=False, out_scale=None, tail_f32=False):
    m = x.shape[0]
    widths = widths or (w.shape[0] if w_is_t else w.shape[1],)
    assert widths[0] % MODMM_COLS == 0 and not (tail_f32 and len(widths) > 1)
    out_specs = [pl.BlockSpec((tm, wd), lambda i: (i, 0)) for wd in widths]
    out_shape = [jax.ShapeDtypeStruct((m, wd), out_dtype) for wd in widths]
    if tail_f32:
        out_specs.append(pl.BlockSpec((tm, widths[0]), lambda i: (0, 0)))
        out_shape.append(jax.ShapeDtypeStruct((tm, widths[0]), F32))
    out = pl.pallas_call(
        functools.partial(_modmm_kernel, seq=shift.seq, w_is_t=w_is_t, out_scale=out_scale, tail_f32=tail_f32),
        grid=(m // tm,),
        in_specs=[
            pl.BlockSpec((tm, D_MODEL), lambda i: (i, 0)),
            shift.spec, scale.spec,
            pl.BlockSpec(w.shape, lambda i: (0, 0), pipeline_mode=pl.Buffered(1)),
        ],
        out_specs=out_specs,
        out_shape=out_shape,
        scratch_shapes=[pltpu.VMEM((tm, D_MODEL), BF16)],
        compiler_params=_params("arbitrary"),
        name="modmm",
    )(x, shift.table, scale.table, w)
    return out if len(out) > 1 else out[0]


def _proj_ln_kernel(a_ref, w_ref, x_ref, gt_ref, g_ref, b_ref, *rest, seq, jobs, n_steps):
    n_jobs = len(jobs)
    o_ref = rest[n_jobs]
    _side_casts(pl.program_id(0), n_steps, jobs, rest[:n_jobs], rest[n_jobs + 1:])
    sub = min(PROJ_SUB, a_ref.shape[0])
    n_sub = a_ref.shape[0] // sub
    dot = lambda r: jnp.dot(a_ref[r * sub:(r + 1) * sub, :], w_ref[...], preferred_element_type=F32)
    gate = _mod_rows(gt_ref, a_ref.shape[0], seq)
    y = dot(0)
    for r in range(n_sub):
        y_next = dot(r + 1) if r + 1 < n_sub else None
        rows = slice(r * sub, (r + 1) * sub)
        z = ALPHA * x_ref[rows, :] + (1.0 + gate[rows if gate.shape[0] > 1 else slice(None), :]) * y
        o_ref[rows, :] = _layer_norm(z, g_ref[...], b_ref[...])
        y = y_next


def _proj_ln(a, w, x, gate, ln_g, ln_b, tm, jobs=()):
    m = x.shape[0]
    vec = pl.BlockSpec((1, D_MODEL), lambda i: (0, 0))
    job_in, job_out, job_shape = _job_specs(jobs, lambda i: i, m // tm)
    return pl.pallas_call(
        functools.partial(_proj_ln_kernel, seq=gate.seq, jobs=jobs, n_steps=m // tm),
        grid=(m // tm,),
        in_specs=[
            pl.BlockSpec((tm, D_MODEL), lambda i: (i, 0)),
            pl.BlockSpec((D_MODEL, D_MODEL), lambda i: (0, 0)),
            pl.BlockSpec((tm, D_MODEL), lambda i: (i, 0)),
            gate.spec,
            vec, vec,
            *job_in,
        ],
        out_specs=[pl.BlockSpec((tm, D_MODEL), lambda i: (i, 0)), *job_out],
        out_shape=[jax.ShapeDtypeStruct((m, D_MODEL), F32), *job_shape],
        compiler_params=_params("arbitrary"),
        name="proj_ln",
    )(a, w, x, gate.table, ln_g, ln_b, *(job.src for job in jobs))


def _gla_kernel(q_ref, k_ref, v_ref, g_ref, gk_ref, w2_ref, bgk_ref, gn_ref, s0_ref, *rest,
                blk, n_blk, n_heads, jobs, n_steps):
    n_jobs = len(jobs)
    o_ref, s_ref = rest[n_jobs:n_jobs + 2]
    step = (pl.program_id(0) * pl.num_programs(1) + pl.program_id(1)) * pl.num_programs(2) + pl.program_id(2)
    _side_casts(step, n_steps, jobs, rest[:n_jobs], rest[n_jobs + 2:])

    @pl.when(pl.program_id(2) == 0)
    def _():
        s_ref[...] = s0_ref[...]

    causal = (lax.broadcasted_iota(jnp.int32, (blk, blk), 0)
              >= lax.broadcasted_iota(jnp.int32, (blk, blk), 1))
    tril = jnp.broadcast_to(jnp.where(causal, 1.0, 0.0).astype(BF16), (n_blk, blk, blk))
    eye = (lax.broadcasted_iota(jnp.int32, (GLA_HK, GLA_HK), 0)
           == lax.broadcasted_iota(jnp.int32, (GLA_HK, GLA_HK), 1))
    tn = (((0,), (0,)), ((), ()))
    bnn = (((2,), (1,)), ((0,), (0,)))
    bnt = (((2,), (2,)), ((0,), (0,)))

    def blocks(t):
        return t.reshape(n_blk, blk, t.shape[-1])

    for head in range(n_heads):
        kc = slice(head * GLA_HK, (head + 1) * GLA_HK)
        vc = slice(head * GLA_HV, (head + 1) * GLA_HV)
        z = jnp.dot(gk_ref[...], w2_ref[:, kc], preferred_element_type=F32) + bgk_ref[:, kc]
        log_a = blocks((jnp.minimum(z, 0.0) - jnp.log(1.0 + jnp.exp(-jnp.abs(z)))) * (1.0 / GLA_GATE_NORMALIZER))
        hi = log_a.astype(BF16)
        lo = (log_a - hi.astype(F32)).astype(BF16)
        b = (lax.dot_general(tril, hi, bnn, preferred_element_type=F32)
             + lax.dot_general(tril, lo, bnn, preferred_element_type=F32))
        b_last = b[:, blk - 1:blk, :]
        q = blocks(q_ref[:, kc].astype(F32)) * (GLA_HK ** -0.5)
        k = blocks(k_ref[:, kc].astype(F32))
        v = blocks(v_ref[:, vc])
        q_t = (q * jnp.exp(b)).astype(BF16)
        k_t = (k * jnp.exp(-b)).astype(BF16)
        k_dec = (k * jnp.exp(b_last - b)).astype(BF16)
        scores = lax.dot_general(q_t, k_t, bnt, preferred_element_type=F32)
        scores = jnp.where(causal[None], scores, 0.0).astype(BF16)
        o_intra = lax.dot_general(scores, v, bnn, preferred_element_type=F32)
        decay = jnp.exp(b_last)

        update = [lax.dot_general(k_dec[c], v[c], tn, preferred_element_type=F32) for c in range(n_blk)]
        decay_col = [jnp.sum(jnp.where(eye, jnp.broadcast_to(decay[c], (GLA_HK, GLA_HK)), 0.0),
                             axis=1, keepdims=True) for c in range(n_blk)]
        state = s_ref[0, head]
        starts = []
        for c in range(n_blk):
            starts.append(state.astype(BF16))
            state = decay_col[c] * state + update[c]
        s_ref[0, head] = state
        o = o_intra + jnp.stack([jnp.dot(q_t[c], starts[c], preferred_element_type=F32) for c in range(n_blk)])
        on = o * lax.rsqrt(jnp.mean(o * o, axis=-1, keepdims=True) + RMS_EPS) * gn_ref[...]
        gate = _silu(g_ref[:, vc].astype(F32)).reshape(o.shape)
        o_ref[:, vc] = (on * gate).reshape(o_ref.shape[0], GLA_HV).astype(o_ref.dtype)


def _gla_steps(batch, seq, n_heads):
    return batch * (GLA_HEADS // n_heads) * (seq // min(ROW_TILE, seq))


def _gla(proj, gk_low, w_gk2, b_gk, g_norm, s0, batch, seq, n_heads, jobs=()):
    blk = min(CHUNK, seq)
    rows = min(ROW_TILE, seq)
    n_steps = seq // rows
    n_groups = GLA_HEADS // n_heads
    m = batch * seq
    wk, wv = n_heads * GLA_HK, n_heads * GLA_HV
    k_off = GLA_DK // wk
    v_off = 2 * GLA_DK // wv
    g_off = (2 * GLA_DK + GLA_DV) // wv
    row = lambda b, h, c: b * n_steps + c
    state_spec = pl.BlockSpec((1, n_heads, GLA_HK, GLA_HV), lambda b, h, c: (b, h, 0, 0))
    grid_steps = _gla_steps(batch, seq, n_heads)
    job_in, job_out, job_shape = _job_specs(jobs, lambda b, h, c: (b * n_groups + h) * n_steps + c, grid_steps)
    return pl.pallas_call(
        functools.partial(_gla_kernel, blk=blk, n_blk=rows // blk, n_heads=n_heads, jobs=jobs, n_steps=grid_steps),
        grid=(batch, n_groups, n_steps),
        in_specs=[
            pl.BlockSpec((rows, wk), lambda b, h, c: (row(b, h, c), h)),
            pl.BlockSpec((rows, wk), lambda b, h, c: (row(b, h, c), k_off + h)),
            pl.BlockSpec((rows, wv), lambda b, h, c: (row(b, h, c), v_off + h)),
            pl.BlockSpec((rows, wv), lambda b, h, c: (row(b, h, c), g_off + h)),
            pl.BlockSpec((rows, GLA_GATE_RANK), lambda b, h, c: (row(b, h, c), 0)),
            pl.BlockSpec((GLA_GATE_RANK, wk), lambda b, h, c: (0, h)),
            pl.BlockSpec((1, wk), lambda b, h, c: (0, h)),
            pl.BlockSpec((1, GLA_HV), lambda b, h, c: (0, 0)),
            state_spec,
            *job_in,
        ],
        out_specs=[
            pl.BlockSpec((rows, wv), lambda b, h, c: (row(b, h, c), h)),
            state_spec,
            *job_out,
        ],
        out_shape=[
            jax.ShapeDtypeStruct((m, GLA_DV), BF16),
            jax.ShapeDtypeStruct(s0.shape, F32),
            *job_shape,
        ],
        compiler_params=_params("arbitrary", "arbitrary", "arbitrary"),
        name="gla",
    )(proj, proj, proj, proj, gk_low, w_gk2, b_gk, g_norm, s0, *(job.src for job in jobs))


def _band_attn_kernel(q_ref, kp_ref, kc_ref, vp_ref, vc_ref, rel_ref, *rest, jobs, n_steps):
    n_jobs = len(jobs)
    o_ref, bias_ref = rest[n_jobs], rest[-1]
    i = pl.program_id(1)
    _side_casts(pl.program_id(0) * pl.num_programs(1) + i, n_steps, jobs, rest[:n_jobs], rest[n_jobs + 1:-1])
    nt = (((1,), (1,)), ((), ()))
    n_keys = BAND_PAST + ATT_SUB
    n_sub = BAND_PAST // ATT_SUB

    @pl.when(i <= 1)
    def _():
        col = lax.broadcasted_iota(jnp.int32, (ATT_SUB, n_keys), 1)
        qc = lax.broadcasted_iota(jnp.int32, (ATT_SUB, n_keys), 0) // CHUNK
        kc = col // CHUNK
        in_band = (kc >= qc) & (kc <= qc + BAND_PAST // CHUNK)
        for h in range(ATT_HEADS_PER_STEP):
            rel = jnp.broadcast_to(rel_ref[h], (ATT_SUB, REL_LEN))
            toeplitz = pltpu.roll(rel, 0, 1, stride=1, stride_axis=0)[:, ATT_SUB:] * LOG2E
            for s in range(n_sub):
                key_pos = (i - 1) * BAND_PAST + s * ATT_SUB + col
                bias_ref[h, s] = jnp.where(in_band & (key_pos >= 0), toeplitz, NEG)

    units = [(h, s) for h in range(ATT_HEADS_PER_STEP) for s in range(n_sub)]

    def scores(h, s):
        lo, hi = s * ATT_SUB, (s + 1) * ATT_SUB
        cols = slice(h * ATT_HD, (h + 1) * ATT_HD)
        k = jnp.concatenate([kp_ref[lo:, cols], kc_ref[:hi, cols]], axis=0)
        return lax.dot_general(q_ref[lo:hi, cols], k, nt, preferred_element_type=F32) + bias_ref[h, s]

    def finish(h, s, sc):
        lo, hi = s * ATT_SUB, (s + 1) * ATT_SUB
        cols = slice(h * ATT_HD, (h + 1) * ATT_HD)
        v = jnp.concatenate([vp_ref[lo:, cols], vc_ref[:hi, cols]], axis=0)
        p = jnp.exp2(sc - jnp.max(sc, axis=-1, keepdims=True))
        denom = jnp.sum(p, axis=-1, keepdims=True)
        o = jnp.dot(p.astype(BF16), v, preferred_element_type=F32) / denom
        o_ref[lo:hi, cols] = o.astype(o_ref.dtype)

    sc = scores(*units[0])
    for n, unit in enumerate(units):
        sc_next = scores(*units[n + 1]) if n + 1 < len(units) else None
        finish(*unit, sc)
        sc = sc_next


def _band_rel(table):
    assert BAND_PAST + ATT_SUB - REL_LEN // 2 == MAX_REL and REL_LEN // 2 <= 2 * MAX_REL
    far = jnp.broadcast_to(table[:, 2 * MAX_REL:], (ATT_HEADS, REL_LEN // 2))
    near = table[:, 2 * MAX_REL - REL_LEN // 2 + 1:][:, ::-1]
    return jnp.concatenate([far, near], axis=1).astype(F32)[:, None, :]


def _band_attn(q, kv, table, jobs=()):
    t = q.shape[0]
    n_keys = BAND_PAST + ATT_SUB
    prev = lambda i: jnp.maximum(i - 1, 0)
    n_groups = ATT_HEADS // ATT_HEADS_PER_STEP
    blk = (BAND_PAST, ATT_HEADS_PER_STEP * ATT_HD)
    n_blocks = t // BAND_PAST
    job_in, job_out, job_shape = _job_specs(jobs, lambda h, i: h * n_blocks + i, n_groups * n_blocks)
    return pl.pallas_call(
        functools.partial(_band_attn_kernel, jobs=jobs, n_steps=n_groups * n_blocks),
        grid=(n_groups, n_blocks),
        in_specs=[
            pl.BlockSpec(blk, lambda h, i: (i, h)),
            pl.BlockSpec(blk, lambda h, i: (prev(i), h)),
            pl.BlockSpec(blk, lambda h, i: (i, h)),
            pl.BlockSpec(blk, lambda h, i: (prev(i), n_groups + h)),
            pl.BlockSpec(blk, lambda h, i: (i, n_groups + h)),
            pl.BlockSpec((ATT_HEADS_PER_STEP, 1, REL_LEN), lambda h, i: (h, 0, 0)),
            *job_in,
        ],
        out_specs=[pl.BlockSpec(blk, lambda h, i: (i, h)), *job_out],
        out_shape=[jax.ShapeDtypeStruct((t, D_MODEL), BF16), *job_shape],
        scratch_shapes=[pltpu.VMEM((ATT_HEADS_PER_STEP, BAND_PAST // ATT_SUB, ATT_SUB, n_keys), F32)],
        compiler_params=_params("arbitrary", "arbitrary"),
        name="band_attn",
    )(q, kv, kv, kv, kv, _band_rel(table), *(job.src for job in jobs))


def _step_attn_kernel(q_ref, kvn_ref, ck_ref, cv_ref, bp_ref, bn_ref, o_ref, bias_ref):
    nt = (((1,), (1,)), ((), ()))
    n_past = ck_ref.shape[1]
    n_keys = n_past * ATT_HEADS

    @pl.when(pl.program_id(0) == 0)
    def _():
        frame = lax.broadcasted_iota(jnp.int32, (n_past, n_keys), 0)
        key = lax.broadcasted_iota(jnp.int32, (n_past, n_keys), 1)
        spread = jnp.where(key // ATT_HEADS == frame, 1.0, 0.0).astype(BF16)
        bp = bp_ref[...]
        hi = bp.astype(BF16)
        lo = (bp - hi.astype(F32)).astype(BF16)
        wide = (jnp.dot(hi, spread, preferred_element_type=F32)
                + jnp.dot(lo, spread, preferred_element_type=F32))
        q_head = lax.broadcasted_iota(jnp.int32, wide.shape, 0) // q_ref.shape[0]
        k_head = lax.broadcasted_iota(jnp.int32, wide.shape, 1) % ATT_HEADS
        bias_ref[...] = jnp.where(q_head == k_head, wide, NEG)

    heads = lambda ref, off: jnp.concatenate(
        [ref[:, off + h * ATT_HD:off + (h + 1) * ATT_HD] for h in range(ATT_HEADS)], axis=0)
    q = heads(q_ref, 0)
    k_new = heads(kvn_ref, 0).astype(BF16)
    v_new = heads(kvn_ref, D_MODEL).astype(BF16)
    k_past = ck_ref[0].reshape(n_keys, ATT_HD).astype(BF16)
    v_past = cv_ref[0].reshape(n_keys, ATT_HD).astype(BF16)
    sp = lax.dot_general(q, k_past, nt, preferred_element_type=F32) + bias_ref[...]
    sn = lax.dot_general(q, k_new, nt, preferred_element_type=F32) + bn_ref[...]
    mx = jnp.maximum(jnp.max(sp, axis=-1, keepdims=True), jnp.max(sn, axis=-1, keepdims=True))
    pp = jnp.exp2(sp - mx)
    pn = jnp.exp2(sn - mx)
    denom = jnp.sum(pp, axis=-1, keepdims=True) + jnp.sum(pn, axis=-1, keepdims=True)
    o = (jnp.dot(pp.astype(BF16), v_past, preferred_element_type=F32)
         + jnp.dot(pn.astype(BF16), v_new, preferred_element_type=F32)) / denom
    seq = q_ref.shape[0]
    for h in range(ATT_HEADS):
        o_ref[:, h * ATT_HD:(h + 1) * ATT_HD] = o[h * seq:(h + 1) * seq, :].astype(o_ref.dtype)


def _step_attn(q, kv_new, cache_k, cache_v, table, batch, seq):
    n_past = cache_k.shape[1]
    dist = n_past + jnp.arange(seq)[:, None] - jnp.arange(n_past + seq)[None, :]
    bias = table[:, jnp.clip(dist, -MAX_REL, MAX_REL) + MAX_REL].astype(F32) * LOG2E
    rows = ATT_HEADS * seq
    bias_past = bias[:, :, :n_past].reshape(rows, n_past)
    same_head = jnp.eye(ATT_HEADS, dtype=bool)[:, None, :, None]
    bias_new = jnp.where(same_head, bias[:, :, None, n_past:], NEG).reshape(rows, rows)
    return pl.pallas_call(
        _step_attn_kernel,
        grid=(batch,),
        in_specs=[
            pl.BlockSpec((seq, D_MODEL), lambda b: (b, 0)),
            pl.BlockSpec((seq, 2 * D_MODEL), lambda b: (b, 0)),
            pl.BlockSpec((1, n_past, ATT_HEADS, ATT_HD), lambda b: (b, 0, 0, 0)),
            pl.BlockSpec((1, n_past, ATT_HEADS, ATT_HD), lambda b: (b, 0, 0, 0)),
            pl.BlockSpec((rows, n_past), lambda b: (0, 0)),
            pl.BlockSpec((rows, rows), lambda b: (0, 0)),
        ],
        out_specs=pl.BlockSpec((seq, D_MODEL), lambda b: (b, 0)),
        out_shape=jax.ShapeDtypeStruct((batch * seq, D_MODEL), BF16),
        scratch_shapes=[pltpu.VMEM((rows, n_past * ATT_HEADS), F32)],
        compiler_params=_params("arbitrary"),
        name="step_attn",
    )(q, kv_new, cache_k, cache_v, bias_past, bias_new)


def _trunk(x, ada, ada_kv, row_block, gla_s0, past, wts):
    batch, seq, _ = x.shape
    m = batch * seq
    tm = min(ROW_TILE, m)
    x = x.reshape(m, D_MODEL)
    assert batch <= MOD_ROWS and (batch == 1 or m == tm)
    per_seq = None if batch == 1 else seq

    def mod(l, sub, which):
        return _Mod(ada, l, row_block, 3 * sub + which, per_seq)

    def ffn(x, l, sub, idx):
        return _ffn(x, mod(l, sub, 0), mod(l, sub, 1), mod(l, sub, 2), *wts["ffn"][l, idx],
                    wts["ln_g"][l, sub][None], wts["ln_b"][l, sub][None], min(FFN_ROW_TILE, m))

    def side_jobs(ffn_keys, names, n_steps):
        ffn_keys = [key for key in ffn_keys if key not in wts["ffn"]]
        names = [name for name in names if name not in wts]
        jobs = tuple(job for key in ffn_keys for job in _ffn_cast_jobs(*wts["ffn_f32"], *key, n_steps))
        jobs += tuple(_CastJob(wts["f32"][name], 0, 0, D_MODEL // n_steps, (wts["f32"][name].shape[-1],))
                      for name in names)

        def keep(cast):
            for n, key in enumerate(ffn_keys):
                wts["ffn"][key] = tuple(cast[3 * n:3 * n + 3])
            for n, name in enumerate(names):
                wts[name] = cast[3 * len(ffn_keys) + n]
        return jobs, keep

    x = ffn(x, 0, 0, 0)
    proj, gk_low = _modmm(x, mod(0, 1, 0), mod(0, 1, 1), wts["a_in_t"], tm, BF16,
                          widths=(GLA_MAIN, GLA_GATE_RANK), w_is_t=True)
    gla_heads = GLA_HEADS_PER_STEP if batch == 1 else GLA_HEADS
    jobs, keep = side_jobs(((0, 1), (1, 0)), ("a_out",), _gla_steps(batch, seq, gla_heads))
    o, gla_state, *cast = _gla(proj, gk_low, wts["a_gk2"], wts["b_a_gk"], wts["g_a_norm"], gla_s0, batch, seq,
                               gla_heads, jobs)
    keep(cast)
    jobs, keep = side_jobs((), ("kv", "b_q", "b_out"), m // tm)
    x, *cast = _proj_ln(o, wts["a_out"], x, mod(0, 1, 2), wts["ln_g"][0, 1][None], wts["ln_b"][0, 1][None], tm, jobs)
    keep(cast)
    x = ffn(x, 0, 2, 1)

    kv_shift = _Mod(ada_kv, 0, row_block, 0, per_seq)
    kv_scale = _Mod(ada_kv, 0, row_block, 1, per_seq)
    if past is None:
        assert tm == min(BAND_PAST, seq)
        kv, kv_out = _modmm(x, kv_shift, kv_scale, wts["kv"], tm, BF16, tail_f32=True)
    else:
        kv_out = _modmm(x, kv_shift, kv_scale, wts["kv"], tm, F32)

    x = ffn(x, 1, 0, 0)
    q = _modmm(x, mod(1, 1, 0), mod(1, 1, 1), wts["b_q"], tm, BF16,
               out_scale=ATT_HD ** -0.5 * LOG2E)
    if past is None:
        jobs, keep = side_jobs(((1, 1),), (), (ATT_HEADS // ATT_HEADS_PER_STEP) * (seq // BAND_PAST))
        o, *cast = _band_attn(q, kv, wts["rel_bias"], jobs)
        keep(cast)
    else:
        o = _step_attn(q, kv_out, past[0], past[1], wts["rel_bias"], batch, seq)
    x, = _proj_ln(o, wts["b_out"], x, mod(1, 1, 2), wts["ln_g"][1, 1][None], wts["ln_b"][1, 1][None], tm)
    x = ffn(x, 1, 2, 1)

    n_rows = kv_out.shape[0] // batch
    k_out = kv_out[:, :D_MODEL].reshape(batch, n_rows, ATT_HEADS, ATT_HD)
    v_out = kv_out[:, D_MODEL:].reshape(batch, n_rows, ATT_HEADS, ATT_HD)
    return x.reshape(batch, seq, D_MODEL), gla_state[None], k_out, v_out


def kernel(x_prompt, x_sample, state_gla, cache_band_k, cache_band_v, c_prompt, c_sample, w_ada, b_ada, ln_g, ln_b, w_ffn_up, w_ffn_down, w_a_in, w_a_gk2, b_a_gk, g_a_norm, w_a_out, w_ada_kv, b_ada_kv, w_kv, w_b_q, rel_bias, w_b_out):
    assert DEPTH == 2 and w_a_in.shape[0] == 1 and w_b_q.shape[0] == 1
    n_prompt, n_sample = x_prompt.shape[0], x_sample.shape[0]
    assert n_prompt == 1 and n_sample <= MOD_ROWS

    lead = lambda w: w.reshape((1, 1) + w.shape[-2:])
    wts = {
        "ffn_f32": (w_ffn_up, w_ffn_down),
        "ffn": {(0, 0): (*_cast_split(w_ffn_up, (D_FF, D_FF), CAST_ROWS),
                         *_cast_split(w_ffn_down, (D_MODEL,), D_FF // 8))},
        "a_in_t": _cast_split(lead(jnp.swapaxes(w_a_in, 1, 2)), (D_MODEL,), A_IN_CAST_ROWS)[0],
        "a_gk2": w_a_gk2[0].astype(BF16),
        "b_a_gk": b_a_gk[0][None],
        "g_a_norm": g_a_norm[0][None],
        "f32": {"a_out": lead(w_a_out), "kv": lead(w_kv), "b_q": lead(w_b_q), "b_out": lead(w_b_out)},
        "rel_bias": rel_bias[0],
        "ln_g": ln_g,
        "ln_b": ln_b,
    }

    c_all = jnp.concatenate([jnp.pad(c, ((0, MOD_ROWS - c.shape[0]), (0, 0))) for c in (c_prompt, c_sample)], axis=0)
    ada = _ada(c_all, w_ada, b_ada)
    ada_kv = _ada(c_all, w_ada_kv[None], b_ada_kv[None])

    gla_zero = jnp.zeros((n_prompt, GLA_HEADS, GLA_HK, GLA_HV), state_gla.dtype)
    y_p, s_p, k_p, v_p = _trunk(x_prompt, ada, ada_kv, 0, gla_zero, None, wts)
    past = (cache_band_k, cache_band_v)
    y_s, s_s, k_s, v_s = _trunk(x_sample, ada, ada_kv, 1, state_gla[0], past, wts)
    return (y_p, y_s, s_p, s_s, k_p, v_p, k_s, v_s)
```

```python
import functools
from typing import NamedTuple

import jax
import jax.numpy as jnp
from jax import lax
from jax.experimental import pallas as pl
from jax.experimental.pallas import tpu as pltpu

F32 = jnp.float32
BF16 = jnp.bfloat16

D_MODEL = 2048
DEPTH = 2
CHUNK = 64
GLA_HEADS = 4
GLA_DK = D_MODEL // 2
GLA_DV = D_MODEL
GLA_HK = GLA_DK // GLA_HEADS
GLA_HV = GLA_DV // GLA_HEADS
GLA_GATE_RANK = 16
GLA_GATE_NORMALIZER = 16.0
GLA_MAIN = 2 * GLA_DK + 2 * GLA_DV
ATT_HEADS = 16
ATT_HD = D_MODEL // ATT_HEADS
BAND_PAST = 8 * CHUNK
MAX_REL = 256
D_FF = 5504
ALPHA = (2 * DEPTH) ** 0.25
LN_EPS = 1e-5
RMS_EPS = 1e-6
NEG = -1e30
LOG2E = 1.4426950408889634

LANE = 128
FF_TILE = 512
FF_LAST = D_FF - (D_FF - 1) // FF_TILE * FF_TILE
ROW_TILE = 512
FFN_ROW_TILE = 1024
A_IN_CAST_ROWS = 560
CAST_ROWS = 256
FFN_CAST_DOWN_ROWS = 128
GLA_HEADS_PER_STEP = 2
PROJ_SUB = 128
MODMM_COLS = 1024
ATT_SUB = 256
ATT_HEADS_PER_STEP = 4
REL_LEN = BAND_PAST + 2 * ATT_SUB
MOD_ROWS = 8
ADA_ROWS = 2 * MOD_ROWS
ADA_TILE = 1024
VMEM_LIMIT = 56 * 1024 * 1024
FFN_VMEM_LIMIT = 60 * 1024 * 1024


def _params(*sem):
    return pltpu.CompilerParams(dimension_semantics=sem, vmem_limit_bytes=VMEM_LIMIT)


def _layer_norm(z, g, b):
    mu = jnp.mean(z, axis=-1, keepdims=True)
    zc = z - mu
    var = jnp.mean(zc * zc, axis=-1, keepdims=True)
    return zc * lax.rsqrt(var + LN_EPS) * g + b


def _silu(a):
    half = 0.5 * a
    return half + half * jnp.tanh(half)


def _ada_kernel(c_ref, w_ref, b_ref, o_ref):
    o_ref[...] = jnp.dot(_silu(c_ref[...]), w_ref[...], preferred_element_type=F32) + b_ref[...]


def _ada(c_all, w, b):
    n_l, _, n = w.shape
    return pl.pallas_call(
        _ada_kernel,
        grid=(n_l, n // ADA_TILE),
        in_specs=[
            pl.BlockSpec((ADA_ROWS, D_MODEL), lambda l, j: (0, 0)),
            pl.BlockSpec((None, D_MODEL, ADA_TILE), lambda l, j: (l, 0, j)),
            pl.BlockSpec((None, 1, ADA_TILE), lambda l, j: (l, 0, j)),
        ],
        out_specs=pl.BlockSpec((None, ADA_ROWS, ADA_TILE), lambda l, j: (l, 0, j)),
        out_shape=jax.ShapeDtypeStruct((n_l, ADA_ROWS, n), F32),
        compiler_params=_params("arbitrary", "arbitrary"),
        name="ada",
    )(c_all, w, b.reshape(n_l, 1, n))


class _Mod(NamedTuple):
    table: jax.Array
    layer: int
    row_block: int
    col: int
    seq: int | None

    @property
    def spec(self):
        return pl.BlockSpec((None, MOD_ROWS, D_MODEL), lambda *_: (self.layer, self.row_block, self.col))


def _mod_rows(ref, n_rows, seq):
    if seq is None:
        return ref[0:1, :]
    n_seq = n_rows // seq
    return jnp.broadcast_to(ref[0:n_seq, :][:, None, :], (n_seq, seq, D_MODEL)).reshape(n_rows, D_MODEL)


def _ffn_kernel(*refs, seqs):
    n = len(seqs)
    wa_ref, wu_ref, wd_ref, g_ref, b_ref = refs[4 * n:4 * n + 5]
    o_refs = refs[4 * n + 5:5 * n + 5]
    h_refs = refs[5 * n + 5:]
    i, j = pl.program_id(0), pl.program_id(1)
    last = pl.num_programs(1) - 1

    def stream(x_ref, sh_ref, sc_ref, gt_ref, o_ref, h_ref, seq, active):
        mod = lambda ref: _mod_rows(ref, x_ref.shape[0], seq)

        @pl.when(active & (j == 0))
        def _():
            h_ref[...] = (x_ref[...] * (1.0 + mod(sc_ref)) + mod(sh_ref)).astype(BF16)
            o_ref[...] = jnp.zeros_like(o_ref)

        def chunk(cols):
            h = h_ref[...]
            a = jnp.dot(h, wa_ref[:, :cols], preferred_element_type=F32)
            u = jnp.dot(h, wu_ref[:, :cols], preferred_element_type=F32)
            o_ref[...] += jnp.dot((_silu(a) * u).astype(BF16), wd_ref[:cols, :], preferred_element_type=F32)

        @pl.when(active & (j < last))
        def _():
            chunk(FF_TILE)

        @pl.when(active & (j == last))
        def _():
            chunk(FF_LAST)
            z = ALPHA * x_ref[...] + (0.5 * (1.0 + mod(gt_ref))) * o_ref[...]
            o_ref[...] = _layer_norm(z, g_ref[...], b_ref[...])

    for s in range(n):
        active = True if s == 0 else (i == pl.num_programs(0) - 1)
        stream(*refs[4 * s:4 * s + 4], o_refs[s], h_refs[s], seqs[s], active)


def _ffn(streams, w_a, w_u, w_down, ln_g, ln_b, tm):
    m = streams[0][0].shape[0]
    n_ff = pl.cdiv(D_FF, FF_TILE)
    vec = pl.BlockSpec((1, D_MODEL), lambda i, j: (0, 0))
    up = pl.BlockSpec((D_MODEL, FF_TILE), lambda i, j: (0, j))
    x_specs = [pl.BlockSpec((tm, D_MODEL), lambda i, j: (i, 0))]
    x_specs += [pl.BlockSpec(x.shape, lambda i, j: (0, 0)) for x, *_ in streams[1:]]
    in_specs, args = [], []
    for x_spec, (x, shift, scale, gate) in zip(x_specs, streams):
        in_specs += [x_spec, shift.spec, scale.spec, gate.spec]
        args += [x, shift.table, scale.table, gate.table]
    return pl.pallas_call(
        functools.partial(_ffn_kernel, seqs=tuple(shift.seq for _, shift, *_ in streams)),
        grid=(m // tm, n_ff),
        in_specs=[
            *in_specs,
            up, up,
            pl.BlockSpec((FF_TILE, D_MODEL), lambda i, j: (j, 0)),
            vec, vec,
        ],
        out_specs=x_specs,
        out_shape=[jax.ShapeDtypeStruct(x.shape, F32) for x, *_ in streams],
        scratch_shapes=[pltpu.VMEM(spec.block_shape, BF16) for spec in x_specs],
        compiler_params=pltpu.CompilerParams(dimension_semantics=("arbitrary", "arbitrary"),
                                             vmem_limit_bytes=FFN_VMEM_LIMIT),
        name="ffn",
    )(*args, w_a, w_u, w_down, ln_g, ln_b)


def _cast_kernel(x_ref, *o_refs):
    off = 0
    for o_ref in o_refs:
        n = o_ref.shape[-1]
        o_ref[...] = x_ref[:, off:off + n].astype(o_ref.dtype)
        off += n


class _CastJob(NamedTuple):
    src: jax.Array
    layer: int
    idx: int
    rows: int
    widths: tuple

    @property
    def n_blocks(self):
        return self.src.shape[2] // self.rows

    def specs(self, step_of):
        r, n = self.src.shape[2:]
        assert r % self.rows == 0 and sum(self.widths) <= n and all(wd % LANE == 0 for wd in self.widths)
        blk = lambda *ids: jnp.minimum(step_of(*ids), self.n_blocks - 1)
        in_spec = pl.BlockSpec((None, None, self.rows, n), lambda *ids: (self.layer, self.idx, blk(*ids), 0))
        out_specs = [pl.BlockSpec((self.rows, wd), lambda *ids: (blk(*ids), 0)) for wd in self.widths]
        out_shape = [jax.ShapeDtypeStruct((r, wd), BF16) for wd in self.widths]
        return in_spec, out_specs, out_shape


def _side_casts(step, n_steps, jobs, in_refs, out_refs):
    pos = 0
    for job, x_ref in zip(jobs, in_refs):
        outs = out_refs[pos:pos + len(job.widths)]
        pos += len(job.widths)
        if job.n_blocks == n_steps:
            _cast_kernel(x_ref, *outs)
        else:
            pl.when(step < job.n_blocks)(functools.partial(_cast_kernel, x_ref, *outs))


def _job_specs(jobs, step_of, n_steps):
    in_specs, out_specs, out_shape = [], [], []
    for job in jobs:
        assert job.n_blocks <= n_steps
        i, o, s = job.specs(step_of)
        in_specs.append(i)
        out_specs += o
        out_shape += s
    return in_specs, out_specs, out_shape


def _cast_split(w, widths, rows, layer=0, idx=0):
    job = _CastJob(w, layer, idx, rows, tuple(widths))
    in_spec, out_specs, out_shape = job.specs(lambda i: i)
    return pl.pallas_call(
        _cast_kernel,
        grid=(job.n_blocks,),
        in_specs=[in_spec],
        out_specs=out_specs,
        out_shape=out_shape,
        compiler_params=_params("parallel"),
        name="cast",
    )(w)


def _ffn_cast_jobs(w_ffn_up, w_ffn_down, layer, idx, n_steps):
    return (_CastJob(w_ffn_up, layer, idx, D_MODEL // n_steps, (D_FF, D_FF)),
            _CastJob(w_ffn_down, layer, idx, FFN_CAST_DOWN_ROWS, (D_MODEL,)))


def _modmm_kernel(x_ref, sh_ref, sc_ref, w_ref, o_ref, *rest, seq, w_is_t, out_scale, tail_f32):
    h_ref = rest[-1]
    mod = lambda ref: _mod_rows(ref, x_ref.shape[0], seq)
    h_ref[...] = (x_ref[...] * (1.0 + mod(sc_ref)) + mod(sh_ref)).astype(BF16)
    h = h_ref[...]
    nt = (((1,), (1,)), ((), ()))

    def project(lo, hi):
        if w_is_t:
            y = lax.dot_general(h, w_ref[lo:hi, :], nt, preferred_element_type=F32)
        else:
            y = jnp.dot(h, w_ref[:, lo:hi], preferred_element_type=F32)
        return y if out_scale is None else y * out_scale

    n = o_ref.shape[1]
    if tail_f32:
        tail_ref, = rest[:-1]
        for lo in range(0, n, MODMM_COLS):
            y = project(lo, lo + MODMM_COLS)
            o_ref[:, lo:lo + MODMM_COLS] = y.astype(o_ref.dtype)
            tail_ref[:, lo:lo + MODMM_COLS] = y
        return
    for lo in range(0, n, MODMM_COLS):
        o_ref[:, lo:lo + MODMM_COLS] = project(lo, lo + MODMM_COLS).astype(o_ref.dtype)
    for extra_ref in rest[:-1]:
        extra_ref[...] = project(n, n + extra_ref.shape[1]).astype(extra_ref.dtype)
        n += extra_ref.shape[1]


def _modmm(x, shift, scale, w, tm, out_dtype, widths=None, w_is_t=False, out_scale=None, tail_f32=False):
    m = x.shape[0]
    widths = widths or (w.shape[0] if w_is_t else w.shape[1],)
    assert widths[0] % MODMM_COLS == 0 and not (tail_f32 and len(widths) > 1)
    out_specs = [pl.BlockSpec((tm, wd), lambda i: (i, 0)) for wd in widths]
    out_shape = [jax.ShapeDtypeStruct((m, wd), out_dtype) for wd in widths]
    if tail_f32:
        out_specs.append(pl.BlockSpec((tm, widths[0]), lambda i: (0, 0)))
        out_shape.append(jax.ShapeDtypeStruct((tm, widths[0]), F32))
    out = pl.pallas_call(
        functools.partial(_modmm_kernel, seq=shift.seq, w_is_t=w_is_t, out_scale=out_scale, tail_f32=tail_f32),
        grid=(m // tm,),
        in_specs=[
            pl.BlockSpec((tm, D_MODEL), lambda i: (i, 0)),
            shift.spec, scale.spec,
            pl.BlockSpec(w.shape, lambda i: (0, 0), pipeline_mode=pl.Buffered(1)),
        ],
        out_specs=out_specs,
        out_shape=out_shape,
        scratch_shapes=[pltpu.VMEM((tm, D_MODEL), BF16)],
        compiler_params=_params("arbitrary"),
        name="modmm",
    )(x, shift.table, scale.table, w)
    return out if len(out) > 1 else out[0]


def _proj_ln_kernel(a_ref, w_ref, x_ref, gt_ref, g_ref, b_ref, *rest, seq, jobs, n_steps):
    n_jobs = len(jobs)
    o_ref = rest[n_jobs]
    _side_casts(pl.program_id(0), n_steps, jobs, rest[:n_jobs], rest[n_jobs + 1:])
    sub = min(PROJ_SUB, a_ref.shape[0])
    n_sub = a_ref.shape[0] // sub
    dot = lambda r: jnp.dot(a_ref[r * sub:(r + 1) * sub, :], w_ref[...], preferred_element_type=F32)
    gate = _mod_rows(gt_ref, a_ref.shape[0], seq)
    y = dot(0)
    for r in range(n_sub):
        y_next = dot(r + 1) if r + 1 < n_sub else None
        rows = slice(r * sub, (r + 1) * sub)
        z = ALPHA * x_ref[rows, :] + (1.0 + gate[rows if gate.shape[0] > 1 else slice(None), :]) * y
        o_ref[rows, :] = _layer_norm(z, g_ref[...], b_ref[...])
        y = y_next


def _proj_ln(a, w, x, gate, ln_g, ln_b, tm, jobs=()):
    m = x.shape[0]
    vec = pl.BlockSpec((1, D_MODEL), lambda i: (0, 0))
    job_in, job_out, job_shape = _job_specs(jobs, lambda i: i, m // tm)
    return pl.pallas_call(
        functools.partial(_proj_ln_kernel, seq=gate.seq, jobs=jobs, n_steps=m // tm),
        grid=(m // tm,),
        in_specs=[
            pl.BlockSpec((tm, D_MODEL), lambda i: (i, 0)),
            pl.BlockSpec((D_MODEL, D_MODEL), lambda i: (0, 0)),
            pl.BlockSpec((tm, D_MODEL), lambda i: (i, 0)),
            gate.spec,
            vec, vec,
            *job_in,
        ],
        out_specs=[pl.BlockSpec((tm, D_MODEL), lambda i: (i, 0)), *job_out],
        out_shape=[jax.ShapeDtypeStruct((m, D_MODEL), F32), *job_shape],
        compiler_params=_params("arbitrary"),
        name="proj_ln",
    )(a, w, x, gate.table, ln_g, ln_b, *(job.src for job in jobs))


def _gla_kernel(q_ref, k_ref, v_ref, g_ref, gk_ref, w2_ref, bgk_ref, gn_ref, s0_ref, *rest,
                blk, n_blk, n_heads, jobs, n_steps):
    n_jobs = len(jobs)
    o_ref, s_ref = rest[n_jobs:n_jobs + 2]
    step = (pl.program_id(0) * pl.num_programs(1) + pl.program_id(1)) * pl.num_programs(2) + pl.program_id(2)
    _side_casts(step, n_steps, jobs, rest[:n_jobs], rest[n_jobs + 2:])

    @pl.when(pl.program_id(2) == 0)
    def _():
        s_ref[...] = s0_ref[...]

    causal = (lax.broadcasted_iota(jnp.int32, (blk, blk), 0)
              >= lax.broadcasted_iota(jnp.int32, (blk, blk), 1))
    tril = jnp.broadcast_to(jnp.where(causal, 1.0, 0.0).astype(BF16), (n_blk, blk, blk))
    eye = (lax.broadcasted_iota(jnp.int32, (GLA_HK, GLA_HK), 0)
           == lax.broadcasted_iota(jnp.int32, (GLA_HK, GLA_HK), 1))
    tn = (((0,), (0,)), ((), ()))
    bnn = (((2,), (1,)), ((0,), (0,)))
    bnt = (((2,), (2,)), ((0,), (0,)))

    def blocks(t):
        return t.reshape(n_blk, blk, t.shape[-1])

    for head in range(n_heads):
        kc = slice(head * GLA_HK, (head + 1) * GLA_HK)
        vc = slice(head * GLA_HV, (head + 1) * GLA_HV)
        z = jnp.dot(gk_ref[...], w2_ref[:, kc], preferred_element_type=F32) + bgk_ref[:, kc]
        log_a = blocks((jnp.minimum(z, 0.0) - jnp.log(1.0 + jnp.exp(-jnp.abs(z)))) * (1.0 / GLA_GATE_NORMALIZER))
        hi = log_a.astype(BF16)
        lo = (log_a - hi.astype(F32)).astype(BF16)
        b = (lax.dot_general(tril, hi, bnn, preferred_element_type=F32)
             + lax.dot_general(tril, lo, bnn, preferred_element_type=F32))
        b_last = b[:, blk - 1:blk, :]
        q = blocks(q_ref[:, kc].astype(F32)) * (GLA_HK ** -0.5)
        k = blocks(k_ref[:, kc].astype(F32))
        v = blocks(v_ref[:, vc])
        q_t = (q * jnp.exp(b)).astype(BF16)
        k_t = (k * jnp.exp(-b)).astype(BF16)
        k_dec = (k * jnp.exp(b_last - b)).astype(BF16)
        scores = lax.dot_general(q_t, k_t, bnt, preferred_element_type=F32)
        scores = jnp.where(causal[None], scores, 0.0).astype(BF16)
        o_intra = lax.dot_general(scores, v, bnn, preferred_element_type=F32)
        decay = jnp.exp(b_last)

        update = [lax.dot_general(k_dec[c], v[c], tn, preferred_element_type=F32) for c in range(n_blk)]
        decay_col = [jnp.sum(jnp.where(eye, jnp.broadcast_to(decay[c], (GLA_HK, GLA_HK)), 0.0),
                             axis=1, keepdims=True) for c in range(n_blk)]
        state = s_ref[0, head]
        starts = []
        for c in range(n_blk):
            starts.append(state.astype(BF16))
            state = decay_col[c] * state + update[c]
        s_ref[0, head] = state
        o = o_intra + jnp.stack([jnp.dot(q_t[c], starts[c], preferred_element_type=F32) for c in range(n_blk)])
        on = o * lax.rsqrt(jnp.mean(o * o, axis=-1, keepdims=True) + RMS_EPS) * gn_ref[...]
        gate = _silu(g_ref[:, vc].astype(F32)).reshape(o.shape)
        o_ref[:, vc] = (on * gate).reshape(o_ref.shape[0], GLA_HV).astype(o_ref.dtype)


def _gla_steps(batch, seq, n_heads):
    return batch * (GLA_HEADS // n_heads) * (seq // min(ROW_TILE, seq))


def _gla(proj, gk_low, w_gk2, b_gk, g_norm, s0, batch, seq, n_heads, jobs=()):
    blk = min(CHUNK, seq)
    rows = min(ROW_TILE, seq)
    n_steps = seq // rows
    n_groups = GLA_HEADS // n_heads
    m = batch * seq
    wk, wv = n_heads * GLA_HK, n_heads * GLA_HV
    k_off = GLA_DK // wk
    v_off = 2 * GLA_DK // wv
    g_off = (2 * GLA_DK + GLA_DV) // wv
    row = lambda b, h, c: b * n_steps + c
    state_spec = pl.BlockSpec((1, n_heads, GLA_HK, GLA_HV), lambda b, h, c: (b, h, 0, 0))
    grid_steps = _gla_steps(batch, seq, n_heads)
    job_in, job_out, job_shape = _job_specs(jobs, lambda b, h, c: (b * n_groups + h) * n_steps + c, grid_steps)
    return pl.pallas_call(
        functools.partial(_gla_kernel, blk=blk, n_blk=rows // blk, n_heads=n_heads, jobs=jobs, n_steps=grid_steps),
        grid=(batch, n_groups, n_steps),
        in_specs=[
            pl.BlockSpec((rows, wk), lambda b, h, c: (row(b, h, c), h)),
            pl.BlockSpec((rows, wk), lambda b, h, c: (row(b, h, c), k_off + h)),
            pl.BlockSpec((rows, wv), lambda b, h, c: (row(b, h, c), v_off + h)),
            pl.BlockSpec((rows, wv), lambda b, h, c: (row(b, h, c), g_off + h)),
            pl.BlockSpec((rows, GLA_GATE_RANK), lambda b, h, c: (row(b, h, c), 0)),
            pl.BlockSpec((GLA_GATE_RANK, wk), lambda b, h, c: (0, h)),
            pl.BlockSpec((1, wk), lambda b, h, c: (0, h)),
            pl.BlockSpec((1, GLA_HV), lambda b, h, c: (0, 0)),
            state_spec,
            *job_in,
        ],
        out_specs=[
            pl.BlockSpec((rows, wv), lambda b, h, c: (row(b, h, c), h)),
            state_spec,
            *job_out,
        ],
        out_shape=[
            jax.ShapeDtypeStruct((m, GLA_DV), BF16),
            jax.ShapeDtypeStruct(s0.shape, F32),
            *job_shape,
        ],
        compiler_params=_params("arbitrary", "arbitrary", "arbitrary"),
        name="gla",
    )(proj, proj, proj, proj, gk_low, w_gk2, b_gk, g_norm, s0, *(job.src for job in jobs))


def _band_attn_kernel(q_ref, kp_ref, kc_ref, vp_ref, vc_ref, rel_ref, *rest, jobs, n_steps):
    n_jobs = len(jobs)
    o_ref, bias_ref = rest[n_jobs], rest[-1]
    i = pl.program_id(1)
    _side_casts(pl.program_id(0) * pl.num_programs(1) + i, n_steps, jobs, rest[:n_jobs], rest[n_jobs + 1:-1])
    nt = (((1,), (1,)), ((), ()))
    n_keys = BAND_PAST + ATT_SUB
    n_sub = BAND_PAST // ATT_SUB

    @pl.when(i <= 1)
    def _():
        col = lax.broadcasted_iota(jnp.int32, (ATT_SUB, n_keys), 1)
        qc = lax.broadcasted_iota(jnp.int32, (ATT_SUB, n_keys), 0) // CHUNK
        kc = col // CHUNK
        in_band = (kc >= qc) & (kc <= qc + BAND_PAST // CHUNK)
        for h in range(ATT_HEADS_PER_STEP):
            rel = jnp.broadcast_to(rel_ref[h], (ATT_SUB, REL_LEN))
            toeplitz = pltpu.roll(rel, 0, 1, stride=1, stride_axis=0)[:, ATT_SUB:] * LOG2E
            for s in range(n_sub):
                key_pos = (i - 1) * BAND_PAST + s * ATT_SUB + col
                bias_ref[h, s] = jnp.where(in_band & (key_pos >= 0), toeplitz, NEG)

    units = [(h, s) for h in range(ATT_HEADS_PER_STEP) for s in range(n_sub)]

    def scores(h, s):
        lo, hi = s * ATT_SUB, (s + 1) * ATT_SUB
        cols = slice(h * ATT_HD, (h + 1) * ATT_HD)
        k = jnp.concatenate([kp_ref[lo:, cols], kc_ref[:hi, cols]], axis=0)
        return lax.dot_general(q_ref[lo:hi, cols], k, nt, preferred_element_type=F32) + bias_ref[h, s]

    def finish(h, s, sc):
        lo, hi = s * ATT_SUB, (s + 1) * ATT_SUB
        cols = slice(h * ATT_HD, (h + 1) * ATT_HD)
        v = jnp.concatenate([vp_ref[lo:, cols], vc_ref[:hi, cols]], axis=0)
        p = jnp.exp2(sc - jnp.max(sc, axis=-1, keepdims=True))
        denom = jnp.sum(p, axis=-1, keepdims=True)
        o = jnp.dot(p.astype(BF16), v, preferred_element_type=F32) / denom
        o_ref[lo:hi, cols] = o.astype(o_ref.dtype)

    sc = scores(*units[0])
    for n, unit in enumerate(units):
        sc_next = scores(*units[n + 1]) if n + 1 < len(units) else None
        finish(*unit, sc)
        sc = sc_next


def _band_rel(table):
    assert BAND_PAST + ATT_SUB - REL_LEN // 2 == MAX_REL and REL_LEN // 2 <= 2 * MAX_REL
    far = jnp.broadcast_to(table[:, 2 * MAX_REL:], (ATT_HEADS, REL_LEN // 2))
    near = table[:, 2 * MAX_REL - REL_LEN // 2 + 1:][:, ::-1]
    return jnp.concatenate([far, near], axis=1).astype(F32)[:, None, :]


def _band_attn(q, kv, table, jobs=()):
    t = q.shape[0]
    n_keys = BAND_PAST + ATT_SUB
    prev = lambda i: jnp.maximum(i - 1, 0)
    n_groups = ATT_HEADS // ATT_HEADS_PER_STEP
    blk = (BAND_PAST, ATT_HEADS_PER_STEP * ATT_HD)
    n_blocks = t // BAND_PAST
    job_in, job_out, job_shape = _job_specs(jobs, lambda h, i: h * n_blocks + i, n_groups * n_blocks)
    return pl.pallas_call(
        functools.partial(_band_attn_kernel, jobs=jobs, n_steps=n_groups * n_blocks),
        grid=(n_groups, n_blocks),
        in_specs=[
            pl.BlockSpec(blk, lambda h, i: (i, h)),
            pl.BlockSpec(blk, lambda h, i: (prev(i), h)),
            pl.BlockSpec(blk, lambda h, i: (i, h)),
            pl.BlockSpec(blk, lambda h, i: (prev(i), n_groups + h)),
            pl.BlockSpec(blk, lambda h, i: (i, n_groups + h)),
            pl.BlockSpec((ATT_HEADS_PER_STEP, 1, REL_LEN), lambda h, i: (h, 0, 0)),
            *job_in,
        ],
        out_specs=[pl.BlockSpec(blk, lambda h, i: (i, h)), *job_out],
        out_shape=[jax.ShapeDtypeStruct((t, D_MODEL), BF16), *job_shape],
        scratch_shapes=[pltpu.VMEM((ATT_HEADS_PER_STEP, BAND_PAST // ATT_SUB, ATT_SUB, n_keys), F32)],
        compiler_params=_params("arbitrary", "arbitrary"),
        name="band_attn",
    )(q, kv, kv, kv, kv, _band_rel(table), *(job.src for job in jobs))


def _step_attn_kernel(q_ref, kvn_ref, ck_ref, cv_ref, bp_ref, bn_ref, o_ref, bias_ref):
    nt = (((1,), (1,)), ((), ()))
    n_past = ck_ref.shape[1]
    n_keys = n_past * ATT_HEADS

    @pl.when(pl.program_id(0) == 0)
    def _():
        frame = lax.broadcasted_iota(jnp.int32, (n_past, n_keys), 0)
        key = lax.broadcasted_iota(jnp.int32, (n_past, n_keys), 1)
        spread = jnp.where(key // ATT_HEADS == frame, 1.0, 0.0).astype(BF16)
        bp = bp_ref[...]
        hi = bp.astype(BF16)
        lo = (bp - hi.astype(F32)).astype(BF16)
        wide = (jnp.dot(hi, spread, preferred_element_type=F32)
                + jnp.dot(lo, spread, preferred_element_type=F32))
        q_head = lax.broadcasted_iota(jnp.int32, wide.shape, 0) // q_ref.shape[0]
        k_head = lax.broadcasted_iota(jnp.int32, wide.shape, 1) % ATT_HEADS
        bias_ref[...] = jnp.where(q_head == k_head, wide, NEG)

    heads = lambda ref, off: jnp.concatenate(
        [ref[:, off + h * ATT_HD:off + (h + 1) * ATT_HD] for h in range(ATT_HEADS)], axis=0)
    q = heads(q_ref, 0)
    k_new = heads(kvn_ref, 0).astype(BF16)
    v_new = heads(kvn_ref, D_MODEL).astype(BF16)
    k_past = ck_ref[0].reshape(n_keys, ATT_HD).astype(BF16)
    v_past = cv_ref[0].reshape(n_keys, ATT_HD).astype(BF16)
    sp = lax.dot_general(q, k_past, nt, preferred_element_type=F32) + bias_ref[...]
    sn = lax.dot_general(q, k_new, nt, preferred_element_type=F32) + bn_ref[...]
    mx = jnp.maximum(jnp.max(sp, axis=-1, keepdims=True), jnp.max(sn, axis=-1, keepdims=True))
    pp = jnp.exp2(sp - mx)
    pn = jnp.exp2(sn - mx)
    denom = jnp.sum(pp, axis=-1, keepdims=True) + jnp.sum(pn, axis=-1, keepdims=True)
    o = (jnp.dot(pp.astype(BF16), v_past, preferred_element_type=F32)
         + jnp.dot(pn.astype(BF16), v_new, preferred_element_type=F32)) / denom
    seq = q_ref.shape[0]
    for h in range(ATT_HEADS):
        o_ref[:, h * ATT_HD:(h + 1) * ATT_HD] = o[h * seq:(h + 1) * seq, :].astype(o_ref.dtype)


def _step_attn(q, kv_new, cache_k, cache_v, table, batch, seq):
    n_past = cache_k.shape[1]
    dist = n_past + jnp.arange(seq)[:, None] - jnp.arange(n_past + seq)[None, :]
    bias = table[:, jnp.clip(dist, -MAX_REL, MAX_REL) + MAX_REL].astype(F32) * LOG2E
    rows = ATT_HEADS * seq
    bias_past = bias[:, :, :n_past].reshape(rows, n_past)
    same_head = jnp.eye(ATT_HEADS, dtype=bool)[:, None, :, None]
    bias_new = jnp.where(same_head, bias[:, :, None, n_past:], NEG).reshape(rows, rows)
    return pl.pallas_call(
        _step_attn_kernel,
        grid=(batch,),
        in_specs=[
            pl.BlockSpec((seq, D_MODEL), lambda b: (b, 0)),
            pl.BlockSpec((seq, 2 * D_MODEL), lambda b: (b, 0)),
            pl.BlockSpec((1, n_past, ATT_HEADS, ATT_HD), lambda b: (b, 0, 0, 0)),
            pl.BlockSpec((1, n_past, ATT_HEADS, ATT_HD), lambda b: (b, 0, 0, 0)),
            pl.BlockSpec((rows, n_past), lambda b: (0, 0)),
            pl.BlockSpec((rows, rows), lambda b: (0, 0)),
        ],
        out_specs=pl.BlockSpec((seq, D_MODEL), lambda b: (b, 0)),
        out_shape=jax.ShapeDtypeStruct((batch * seq, D_MODEL), BF16),
        scratch_shapes=[pltpu.VMEM((rows, n_past * ATT_HEADS), F32)],
        compiler_params=_params("arbitrary"),
        name="step_attn",
    )(q, kv_new, cache_k, cache_v, bias_past, bias_new)


def _trunk(x, ada, ada_kv, row_block, gla_s0, past, wts):
    batch, seq, _ = x.shape
    m = batch * seq
    tm = min(ROW_TILE, m)
    x = x.reshape(m, D_MODEL)
    assert batch <= MOD_ROWS and (batch == 1 or m == tm)
    per_seq = None if batch == 1 else seq

    def mod(l, sub, which):
        return _Mod(ada, l, row_block, 3 * sub + which, per_seq)

    def ffn(x, l, sub, idx):
        return (x, mod(l, sub, 0), mod(l, sub, 1), mod(l, sub, 2)), (l, sub, idx)

    def side_jobs(ffn_keys, names, n_steps):
        ffn_keys = [key for key in ffn_keys if key not in wts["ffn"]]
        names = [name for name in names if name not in wts]
        jobs = tuple(job for key in ffn_keys for job in _ffn_cast_jobs(*wts["ffn_f32"], *key, n_steps))
        jobs += tuple(_CastJob(wts["f32"][name], 0, 0, D_MODEL // n_steps, (wts["f32"][name].shape[-1],))
                      for name in names)

        def keep(cast):
            for n, key in enumerate(ffn_keys):
                wts["ffn"][key] = tuple(cast[3 * n:3 * n + 3])
            for n, name in enumerate(names):
                wts[name] = cast[3 * len(ffn_keys) + n]
        return jobs, keep

    x = yield ffn(x, 0, 0, 0)
    proj, gk_low = _modmm(x, mod(0, 1, 0), mod(0, 1, 1), wts["a_in_t"], tm, BF16,
                          widths=(GLA_MAIN, GLA_GATE_RANK), w_is_t=True)
    gla_heads = GLA_HEADS_PER_STEP if batch == 1 else GLA_HEADS
    jobs, keep = side_jobs(((0, 1), (1, 0)), ("a_out",), _gla_steps(batch, seq, gla_heads))
    o, gla_state, *cast = _gla(proj, gk_low, wts["a_gk2"], wts["b_a_gk"], wts["g_a_norm"], gla_s0, batch, seq,
                               gla_heads, jobs)
    keep(cast)
    jobs, keep = side_jobs((), ("kv", "b_q", "b_out"), m // tm)
    x, *cast = _proj_ln(o, wts["a_out"], x, mod(0, 1, 2), wts["ln_g"][0, 1][None], wts["ln_b"][0, 1][None], tm, jobs)
    keep(cast)
    x = yield ffn(x, 0, 2, 1)

    kv_shift = _Mod(ada_kv, 0, row_block, 0, per_seq)
    kv_scale = _Mod(ada_kv, 0, row_block, 1, per_seq)
    if past is None:
        assert tm == min(BAND_PAST, seq)
        kv, kv_out = _modmm(x, kv_shift, kv_scale, wts["kv"], tm, BF16, tail_f32=True)
    else:
        kv_out = _modmm(x, kv_shift, kv_scale, wts["kv"], tm, F32)

    x = yield ffn(x, 1, 0, 0)
    q = _modmm(x, mod(1, 1, 0), mod(1, 1, 1), wts["b_q"], tm, BF16,
               out_scale=ATT_HD ** -0.5 * LOG2E)
    if past is None:
        jobs, keep = side_jobs(((1, 1),), (), (ATT_HEADS // ATT_HEADS_PER_STEP) * (seq // BAND_PAST))
        o, *cast = _band_attn(q, kv, wts["rel_bias"], jobs)
        keep(cast)
    else:
        o = _step_attn(q, kv_out, past[0], past[1], wts["rel_bias"], batch, seq)
    x, = _proj_ln(o, wts["b_out"], x, mod(1, 1, 2), wts["ln_g"][1, 1][None], wts["ln_b"][1, 1][None], tm)
    x = yield ffn(x, 1, 2, 1)

    n_rows = kv_out.shape[0] // batch
    k_out = kv_out[:, :D_MODEL].reshape(batch, n_rows, ATT_HEADS, ATT_HD)
    v_out = kv_out[:, D_MODEL:].reshape(batch, n_rows, ATT_HEADS, ATT_HD)
    return x.reshape(batch, seq, D_MODEL), gla_state[None], k_out, v_out


def kernel(x_prompt, x_sample, state_gla, cache_band_k, cache_band_v, c_prompt, c_sample, w_ada, b_ada, ln_g, ln_b, w_ffn_up, w_ffn_down, w_a_in, w_a_gk2, b_a_gk, g_a_norm, w_a_out, w_ada_kv, b_ada_kv, w_kv, w_b_q, rel_bias, w_b_out):
    assert DEPTH == 2 and w_a_in.shape[0] == 1 and w_b_q.shape[0] == 1
    n_prompt, n_sample = x_prompt.shape[0], x_sample.shape[0]
    assert n_prompt == 1 and n_sample <= MOD_ROWS

    lead = lambda w: w.reshape((1, 1) + w.shape[-2:])
    wts = {
        "ffn_f32": (w_ffn_up, w_ffn_down),
        "ffn": {(0, 0): (*_cast_split(w_ffn_up, (D_FF, D_FF), CAST_ROWS),
                         *_cast_split(w_ffn_down, (D_MODEL,), D_FF // 8))},
        "a_in_t": _cast_split(lead(jnp.swapaxes(w_a_in, 1, 2)), (D_MODEL,), A_IN_CAST_ROWS)[0],
        "a_gk2": w_a_gk2[0].astype(BF16),
        "b_a_gk": b_a_gk[0][None],
        "g_a_norm": g_a_norm[0][None],
        "f32": {"a_out": lead(w_a_out), "kv": lead(w_kv), "b_q": lead(w_b_q), "b_out": lead(w_b_out)},
        "rel_bias": rel_bias[0],
        "ln_g": ln_g,
        "ln_b": ln_b,
    }

    c_all = jnp.concatenate([jnp.pad(c, ((0, MOD_ROWS - c.shape[0]), (0, 0))) for c in (c_prompt, c_sample)], axis=0)
    ada = _ada(c_all, w_ada, b_ada)
    ada_kv = _ada(c_all, w_ada_kv[None], b_ada_kv[None])

    gla_zero = jnp.zeros((n_prompt, GLA_HEADS, GLA_HK, GLA_HV), state_gla.dtype)
    trunks = [_trunk(x_prompt, ada, ada_kv, 0, gla_zero, None, wts),
              _trunk(x_sample, ada, ada_kv, 1, state_gla[0], (cache_band_k, cache_band_v), wts)]
    requests = [next(trunk) for trunk in trunks]
    results = [None, None]
    while None in results:
        streams = [stream for stream, _ in requests]
        (l, sub, idx), = {key for _, key in requests}
        new_x = _ffn(streams, *wts["ffn"][l, idx], wts["ln_g"][l, sub][None], wts["ln_b"][l, sub][None],
                     FFN_ROW_TILE)
        for n, (trunk, x) in enumerate(zip(trunks, new_x)):
            try:
                requests[n] = trunk.send(x)
            except StopIteration as done:
                results[n] = done.value
    (y_p, s_p, k_p, v_p), (y_s, s_s, k_s, v_s) = results
    return (y_p, y_s, s_p, s_s, k_p, v_p, k_s, v_s)
```

```python
import functools
from typing import NamedTuple

import jax
import jax.numpy as jnp
from jax import lax
from jax.experimental import pallas as pl
from jax.experimental.pallas import tpu as pltpu

F32 = jnp.float32
BF16 = jnp.bfloat16

D_MODEL = 2048
DEPTH = 2
CHUNK = 64
GLA_HEADS = 4
GLA_DK = D_MODEL // 2
GLA_DV = D_MODEL
GLA_HK = GLA_DK // GLA_HEADS
GLA_HV = GLA_DV // GLA_HEADS
GLA_GATE_RANK = 16
GLA_GATE_NORMALIZER = 16.0
GLA_MAIN = 2 * GLA_DK + 2 * GLA_DV
ATT_HEADS = 16
ATT_HD = D_MODEL // ATT_HEADS
BAND_PAST = 8 * CHUNK
MAX_REL = 256
D_FF = 5504
ALPHA = (2 * DEPTH) ** 0.25
LN_EPS = 1e-5
RMS_EPS = 1e-6
NEG = -1e30
LOG2E = 1.4426950408889634

LANE = 128
FF_TILE = 512
FF_LAST = D_FF - (D_FF - 1) // FF_TILE * FF_TILE
ROW_TILE = 512
FFN_ROW_TILE = 1024
A_IN_CAST_ROWS = 560
CAST_ROWS = 256
FFN_CAST_DOWN_ROWS = 128
GLA_HEADS_PER_STEP = 2
PROJ_SUB = 128
MODMM_COLS = 1024
ATT_SUB = 256
ATT_HEADS_PER_STEP = 4
REL_LEN = BAND_PAST + 2 * ATT_SUB
MOD_ROWS = 8
ADA_ROWS = 2 * MOD_ROWS
ADA_TILE = 2048
VMEM_LIMIT = 56 * 1024 * 1024
FFN_VMEM_LIMIT = 60 * 1024 * 1024


def _params(*sem):
    return pltpu.CompilerParams(dimension_semantics=sem, vmem_limit_bytes=VMEM_LIMIT)


def _layer_norm(z, g, b):
    mu = jnp.mean(z, axis=-1, keepdims=True)
    zc = z - mu
    var = jnp.mean(zc * zc, axis=-1, keepdims=True)
    return zc * lax.rsqrt(var + LN_EPS) * g + b


def _silu(a):
    half = 0.5 * a
    return half + half * jnp.tanh(half)


def _ada_kernel(c_ref, w_ref, b_ref, o_ref):
    o_ref[...] = jnp.dot(_silu(c_ref[...]), w_ref[...], preferred_element_type=F32) + b_ref[...]


def _ada(c_all, w, b):
    n_l, _, n = w.shape
    return pl.pallas_call(
        _ada_kernel,
        grid=(n_l, n // ADA_TILE),
        in_specs=[
            pl.BlockSpec((ADA_ROWS, D_MODEL), lambda l, j: (0, 0)),
            pl.BlockSpec((None, D_MODEL, ADA_TILE), lambda l, j: (l, 0, j)),
            pl.BlockSpec((None, 1, ADA_TILE), lambda l, j: (l, 0, j)),
        ],
        out_specs=pl.BlockSpec((None, ADA_ROWS, ADA_TILE), lambda l, j: (l, 0, j)),
        out_shape=jax.ShapeDtypeStruct((n_l, ADA_ROWS, n), F32),
        compiler_params=_params("arbitrary", "arbitrary"),
        name="ada",
    )(c_all, w, b.reshape(n_l, 1, n))


class _Mod(NamedTuple):
    table: jax.Array
    layer: int
    row_block: int
    col: int
    seq: int | None

    @property
    def spec(self):
        return pl.BlockSpec((None, MOD_ROWS, D_MODEL), lambda *_: (self.layer, self.row_block, self.col))


def _mod_rows(ref, n_rows, seq):
    if seq is None:
        return ref[0:1, :]
    n_seq = n_rows // seq
    return jnp.broadcast_to(ref[0:n_seq, :][:, None, :], (n_seq, seq, D_MODEL)).reshape(n_rows, D_MODEL)


def _ffn_kernel(*refs, seqs):
    n = len(seqs)
    wa_ref, wu_ref, wd_ref, g_ref, b_ref = refs[4 * n:4 * n + 5]
    o_refs = refs[4 * n + 5:5 * n + 5]
    h_refs = refs[5 * n + 5:]
    i, j = pl.program_id(0), pl.program_id(1)
    last = pl.num_programs(1) - 1

    def stream(x_ref, sh_ref, sc_ref, gt_ref, o_ref, h_ref, seq, active):
        mod = lambda ref: _mod_rows(ref, x_ref.shape[0], seq)

        @pl.when(active & (j == 0))
        def _():
            h_ref[...] = (x_ref[...] * (1.0 + mod(sc_ref)) + mod(sh_ref)).astype(BF16)
            o_ref[...] = jnp.zeros_like(o_ref)

        def chunk(cols):
            h = h_ref[...]
            a = jnp.dot(h, wa_ref[:, :cols], preferred_element_type=F32)
            u = jnp.dot(h, wu_ref[:, :cols], preferred_element_type=F32)
            o_ref[...] += jnp.dot((_silu(a) * u).astype(BF16), wd_ref[:cols, :], preferred_element_type=F32)

        @pl.when(active & (j < last))
        def _():
            chunk(FF_TILE)

        @pl.when(active & (j == last))
        def _():
            chunk(FF_LAST)
            z = ALPHA * x_ref[...] + (0.5 * (1.0 + mod(gt_ref))) * o_ref[...]
            o_ref[...] = _layer_norm(z, g_ref[...], b_ref[...])

    for s in range(n):
        active = True if s == 0 else (i == pl.num_programs(0) - 1)
        stream(*refs[4 * s:4 * s + 4], o_refs[s], h_refs[s], seqs[s], active)


def _ffn(streams, w_a, w_u, w_down, ln_g, ln_b, tm):
    m = streams[0][0].shape[0]
    n_ff = pl.cdiv(D_FF, FF_TILE)
    vec = pl.BlockSpec((1, D_MODEL), lambda i, j: (0, 0))
    up = pl.BlockSpec((D_MODEL, FF_TILE), lambda i, j: (0, j))
    x_specs = [pl.BlockSpec((tm, D_MODEL), lambda i, j: (i, 0))]
    x_specs += [pl.BlockSpec(x.shape, lambda i, j: (0, 0)) for x, *_ in streams[1:]]
    in_specs, args = [], []
    for x_spec, (x, shift, scale, gate) in zip(x_specs, streams):
        in_specs += [x_spec, shift.spec, scale.spec, gate.spec]
        args += [x, shift.table, scale.table, gate.table]
    return pl.pallas_call(
        functools.partial(_ffn_kernel, seqs=tuple(shift.seq for _, shift, *_ in streams)),
        grid=(m // tm, n_ff),
        in_specs=[
            *in_specs,
            up, up,
            pl.BlockSpec((FF_TILE, D_MODEL), lambda i, j: (j, 0)),
            vec, vec,
        ],
        out_specs=x_specs,
        out_shape=[jax.ShapeDtypeStruct(x.shape, F32) for x, *_ in streams],
        scratch_shapes=[pltpu.VMEM(spec.block_shape, BF16) for spec in x_specs],
        compiler_params=pltpu.CompilerParams(dimension_semantics=("arbitrary", "arbitrary"),
                                             vmem_limit_bytes=FFN_VMEM_LIMIT),
        name="ffn",
    )(*args, w_a, w_u, w_down, ln_g, ln_b)


def _cast_kernel(x_ref, *o_refs):
    off = 0
    for o_ref in o_refs:
        n = o_ref.shape[-1]
        o_ref[...] = x_ref[:, off:off + n].astype(o_ref.dtype)
        off += n


class _CastJob(NamedTuple):
    src: jax.Array
    layer: int
    idx: int
    rows: int
    widths: tuple

    @property
    def n_blocks(self):
        return self.src.shape[2] // self.rows

    def specs(self, step_of):
        r, n = self.src.shape[2:]
        assert r % self.rows == 0 and sum(self.widths) <= n and all(wd % LANE == 0 for wd in self.widths)
        blk = lambda *ids: jnp.minimum(step_of(*ids), self.n_blocks - 1)
        in_spec = pl.BlockSpec((None, None, self.rows, n), lambda *ids: (self.layer, self.idx, blk(*ids), 0))
        out_specs = [pl.BlockSpec((self.rows, wd), lambda *ids: (blk(*ids), 0)) for wd in self.widths]
        out_shape = [jax.ShapeDtypeStruct((r, wd), BF16) for wd in self.widths]
        return in_spec, out_specs, out_shape


def _side_casts(step, n_steps, jobs, in_refs, out_refs):
    pos = 0
    for job, x_ref in zip(jobs, in_refs):
        outs = out_refs[pos:pos + len(job.widths)]
        pos += len(job.widths)
        if job.n_blocks == n_steps:
            _cast_kernel(x_ref, *outs)
        else:
            pl.when(step < job.n_blocks)(functools.partial(_cast_kernel, x_ref, *outs))


def _job_specs(jobs, step_of, n_steps):
    in_specs, out_specs, out_shape = [], [], []
    for job in jobs:
        assert job.n_blocks <= n_steps
        i, o, s = job.specs(step_of)
        in_specs.append(i)
        out_specs += o
        out_shape += s
    return in_specs, out_specs, out_shape


def _cast_split(w, widths, rows, layer=0, idx=0):
    job = _CastJob(w, layer, idx, rows, tuple(widths))
    in_spec, out_specs, out_shape = job.specs(lambda i: i)
    return pl.pallas_call(
        _cast_kernel,
        grid=(job.n_blocks,),
        in_specs=[in_spec],
        out_specs=out_specs,
        out_shape=out_shape,
        compiler_params=_params("parallel"),
        name="cast",
    )(w)


def _ffn_cast_jobs(w_ffn_up, w_ffn_down, layer, idx, n_steps):
    return (_CastJob(w_ffn_up, layer, idx, D_MODEL // n_steps, (D_FF, D_FF)),
            _CastJob(w_ffn_down, layer, idx, FFN_CAST_DOWN_ROWS, (D_MODEL,)))


def _modmm_kernel(x_ref, sh_ref, sc_ref, w_ref, o_ref, *rest, seq, w_is_t, out_scale, tail_f32):
    h_ref = rest[-1]
    mod = lambda ref: _mod_rows(ref, x_ref.shape[0], seq)
    h_ref[...] = (x_ref[...] * (1.0 + mod(sc_ref)) + mod(sh_ref)).astype(BF16)
    h = h_ref[...]
    nt = (((1,), (1,)), ((), ()))

    def project(lo, hi):
        if w_is_t:
            y = lax.dot_general(h, w_ref[lo:hi, :], nt, preferred_element_type=F32)
        else:
            y = jnp.dot(h, w_ref[:, lo:hi], preferred_element_type=F32)
        return y if out_scale is None else y * out_scale

    n = o_ref.shape[1]
    if tail_f32:
        tail_ref, = rest[:-1]
        for lo in range(0, n, MODMM_COLS):
            y = project(lo, lo + MODMM_COLS)
            o_ref[:, lo:lo + MODMM_COLS] = y.astype(o_ref.dtype)
            tail_ref[:, lo:lo + MODMM_COLS] = y
        return
    for lo in range(0, n, MODMM_COLS):
        o_ref[:, lo:lo + MODMM_COLS] = project(lo, lo + MODMM_COLS).astype(o_ref.dtype)
    for extra_ref in rest[:-1]:
        extra_ref[...] = project(n, n + extra_ref.shape[1]).astype(extra_ref.dtype)
        n += extra_ref.shape[1]


def _modmm(x, shift, scale, w, tm, out_dtype, widths=None, w_is_t=False, out_scale=None, tail_f32=False):
    m = x.shape[0]
    widths = widths or (w.shape[0] if w_is_t else w.shape[1],)
    assert widths[0] % MODMM_COLS == 0 and not (tail_f32 and len(widths) > 1)
    out_specs = [pl.BlockSpec((tm, wd), lambda i: (i, 0)) for wd in widths]
    out_shape = [jax.ShapeDtypeStruct((m, wd), out_dtype) for wd in widths]
    if tail_f32:
        out_specs.append(pl.BlockSpec((tm, widths[0]), lambda i: (0, 0)))
        out_shape.append(jax.ShapeDtypeStruct((tm, widths[0]), F32))
    out = pl.pallas_call(
        functools.partial(_modmm_kernel, seq=shift.seq, w_is_t=w_is_t, out_scale=out_scale, tail_f32=tail_f32),
        grid=(m // tm,),
        in_specs=[
            pl.BlockSpec((tm, D_MODEL), lambda i: (i, 0)),
            shift.spec, scale.spec,
            pl.BlockSpec(w.shape, lambda i: (0, 0), pipeline_mode=pl.Buffered(1)),
        ],
        out_specs=out_specs,
        out_shape=out_shape,
        scratch_shapes=[pltpu.VMEM((tm, D_MODEL), BF16)],
        compiler_params=_params("arbitrary"),
        name="modmm",
    )(x, shift.table, scale.table, w)
    return out if len(out) > 1 else out[0]


def _proj_ln_kernel(a_ref, w_ref, x_ref, gt_ref, g_ref, b_ref, *rest, seq, jobs, n_steps):
    n_jobs = len(jobs)
    o_ref = rest[n_jobs]
    _side_casts(pl.program_id(0), n_steps, jobs, rest[:n_jobs], rest[n_jobs + 1:])
    sub = min(PROJ_SUB, a_ref.shape[0])
    n_sub = a_ref.shape[0] // sub
    dot = lambda r: jnp.dot(a_ref[r * sub:(r + 1) * sub, :], w_ref[...], preferred_element_type=F32)
    gate = _mod_rows(gt_ref, a_ref.shape[0], seq)
    y = dot(0)
    for r in range(n_sub):
        y_next = dot(r + 1) if r + 1 < n_sub else None
        rows = slice(r * sub, (r + 1) * sub)
        z = ALPHA * x_ref[rows, :] + (1.0 + gate[rows if gate.shape[0] > 1 else slice(None), :]) * y
        o_ref[rows, :] = _layer_norm(z, g_ref[...], b_ref[...])
        y = y_next


def _proj_ln(a, w, x, gate, ln_g, ln_b, tm, jobs=()):
    m = x.shape[0]
    vec = pl.BlockSpec((1, D_MODEL), lambda i: (0, 0))
    job_in, job_out, job_shape = _job_specs(jobs, lambda i: i, m // tm)
    return pl.pallas_call(
        functools.partial(_proj_ln_kernel, seq=gate.seq, jobs=jobs, n_steps=m // tm),
        grid=(m // tm,),
        in_specs=[
            pl.BlockSpec((tm, D_MODEL), lambda i: (i, 0)),
            pl.BlockSpec((D_MODEL, D_MODEL), lambda i: (0, 0)),
            pl.BlockSpec((tm, D_MODEL), lambda i: (i, 0)),
            gate.spec,
            vec, vec,
            *job_in,
        ],
        out_specs=[pl.BlockSpec((tm, D_MODEL), lambda i: (i, 0)), *job_out],
        out_shape=[jax.ShapeDtypeStruct((m, D_MODEL), F32), *job_shape],
        compiler_params=_params("arbitrary"),
        name="proj_ln",
    )(a, w, x, gate.table, ln_g, ln_b, *(job.src for job in jobs))


def _gla_kernel(q_ref, k_ref, v_ref, g_ref, gk_ref, w2_ref, bgk_ref, gn_ref, s0_ref, *rest,
                blk, n_blk, n_heads, jobs, n_steps):
    n_jobs = len(jobs)
    o_ref, s_ref = rest[n_jobs:n_jobs + 2]
    step = (pl.program_id(0) * pl.num_programs(1) + pl.program_id(1)) * pl.num_programs(2) + pl.program_id(2)
    _side_casts(step, n_steps, jobs, rest[:n_jobs], rest[n_jobs + 2:])

    @pl.when(pl.program_id(2) == 0)
    def _():
        s_ref[...] = s0_ref[...]

    causal = (lax.broadcasted_iota(jnp.int32, (blk, blk), 0)
              >= lax.broadcasted_iota(jnp.int32, (blk, blk), 1))
    tril = jnp.broadcast_to(jnp.where(causal, 1.0, 0.0).astype(BF16), (n_blk, blk, blk))
    eye = (lax.broadcasted_iota(jnp.int32, (GLA_HK, GLA_HK), 0)
           == lax.broadcasted_iota(jnp.int32, (GLA_HK, GLA_HK), 1))
    tn = (((0,), (0,)), ((), ()))
    bnn = (((2,), (1,)), ((0,), (0,)))
    bnt = (((2,), (2,)), ((0,), (0,)))

    def blocks(t):
        return t.reshape(n_blk, blk, t.shape[-1])

    for head in range(n_heads):
        kc = slice(head * GLA_HK, (head + 1) * GLA_HK)
        vc = slice(head * GLA_HV, (head + 1) * GLA_HV)
        z = jnp.dot(gk_ref[...], w2_ref[:, kc], preferred_element_type=F32) + bgk_ref[:, kc]
        log_a = blocks((jnp.minimum(z, 0.0) - jnp.log(1.0 + jnp.exp(-jnp.abs(z)))) * (1.0 / GLA_GATE_NORMALIZER))
        hi = log_a.astype(BF16)
        lo = (log_a - hi.astype(F32)).astype(BF16)
        b = (lax.dot_general(tril, hi, bnn, preferred_element_type=F32)
             + lax.dot_general(tril, lo, bnn, preferred_element_type=F32))
        b_last = b[:, blk - 1:blk, :]
        q = blocks(q_ref[:, kc].astype(F32)) * (GLA_HK ** -0.5)
        k = blocks(k_ref[:, kc].astype(F32))
        v = blocks(v_ref[:, vc])
        q_t = (q * jnp.exp(b)).astype(BF16)
        k_t = (k * jnp.exp(-b)).astype(BF16)
        k_dec = (k * jnp.exp(b_last - b)).astype(BF16)
        scores = lax.dot_general(q_t, k_t, bnt, preferred_element_type=F32)
        scores = jnp.where(causal[None], scores, 0.0).astype(BF16)
        o_intra = lax.dot_general(scores, v, bnn, preferred_element_type=F32)
        decay = jnp.exp(b_last)

        update = [lax.dot_general(k_dec[c], v[c], tn, preferred_element_type=F32) for c in range(n_blk)]
        decay_col = [jnp.sum(jnp.where(eye, jnp.broadcast_to(decay[c], (GLA_HK, GLA_HK)), 0.0),
                             axis=1, keepdims=True) for c in range(n_blk)]
        state = s_ref[0, head]
        starts = []
        for c in range(n_blk):
            starts.append(state.astype(BF16))
            state = decay_col[c] * state + update[c]
        s_ref[0, head] = state
        o = o_intra + jnp.stack([jnp.dot(q_t[c], starts[c], preferred_element_type=F32) for c in range(n_blk)])
        on = o * lax.rsqrt(jnp.mean(o * o, axis=-1, keepdims=True) + RMS_EPS) * gn_ref[...]
        gate = _silu(g_ref[:, vc].astype(F32)).reshape(o.shape)
        o_ref[:, vc] = (on * gate).reshape(o_ref.shape[0], GLA_HV).astype(o_ref.dtype)


def _gla_steps(batch, seq, n_heads):
    return batch * (GLA_HEADS // n_heads) * (seq // min(ROW_TILE, seq))


def _gla(proj, gk_low, w_gk2, b_gk, g_norm, s0, batch, seq, n_heads, jobs=()):
    blk = min(CHUNK, seq)
    rows = min(ROW_TILE, seq)
    n_steps = seq // rows
    n_groups = GLA_HEADS // n_heads
    m = batch * seq
    wk, wv = n_heads * GLA_HK, n_heads * GLA_HV
    k_off = GLA_DK // wk
    v_off = 2 * GLA_DK // wv
    g_off = (2 * GLA_DK + GLA_DV) // wv
    row = lambda b, h, c: b * n_steps + c
    state_spec = pl.BlockSpec((1, n_heads, GLA_HK, GLA_HV), lambda b, h, c: (b, h, 0, 0))
    grid_steps = _gla_steps(batch, seq, n_heads)
    job_in, job_out, job_shape = _job_specs(jobs, lambda b, h, c: (b * n_groups + h) * n_steps + c, grid_steps)
    return pl.pallas_call(
        functools.partial(_gla_kernel, blk=blk, n_blk=rows // blk, n_heads=n_heads, jobs=jobs, n_steps=grid_steps),
        grid=(batch, n_groups, n_steps),
        in_specs=[
            pl.BlockSpec((rows, wk), lambda b, h, c: (row(b, h, c), h)),
            pl.BlockSpec((rows, wk), lambda b, h, c: (row(b, h, c), k_off + h)),
            pl.BlockSpec((rows, wv), lambda b, h, c: (row(b, h, c), v_off + h)),
            pl.BlockSpec((rows, wv), lambda b, h, c: (row(b, h, c), g_off + h)),
            pl.BlockSpec((rows, GLA_GATE_RANK), lambda b, h, c: (row(b, h, c), 0)),
            pl.BlockSpec((GLA_GATE_RANK, wk), lambda b, h, c: (0, h)),
            pl.BlockSpec((1, wk), lambda b, h, c: (0, h)),
            pl.BlockSpec((1, GLA_HV), lambda b, h, c: (0, 0)),
            state_spec,
            *job_in,
        ],
        out_specs=[
            pl.BlockSpec((rows, wv), lambda b, h, c: (row(b, h, c), h)),
            state_spec,
            *job_out,
        ],
        out_shape=[
            jax.ShapeDtypeStruct((m, GLA_DV), BF16),
            jax.ShapeDtypeStruct(s0.shape, F32),
            *job_shape,
        ],
        compiler_params=_params("arbitrary", "arbitrary", "arbitrary"),
        name="gla",
    )(proj, proj, proj, proj, gk_low, w_gk2, b_gk, g_norm, s0, *(job.src for job in jobs))


def _band_attn_kernel(q_ref, kp_ref, kc_ref, vp_ref, vc_ref, rel_ref, *rest, jobs, n_steps):
    n_jobs = len(jobs)
    o_ref, bias_ref = rest[n_jobs], rest[-1]
    i = pl.program_id(1)
    _side_casts(pl.program_id(0) * pl.num_programs(1) + i, n_steps, jobs, rest[:n_jobs], rest[n_jobs + 1:-1])
    nt = (((1,), (1,)), ((), ()))
    n_keys = BAND_PAST + ATT_SUB
    n_sub = BAND_PAST // ATT_SUB

    @pl.when(i <= 1)
    def _():
        col = lax.broadcasted_iota(jnp.int32, (ATT_SUB, n_keys), 1)
        qc = lax.broadcasted_iota(jnp.int32, (ATT_SUB, n_keys), 0) // CHUNK
        kc = col // CHUNK
        in_band = (kc >= qc) & (kc <= qc + BAND_PAST // CHUNK)
        for h in range(ATT_HEADS_PER_STEP):
            rel = jnp.broadcast_to(rel_ref[h], (ATT_SUB, REL_LEN))
            toeplitz = pltpu.roll(rel, 0, 1, stride=1, stride_axis=0)[:, ATT_SUB:] * LOG2E
            for s in range(n_sub):
                key_pos = (i - 1) * BAND_PAST + s * ATT_SUB + col
                bias_ref[h, s] = jnp.where(in_band & (key_pos >= 0), toeplitz, NEG)

    units = [(h, s) for h in range(ATT_HEADS_PER_STEP) for s in range(n_sub)]

    def scores(h, s):
        lo, hi = s * ATT_SUB, (s + 1) * ATT_SUB
        cols = slice(h * ATT_HD, (h + 1) * ATT_HD)
        k = jnp.concatenate([kp_ref[lo:, cols], kc_ref[:hi, cols]], axis=0)
        return lax.dot_general(q_ref[lo:hi, cols], k, nt, preferred_element_type=F32) + bias_ref[h, s]

    def finish(h, s, sc):
        lo, hi = s * ATT_SUB, (s + 1) * ATT_SUB
        cols = slice(h * ATT_HD, (h + 1) * ATT_HD)
        v = jnp.concatenate([vp_ref[lo:, cols], vc_ref[:hi, cols]], axis=0)
        p = jnp.exp2(sc - jnp.max(sc, axis=-1, keepdims=True))
        denom = jnp.sum(p, axis=-1, keepdims=True)
        o = jnp.dot(p.astype(BF16), v, preferred_element_type=F32) / denom
        o_ref[lo:hi, cols] = o.astype(o_ref.dtype)

    sc = scores(*units[0])
    for n, unit in enumerate(units):
        sc_next = scores(*units[n + 1]) if n + 1 < len(units) else None
        finish(*unit, sc)
        sc = sc_next


def _band_rel(table):
    assert BAND_PAST + ATT_SUB - REL_LEN // 2 == MAX_REL and REL_LEN // 2 <= 2 * MAX_REL
    far = jnp.broadcast_to(table[:, 2 * MAX_REL:], (ATT_HEADS, REL_LEN // 2))
    near = table[:, 2 * MAX_REL - REL_LEN // 2 + 1:][:, ::-1]
    return jnp.concatenate([far, near], axis=1).astype(F32)[:, None, :]


def _band_attn(q, kv, table, jobs=()):
    t = q.shape[0]
    n_keys = BAND_PAST + ATT_SUB
    prev = lambda i: jnp.maximum(i - 1, 0)
    n_groups = ATT_HEADS // ATT_HEADS_PER_STEP
    blk = (BAND_PAST, ATT_HEADS_PER_STEP * ATT_HD)
    n_blocks = t // BAND_PAST
    job_in, job_out, job_shape = _job_specs(jobs, lambda h, i: h * n_blocks + i, n_groups * n_blocks)
    return pl.pallas_call(
        functools.partial(_band_attn_kernel, jobs=jobs, n_steps=n_groups * n_blocks),
        grid=(n_groups, n_blocks),
        in_specs=[
            pl.BlockSpec(blk, lambda h, i: (i, h)),
            pl.BlockSpec(blk, lambda h, i: (prev(i), h)),
            pl.BlockSpec(blk, lambda h, i: (i, h)),
            pl.BlockSpec(blk, lambda h, i: (prev(i), n_groups + h)),
            pl.BlockSpec(blk, lambda h, i: (i, n_groups + h)),
            pl.BlockSpec((ATT_HEADS_PER_STEP, 1, REL_LEN), lambda h, i: (h, 0, 0)),
            *job_in,
        ],
        out_specs=[pl.BlockSpec(blk, lambda h, i: (i, h)), *job_out],
        out_shape=[jax.ShapeDtypeStruct((t, D_MODEL), BF16), *job_shape],
        scratch_shapes=[pltpu.VMEM((ATT_HEADS_PER_STEP, BAND_PAST // ATT_SUB, ATT_SUB, n_keys), F32)],
        compiler_params=_params("arbitrary", "arbitrary"),
        name="band_attn",
    )(q, kv, kv, kv, kv, _band_rel(table), *(job.src for job in jobs))


def _step_attn_kernel(q_ref, kvn_ref, ck_ref, cv_ref, rp_ref, rn_ref, o_ref, bias_ref, bias_new_ref):
    nt = (((1,), (1,)), ((), ()))
    n_past = ck_ref.shape[1]
    n_keys = n_past * ATT_HEADS

    seq = q_ref.shape[0]

    @pl.when(pl.program_id(0) == 0)
    def _():
        def toeplitz(ref):
            rows = [pltpu.roll(jnp.broadcast_to(ref[h], (seq, ref.shape[-1])), 0, 1, stride=1, stride_axis=0)
                    for h in range(ATT_HEADS)]
            return jnp.concatenate(rows, axis=0)

        def spread(bias, col_of_key, head_of_key):
            pick = jnp.where(lax.broadcasted_iota(jnp.int32, col_of_key.shape, 0) == col_of_key, 1.0, 0.0).astype(BF16)
            hi = bias.astype(BF16)
            lo = (bias - hi.astype(F32)).astype(BF16)
            wide = jnp.dot(hi, pick, preferred_element_type=F32) + jnp.dot(lo, pick, preferred_element_type=F32)
            q_head = lax.broadcasted_iota(jnp.int32, wide.shape, 0) // seq
            return jnp.where(q_head == head_of_key[0:1, :], wide, NEG)

        key = lax.broadcasted_iota(jnp.int32, (n_past + LANE, n_keys), 1)
        bias_ref[...] = spread(toeplitz(rp_ref), LANE + key // ATT_HEADS, key % ATT_HEADS)
        key = lax.broadcasted_iota(jnp.int32, (LANE, ATT_HEADS * seq), 1)
        bias_new_ref[...] = spread(toeplitz(rn_ref), seq + key % seq, key // seq)

    heads = lambda ref, off: jnp.concatenate(
        [ref[:, off + h * ATT_HD:off + (h + 1) * ATT_HD] for h in range(ATT_HEADS)], axis=0)
    q = heads(q_ref, 0)
    k_new = heads(kvn_ref, 0).astype(BF16)
    v_new = heads(kvn_ref, D_MODEL).astype(BF16)
    k_past = ck_ref[0].reshape(n_keys, ATT_HD).astype(BF16)
    v_past = cv_ref[0].reshape(n_keys, ATT_HD).astype(BF16)
    sp = lax.dot_general(q, k_past, nt, preferred_element_type=F32) + bias_ref[...]
    sn = lax.dot_general(q, k_new, nt, preferred_element_type=F32) + bias_new_ref[...]
    mx = jnp.maximum(jnp.max(sp, axis=-1, keepdims=True), jnp.max(sn, axis=-1, keepdims=True))
    pp = jnp.exp2(sp - mx)
    pn = jnp.exp2(sn - mx)
    denom = jnp.sum(pp, axis=-1, keepdims=True) + jnp.sum(pn, axis=-1, keepdims=True)
    o = (jnp.dot(pp.astype(BF16), v_past, preferred_element_type=F32)
         + jnp.dot(pn.astype(BF16), v_new, preferred_element_type=F32)) / denom
    for h in range(ATT_HEADS):
        o_ref[:, h * ATT_HD:(h + 1) * ATT_HD] = o[h * seq:(h + 1) * seq, :].astype(o_ref.dtype)


def _step_attn(q, kv_new, cache_k, cache_v, table, batch, seq):
    n_past = cache_k.shape[1]
    rows = ATT_HEADS * seq
    assert n_past >= MAX_REL and seq <= LANE // 2
    n_far = n_past + LANE - MAX_REL + 1
    rel_past = jnp.concatenate([jnp.broadcast_to(table[:, 2 * MAX_REL:], (ATT_HEADS, n_far)),
                                table[:, 2 * MAX_REL - 1:MAX_REL:-1]], axis=1)
    rel_new = table[:, MAX_REL + seq:MAX_REL + seq - LANE:-1]
    rel_past, rel_new = (r.astype(F32)[:, None, :] * LOG2E for r in (rel_past, rel_new))
    return pl.pallas_call(
        _step_attn_kernel,
        grid=(batch,),
        in_specs=[
            pl.BlockSpec((seq, D_MODEL), lambda b: (b, 0)),
            pl.BlockSpec((seq, 2 * D_MODEL), lambda b: (b, 0)),
            pl.BlockSpec((1, n_past, ATT_HEADS, ATT_HD), lambda b: (b, 0, 0, 0)),
            pl.BlockSpec((1, n_past, ATT_HEADS, ATT_HD), lambda b: (b, 0, 0, 0)),
            pl.BlockSpec((ATT_HEADS, 1, n_past + LANE), lambda b: (0, 0, 0)),
            pl.BlockSpec((ATT_HEADS, 1, LANE), lambda b: (0, 0, 0)),
        ],
        out_specs=pl.BlockSpec((seq, D_MODEL), lambda b: (b, 0)),
        out_shape=jax.ShapeDtypeStruct((batch * seq, D_MODEL), BF16),
        scratch_shapes=[pltpu.VMEM((rows, n_past * ATT_HEADS), F32), pltpu.VMEM((rows, rows), F32)],
        compiler_params=_params("arbitrary"),
        name="step_attn",
    )(q, kv_new, cache_k, cache_v, rel_past, rel_new)


def _trunk(x, ada, ada_kv, row_block, gla_s0, past, wts):
    batch, seq, _ = x.shape
    m = batch * seq
    tm = min(ROW_TILE, m)
    x = x.reshape(m, D_MODEL)
    assert batch <= MOD_ROWS and (batch == 1 or m == tm)
    per_seq = None if batch == 1 else seq

    def mod(l, sub, which):
        return _Mod(ada, l, row_block, 3 * sub + which, per_seq)

    def ffn(x, l, sub, idx):
        return (x, mod(l, sub, 0), mod(l, sub, 1), mod(l, sub, 2)), (l, sub, idx)

    def side_jobs(ffn_keys, names, n_steps):
        ffn_keys = [key for key in ffn_keys if key not in wts["ffn"]]
        names = [name for name in names if name not in wts]
        jobs = tuple(job for key in ffn_keys for job in _ffn_cast_jobs(*wts["ffn_f32"], *key, n_steps))
        jobs += tuple(_CastJob(wts["f32"][name], 0, 0, D_MODEL // n_steps, (wts["f32"][name].shape[-1],))
                      for name in names)

        def keep(cast):
            for n, key in enumerate(ffn_keys):
                wts["ffn"][key] = tuple(cast[3 * n:3 * n + 3])
            for n, name in enumerate(names):
                wts[name] = cast[3 * len(ffn_keys) + n]
        return jobs, keep

    x = yield ffn(x, 0, 0, 0)
    proj, gk_low = _modmm(x, mod(0, 1, 0), mod(0, 1, 1), wts["a_in_t"], tm, BF16,
                          widths=(GLA_MAIN, GLA_GATE_RANK), w_is_t=True)
    gla_heads = GLA_HEADS_PER_STEP if batch == 1 else GLA_HEADS
    jobs, keep = side_jobs(((0, 1), (1, 0)), ("a_out",), _gla_steps(batch, seq, gla_heads))
    o, gla_state, *cast = _gla(proj, gk_low, wts["a_gk2"], wts["b_a_gk"], wts["g_a_norm"], gla_s0, batch, seq,
                               gla_heads, jobs)
    keep(cast)
    jobs, keep = side_jobs((), ("kv", "b_q", "b_out"), m // tm)
    x, *cast = _proj_ln(o, wts["a_out"], x, mod(0, 1, 2), wts["ln_g"][0, 1][None], wts["ln_b"][0, 1][None], tm, jobs)
    keep(cast)
    x = yield ffn(x, 0, 2, 1)

    kv_shift = _Mod(ada_kv, 0, row_block, 0, per_seq)
    kv_scale = _Mod(ada_kv, 0, row_block, 1, per_seq)
    if past is None:
        assert tm == min(BAND_PAST, seq)
        kv, kv_out = _modmm(x, kv_shift, kv_scale, wts["kv"], tm, BF16, tail_f32=True)
    else:
        kv_out = _modmm(x, kv_shift, kv_scale, wts["kv"], tm, F32)

    x = yield ffn(x, 1, 0, 0)
    q = _modmm(x, mod(1, 1, 0), mod(1, 1, 1), wts["b_q"], tm, BF16,
               out_scale=ATT_HD ** -0.5 * LOG2E)
    if past is None:
        jobs, keep = side_jobs(((1, 1),), (), (ATT_HEADS // ATT_HEADS_PER_STEP) * (seq // BAND_PAST))
        o, *cast = _band_attn(q, kv, wts["rel_bias"], jobs)
        keep(cast)
    else:
        o = _step_attn(q, kv_out, past[0], past[1], wts["rel_bias"], batch, seq)
    x, = _proj_ln(o, wts["b_out"], x, mod(1, 1, 2), wts["ln_g"][1, 1][None], wts["ln_b"][1, 1][None], tm)
    x = yield ffn(x, 1, 2, 1)

    n_rows = kv_out.shape[0] // batch
    k_out = kv_out[:, :D_MODEL].reshape(batch, n_rows, ATT_HEADS, ATT_HD)
    v_out = kv_out[:, D_MODEL:].reshape(batch, n_rows, ATT_HEADS, ATT_HD)
    return x.reshape(batch, seq, D_MODEL), gla_state[None], k_out, v_out


def kernel(x_prompt, x_sample, state_gla, cache_band_k, cache_band_v, c_prompt, c_sample, w_ada, b_ada, ln_g, ln_b, w_ffn_up, w_ffn_down, w_a_in, w_a_gk2, b_a_gk, g_a_norm, w_a_out, w_ada_kv, b_ada_kv, w_kv, w_b_q, rel_bias, w_b_out):
    assert DEPTH == 2 and w_a_in.shape[0] == 1 and w_b_q.shape[0] == 1
    n_prompt, n_sample = x_prompt.shape[0], x_sample.shape[0]
    assert n_prompt == 1 and n_sample <= MOD_ROWS

    lead = lambda w: w.reshape((1, 1) + w.shape[-2:])
    wts = {
        "ffn_f32": (w_ffn_up, w_ffn_down),
        "ffn": {(0, 0): (*_cast_split(w_ffn_up, (D_FF, D_FF), CAST_ROWS),
                         *_cast_split(w_ffn_down, (D_MODEL,), D_FF // 8))},
        "a_in_t": _cast_split(lead(jnp.swapaxes(w_a_in, 1, 2)), (D_MODEL,), A_IN_CAST_ROWS)[0],
        "a_gk2": w_a_gk2[0].astype(BF16),
        "b_a_gk": b_a_gk[0][None],
        "g_a_norm": g_a_norm[0][None],
        "f32": {"a_out": lead(w_a_out), "kv": lead(w_kv), "b_q": lead(w_b_q), "b_out": lead(w_b_out)},
        "rel_bias": rel_bias[0],
        "ln_g": ln_g,
        "ln_b": ln_b,
    }

    c_all = jnp.concatenate([jnp.pad(c, ((0, MOD_ROWS - c.shape[0]), (0, 0))) for c in (c_prompt, c_sample)], axis=0)
    ada = _ada(c_all, w_ada, b_ada)
    ada_kv = _ada(c_all, w_ada_kv[None], b_ada_kv[None])

    gla_zero = jnp.zeros((n_prompt, GLA_HEADS, GLA_HK, GLA_HV), state_gla.dtype)
    trunks = [_trunk(x_prompt, ada, ada_kv, 0, gla_zero, None, wts),
              _trunk(x_sample, ada, ada_kv, 1, state_gla[0], (cache_band_k, cache_band_v), wts)]
    requests = [next(trunk) for trunk in trunks]
    results = [None, None]
    while None in results:
        streams = [stream for stream, _ in requests]
        (l, sub, idx), = {key for _, key in requests}
        new_x = _ffn(streams, *wts["ffn"][l, idx], wts["ln_g"][l, sub][None], wts["ln_b"][l, sub][None],
                     FFN_ROW_TILE)
        for n, (trunk, x) in enumerate(zip(trunks, new_x)):
            try:
                requests[n] = trunk.send(x)
            except StopIteration as done:
                results[n] = done.value
    (y_p, s_p, k_p, v_p), (y_s, s_s, k_s, v_s) = results
    return (y_p, y_s, s_p, s_s, k_p, v_p, k_s, v_s)
```

```python
import functools
from typing import NamedTuple

import jax
import jax.numpy as jnp
from jax import lax
from jax.experimental import pallas as pl
from jax.experimental.pallas import tpu as pltpu

F32 = jnp.float32
BF16 = jnp.bfloat16

D_MODEL = 2048
DEPTH = 2
CHUNK = 64
GLA_HEADS = 4
GLA_DK = D_MODEL // 2
GLA_DV = D_MODEL
GLA_HK = GLA_DK // GLA_HEADS
GLA_HV = GLA_DV // GLA_HEADS
GLA_GATE_RANK = 16
GLA_GATE_NORMALIZER = 16.0
GLA_MAIN = 2 * GLA_DK + 2 * GLA_DV
ATT_HEADS = 16
ATT_HD = D_MODEL // ATT_HEADS
BAND_PAST = 8 * CHUNK
MAX_REL = 256
D_FF = 5504
ALPHA = (2 * DEPTH) ** 0.25
LN_EPS = 1e-5
RMS_EPS = 1e-6
NEG = -1e30
LOG2E = 1.4426950408889634

LANE = 128
FF_TILE = 512
FF_LAST = D_FF - (D_FF - 1) // FF_TILE * FF_TILE
ROW_TILE = 512
FFN_ROW_TILE = 1024
A_IN_CAST_ROWS = 560
CAST_ROWS = 256
FFN_CAST_DOWN_ROWS = 128
GLA_HEADS_PER_STEP = 2
PROJ_SUB = 128
MODMM_COLS = 1024
ATT_SUB = 256
ATT_HEADS_PER_STEP = 4
REL_LEN = BAND_PAST + 2 * ATT_SUB
MOD_ROWS = 8
ADA_ROWS = 2 * MOD_ROWS
ADA_TILE = 2048
VMEM_LIMIT = 56 * 1024 * 1024
FFN_VMEM_LIMIT = 60 * 1024 * 1024


def _params(*sem):
    return pltpu.CompilerParams(dimension_semantics=sem, vmem_limit_bytes=VMEM_LIMIT)


def _layer_norm(z, g, b):
    mu = jnp.mean(z, axis=-1, keepdims=True)
    zc = z - mu
    var = jnp.mean(zc * zc, axis=-1, keepdims=True)
    return zc * lax.rsqrt(var + LN_EPS) * g + b


def _silu(a):
    half = 0.5 * a
    return half + half * jnp.tanh(half)


def _ada_kernel(c_ref, w_ref, b_ref, o_ref):
    o_ref[...] = jnp.dot(_silu(c_ref[...]), w_ref[...], preferred_element_type=F32) + b_ref[...]


def _ada(c_all, w, b):
    n_l, _, n = w.shape
    return pl.pallas_call(
        _ada_kernel,
        grid=(n_l, n // ADA_TILE),
        in_specs=[
            pl.BlockSpec((ADA_ROWS, D_MODEL), lambda l, j: (0, 0)),
            pl.BlockSpec((None, D_MODEL, ADA_TILE), lambda l, j: (l, 0, j)),
            pl.BlockSpec((None, 1, ADA_TILE), lambda l, j: (l, 0, j)),
        ],
        out_specs=pl.BlockSpec((None, ADA_ROWS, ADA_TILE), lambda l, j: (l, 0, j)),
        out_shape=jax.ShapeDtypeStruct((n_l, ADA_ROWS, n), F32),
        compiler_params=_params("arbitrary", "arbitrary"),
        name="ada",
    )(c_all, w, b.reshape(n_l, 1, n))


class _Mod(NamedTuple):
    table: jax.Array
    layer: int
    row_block: int
    col: int
    seq: int | None

    @property
    def spec(self):
        return pl.BlockSpec((None, MOD_ROWS, D_MODEL), lambda *_: (self.layer, self.row_block, self.col))


def _mod_rows(ref, n_rows, seq):
    if seq is None:
        return ref[0:1, :]
    n_seq = n_rows // seq
    return jnp.broadcast_to(ref[0:n_seq, :][:, None, :], (n_seq, seq, D_MODEL)).reshape(n_rows, D_MODEL)


def _ffn_kernel(*refs, seqs):
    n = len(seqs)
    wa_ref, wu_ref, wd_ref, g_ref, b_ref = refs[4 * n:4 * n + 5]
    o_refs = refs[4 * n + 5:5 * n + 5]
    h_refs = refs[5 * n + 5:]
    i, j = pl.program_id(0), pl.program_id(1)
    last = pl.num_programs(1) - 1

    def stream(x_ref, sh_ref, sc_ref, gt_ref, o_ref, h_ref, seq, active):
        mod = lambda ref: _mod_rows(ref, x_ref.shape[0], seq)

        def chunk(cols, first=False):
            h = h_ref[...]
            a = jnp.dot(h, wa_ref[:, :cols], preferred_element_type=F32)
            u = jnp.dot(h, wu_ref[:, :cols], preferred_element_type=F32)
            y = jnp.dot((_silu(a) * u).astype(BF16), wd_ref[:cols, :], preferred_element_type=F32)
            if first:
                o_ref[...] = y
            else:
                o_ref[...] += y

        @pl.when(active & (j == 0))
        def _():
            h_ref[...] = (x_ref[...] * (1.0 + mod(sc_ref)) + mod(sh_ref)).astype(BF16)
            chunk(FF_TILE, first=True)

        @pl.when(active & (j > 0) & (j < last))
        def _():
            chunk(FF_TILE)

        @pl.when(active & (j == last))
        def _():
            chunk(FF_LAST)
            z = ALPHA * x_ref[...] + (0.5 * (1.0 + mod(gt_ref))) * o_ref[...]
            o_ref[...] = _layer_norm(z, g_ref[...], b_ref[...])

    for s in range(n):
        active = True if s == 0 else (i == pl.num_programs(0) - 1)
        stream(*refs[4 * s:4 * s + 4], o_refs[s], h_refs[s], seqs[s], active)


def _ffn(streams, w_a, w_u, w_down, ln_g, ln_b, tm):
    m = streams[0][0].shape[0]
    n_ff = pl.cdiv(D_FF, FF_TILE)
    vec = pl.BlockSpec((1, D_MODEL), lambda i, j: (0, 0))
    up = pl.BlockSpec((D_MODEL, FF_TILE), lambda i, j: (0, j))
    x_specs = [pl.BlockSpec((tm, D_MODEL), lambda i, j: (i, 0))]
    x_specs += [pl.BlockSpec(x.shape, lambda i, j: (0, 0)) for x, *_ in streams[1:]]
    in_specs, args = [], []
    for x_spec, (x, shift, scale, gate) in zip(x_specs, streams):
        in_specs += [x_spec, shift.spec, scale.spec, gate.spec]
        args += [x, shift.table, scale.table, gate.table]
    return pl.pallas_call(
        functools.partial(_ffn_kernel, seqs=tuple(shift.seq for _, shift, *_ in streams)),
        grid=(m // tm, n_ff),
        in_specs=[
            *in_specs,
            up, up,
            pl.BlockSpec((FF_TILE, D_MODEL), lambda i, j: (j, 0)),
            vec, vec,
        ],
        out_specs=x_specs,
        out_shape=[jax.ShapeDtypeStruct(x.shape, F32) for x, *_ in streams],
        scratch_shapes=[pltpu.VMEM(spec.block_shape, BF16) for spec in x_specs],
        compiler_params=pltpu.CompilerParams(dimension_semantics=("arbitrary", "arbitrary"),
                                             vmem_limit_bytes=FFN_VMEM_LIMIT),
        name="ffn",
    )(*args, w_a, w_u, w_down, ln_g, ln_b)


def _cast_kernel(x_ref, *o_refs):
    off = 0
    for o_ref in o_refs:
        n = o_ref.shape[-1]
        o_ref[...] = x_ref[:, off:off + n].astype(o_ref.dtype)
        off += n


class _CastJob(NamedTuple):
    src: jax.Array
    layer: int
    idx: int
    rows: int
    widths: tuple

    @property
    def n_blocks(self):
        return self.src.shape[2] // self.rows

    def specs(self, step_of):
        r, n = self.src.shape[2:]
        assert r % self.rows == 0 and sum(self.widths) <= n and all(wd % LANE == 0 for wd in self.widths)
        blk = lambda *ids: jnp.minimum(step_of(*ids), self.n_blocks - 1)
        in_spec = pl.BlockSpec((None, None, self.rows, n), lambda *ids: (self.layer, self.idx, blk(*ids), 0))
        out_specs = [pl.BlockSpec((self.rows, wd), lambda *ids: (blk(*ids), 0)) for wd in self.widths]
        out_shape = [jax.ShapeDtypeStruct((r, wd), BF16) for wd in self.widths]
        return in_spec, out_specs, out_shape


def _side_casts(step, n_steps, jobs, in_refs, out_refs):
    pos = 0
    for job, x_ref in zip(jobs, in_refs):
        outs = out_refs[pos:pos + len(job.widths)]
        pos += len(job.widths)
        if job.n_blocks == n_steps:
            _cast_kernel(x_ref, *outs)
        else:
            pl.when(step < job.n_blocks)(functools.partial(_cast_kernel, x_ref, *outs))


def _job_specs(jobs, step_of, n_steps):
    in_specs, out_specs, out_shape = [], [], []
    for job in jobs:
        assert job.n_blocks <= n_steps
        i, o, s = job.specs(step_of)
        in_specs.append(i)
        out_specs += o
        out_shape += s
    return in_specs, out_specs, out_shape


def _cast_split(w, widths, rows, layer=0, idx=0):
    job = _CastJob(w, layer, idx, rows, tuple(widths))
    in_spec, out_specs, out_shape = job.specs(lambda i: i)
    return pl.pallas_call(
        _cast_kernel,
        grid=(job.n_blocks,),
        in_specs=[in_spec],
        out_specs=out_specs,
        out_shape=out_shape,
        compiler_params=_params("parallel"),
        name="cast",
    )(w)


def _ffn_cast_jobs(w_ffn_up, w_ffn_down, layer, idx, n_steps):
    return (_CastJob(w_ffn_up, layer, idx, D_MODEL // n_steps, (D_FF, D_FF)),
            _CastJob(w_ffn_down, layer, idx, FFN_CAST_DOWN_ROWS, (D_MODEL,)))


def _modmm_kernel(x_ref, sh_ref, sc_ref, w_ref, o_ref, *rest, seq, w_is_t, out_scale, tail_f32):
    h_ref = rest[-1]
    mod = lambda ref: _mod_rows(ref, x_ref.shape[0], seq)
    h_ref[...] = (x_ref[...] * (1.0 + mod(sc_ref)) + mod(sh_ref)).astype(BF16)
    h = h_ref[...]
    nt = (((1,), (1,)), ((), ()))

    def project(lo, hi):
        if w_is_t:
            y = lax.dot_general(h, w_ref[lo:hi, :], nt, preferred_element_type=F32)
        else:
            y = jnp.dot(h, w_ref[:, lo:hi], preferred_element_type=F32)
        return y if out_scale is None else y * out_scale

    n = o_ref.shape[1]
    if tail_f32:
        tail_ref, = rest[:-1]
        for lo in range(0, n, MODMM_COLS):
            y = project(lo, lo + MODMM_COLS)
            o_ref[:, lo:lo + MODMM_COLS] = y.astype(o_ref.dtype)
            tail_ref[:, lo:lo + MODMM_COLS] = y
        return
    for lo in range(0, n, MODMM_COLS):
        o_ref[:, lo:lo + MODMM_COLS] = project(lo, lo + MODMM_COLS).astype(o_ref.dtype)
    for extra_ref in rest[:-1]:
        extra_ref[...] = project(n, n + extra_ref.shape[1]).astype(extra_ref.dtype)
        n += extra_ref.shape[1]


def _modmm(x, shift, scale, w, tm, out_dtype, widths=None, w_is_t=False, out_scale=None, tail_f32=False):
    m = x.shape[0]
    widths = widths or (w.shape[0] if w_is_t else w.shape[1],)
    assert widths[0] % MODMM_COLS == 0 and not (tail_f32 and len(widths) > 1)
    out_specs = [pl.BlockSpec((tm, wd), lambda i: (i, 0)) for wd in widths]
    out_shape = [jax.ShapeDtypeStruct((m, wd), out_dtype) for wd in widths]
    if tail_f32:
        out_specs.append(pl.BlockSpec((tm, widths[0]), lambda i: (0, 0)))
        out_shape.append(jax.ShapeDtypeStruct((tm, widths[0]), F32))
    out = pl.pallas_call(
        functools.partial(_modmm_kernel, seq=shift.seq, w_is_t=w_is_t, out_scale=out_scale, tail_f32=tail_f32),
        grid=(m // tm,),
        in_specs=[
            pl.BlockSpec((tm, D_MODEL), lambda i: (i, 0)),
            shift.spec, scale.spec,
            pl.BlockSpec(w.shape, lambda i: (0, 0), pipeline_mode=pl.Buffered(1)),
        ],
        out_specs=out_specs,
        out_shape=out_shape,
        scratch_shapes=[pltpu.VMEM((tm, D_MODEL), BF16)],
        compiler_params=_params("arbitrary"),
        name="modmm",
    )(x, shift.table, scale.table, w)
    return out if len(out) > 1 else out[0]


def _proj_ln_kernel(a_ref, w_ref, x_ref, gt_ref, g_ref, b_ref, *rest, seq, jobs, n_steps):
    n_jobs = len(jobs)
    o_ref = rest[n_jobs]
    _side_casts(pl.program_id(0), n_steps, jobs, rest[:n_jobs], rest[n_jobs + 1:])
    sub = min(PROJ_SUB, a_ref.shape[0])
    n_sub = a_ref.shape[0] // sub
    dot = lambda r: jnp.dot(a_ref[r * sub:(r + 1) * sub, :], w_ref[...], preferred_element_type=F32)
    gate = _mod_rows(gt_ref, a_ref.shape[0], seq)
    y = dot(0)
    for r in range(n_sub):
        y_next = dot(r + 1) if r + 1 < n_sub else None
        rows = slice(r * sub, (r + 1) * sub)
        z = ALPHA * x_ref[rows, :] + (1.0 + gate[rows if gate.shape[0] > 1 else slice(None), :]) * y
        o_ref[rows, :] = _layer_norm(z, g_ref[...], b_ref[...])
        y = y_next


def _proj_ln(a, w, x, gate, ln_g, ln_b, tm, jobs=()):
    m = x.shape[0]
    vec = pl.BlockSpec((1, D_MODEL), lambda i: (0, 0))
    job_in, job_out, job_shape = _job_specs(jobs, lambda i: i, m // tm)
    return pl.pallas_call(
        functools.partial(_proj_ln_kernel, seq=gate.seq, jobs=jobs, n_steps=m // tm),
        grid=(m // tm,),
        in_specs=[
            pl.BlockSpec((tm, D_MODEL), lambda i: (i, 0)),
            pl.BlockSpec((D_MODEL, D_MODEL), lambda i: (0, 0)),
            pl.BlockSpec((tm, D_MODEL), lambda i: (i, 0)),
            gate.spec,
            vec, vec,
            *job_in,
        ],
        out_specs=[pl.BlockSpec((tm, D_MODEL), lambda i: (i, 0)), *job_out],
        out_shape=[jax.ShapeDtypeStruct((m, D_MODEL), F32), *job_shape],
        compiler_params=_params("arbitrary"),
        name="proj_ln",
    )(a, w, x, gate.table, ln_g, ln_b, *(job.src for job in jobs))


def _gla_kernel(q_ref, k_ref, v_ref, g_ref, gk_ref, w2_ref, bgk_ref, gn_ref, s0_ref, *rest,
                blk, n_blk, n_heads, jobs, n_steps):
    n_jobs = len(jobs)
    o_ref, s_ref = rest[n_jobs:n_jobs + 2]
    step = (pl.program_id(0) * pl.num_programs(1) + pl.program_id(1)) * pl.num_programs(2) + pl.program_id(2)
    _side_casts(step, n_steps, jobs, rest[:n_jobs], rest[n_jobs + 2:])

    @pl.when(pl.program_id(2) == 0)
    def _():
        s_ref[...] = s0_ref[...]

    causal = (lax.broadcasted_iota(jnp.int32, (blk, blk), 0)
              >= lax.broadcasted_iota(jnp.int32, (blk, blk), 1))
    tril = jnp.broadcast_to(jnp.where(causal, 1.0, 0.0).astype(BF16), (n_blk, blk, blk))
    eye = (lax.broadcasted_iota(jnp.int32, (GLA_HK, GLA_HK), 0)
           == lax.broadcasted_iota(jnp.int32, (GLA_HK, GLA_HK), 1))
    tn = (((0,), (0,)), ((), ()))
    bnn = (((2,), (1,)), ((0,), (0,)))
    bnt = (((2,), (2,)), ((0,), (0,)))

    def blocks(t):
        return t.reshape(n_blk, blk, t.shape[-1])

    for head in range(n_heads):
        kc = slice(head * GLA_HK, (head + 1) * GLA_HK)
        vc = slice(head * GLA_HV, (head + 1) * GLA_HV)
        z = jnp.dot(gk_ref[...], w2_ref[:, kc], preferred_element_type=F32) + bgk_ref[:, kc]
        log_a = blocks((jnp.minimum(z, 0.0) - jnp.log(1.0 + jnp.exp(-jnp.abs(z)))) * (1.0 / GLA_GATE_NORMALIZER))
        hi = log_a.astype(BF16)
        lo = (log_a - hi.astype(F32)).astype(BF16)
        b = (lax.dot_general(tril, hi, bnn, preferred_element_type=F32)
             + lax.dot_general(tril, lo, bnn, preferred_element_type=F32))
        b_last = b[:, blk - 1:blk, :]
        q = blocks(q_ref[:, kc].astype(F32)) * (GLA_HK ** -0.5)
        k = blocks(k_ref[:, kc].astype(F32))
        v = blocks(v_ref[:, vc])
        q_t = (q * jnp.exp(b)).astype(BF16)
        k_t = (k * jnp.exp(-b)).astype(BF16)
        k_dec = (k * jnp.exp(b_last - b)).astype(BF16)
        scores = lax.dot_general(q_t, k_t, bnt, preferred_element_type=F32)
        scores = jnp.where(causal[None], scores, 0.0).astype(BF16)
        o_intra = lax.dot_general(scores, v, bnn, preferred_element_type=F32)
        decay = jnp.exp(b_last)

        update = [lax.dot_general(k_dec[c], v[c], tn, preferred_element_type=F32) for c in range(n_blk)]
        decay_col = [jnp.sum(jnp.where(eye, jnp.broadcast_to(decay[c], (GLA_HK, GLA_HK)), 0.0),
                             axis=1, keepdims=True) for c in range(n_blk)]
        state = s_ref[0, head]
        starts = []
        for c in range(n_blk):
            starts.append(state.astype(BF16))
            state = decay_col[c] * state + update[c]
        s_ref[0, head] = state
        o = o_intra + jnp.stack([jnp.dot(q_t[c], starts[c], preferred_element_type=F32) for c in range(n_blk)])
        on = o * lax.rsqrt(jnp.mean(o * o, axis=-1, keepdims=True) + RMS_EPS) * gn_ref[...]
        gate = _silu(g_ref[:, vc].astype(F32)).reshape(o.shape)
        o_ref[:, vc] = (on * gate).reshape(o_ref.shape[0], GLA_HV).astype(o_ref.dtype)


def _gla_steps(batch, seq, n_heads):
    return batch * (GLA_HEADS // n_heads) * (seq // min(ROW_TILE, seq))


def _gla(proj, gk_low, w_gk2, b_gk, g_norm, s0, batch, seq, n_heads, jobs=()):
    blk = min(CHUNK, seq)
    rows = min(ROW_TILE, seq)
    n_steps = seq // rows
    n_groups = GLA_HEADS // n_heads
    m = batch * seq
    wk, wv = n_heads * GLA_HK, n_heads * GLA_HV
    k_off = GLA_DK // wk
    v_off = 2 * GLA_DK // wv
    g_off = (2 * GLA_DK + GLA_DV) // wv
    row = lambda b, h, c: b * n_steps + c
    state_spec = pl.BlockSpec((1, n_heads, GLA_HK, GLA_HV), lambda b, h, c: (b, h, 0, 0))
    grid_steps = _gla_steps(batch, seq, n_heads)
    job_in, job_out, job_shape = _job_specs(jobs, lambda b, h, c: (b * n_groups + h) * n_steps + c, grid_steps)
    return pl.pallas_call(
        functools.partial(_gla_kernel, blk=blk, n_blk=rows // blk, n_heads=n_heads, jobs=jobs, n_steps=grid_steps),
        grid=(batch, n_groups, n_steps),
        in_specs=[
            pl.BlockSpec((rows, wk), lambda b, h, c: (row(b, h, c), h)),
            pl.BlockSpec((rows, wk), lambda b, h, c: (row(b, h, c), k_off + h)),
            pl.BlockSpec((rows, wv), lambda b, h, c: (row(b, h, c), v_off + h)),
            pl.BlockSpec((rows, wv), lambda b, h, c: (row(b, h, c), g_off + h)),
            pl.BlockSpec((rows, GLA_GATE_RANK), lambda b, h, c: (row(b, h, c), 0)),
            pl.BlockSpec((GLA_GATE_RANK, wk), lambda b, h, c: (0, h)),
            pl.BlockSpec((1, wk), lambda b, h, c: (0, h)),
            pl.BlockSpec((1, GLA_HV), lambda b, h, c: (0, 0)),
            state_spec,
            *job_in,
        ],
        out_specs=[
            pl.BlockSpec((rows, wv), lambda b, h, c: (row(b, h, c), h)),
            state_spec,
            *job_out,
        ],
        out_shape=[
            jax.ShapeDtypeStruct((m, GLA_DV), BF16),
            jax.ShapeDtypeStruct(s0.shape, F32),
            *job_shape,
        ],
        compiler_params=_params("arbitrary", "arbitrary", "arbitrary"),
        name="gla",
    )(proj, proj, proj, proj, gk_low, w_gk2, b_gk, g_norm, s0, *(job.src for job in jobs))


def _band_attn_kernel(q_ref, kp_ref, kc_ref, vp_ref, vc_ref, rel_ref, *rest, jobs, n_steps):
    n_jobs = len(jobs)
    o_ref, bias_ref = rest[n_jobs], rest[-1]
    i = pl.program_id(1)
    _side_casts(pl.program_id(0) * pl.num_programs(1) + i, n_steps, jobs, rest[:n_jobs], rest[n_jobs + 1:-1])
    nt = (((1,), (1,)), ((), ()))
    n_keys = BAND_PAST + ATT_SUB
    n_sub = BAND_PAST // ATT_SUB

    @pl.when(i <= 1)
    def _():
        col = lax.broadcasted_iota(jnp.int32, (ATT_SUB, n_keys), 1)
        qc = lax.broadcasted_iota(jnp.int32, (ATT_SUB, n_keys), 0) // CHUNK
        kc = col // CHUNK
        in_band = (kc >= qc) & (kc <= qc + BAND_PAST // CHUNK)
        for h in range(ATT_HEADS_PER_STEP):
            rel = jnp.broadcast_to(rel_ref[h], (ATT_SUB, REL_LEN))
            toeplitz = pltpu.roll(rel, 0, 1, stride=1, stride_axis=0)[:, ATT_SUB:] * LOG2E
            for s in range(n_sub):
                key_pos = (i - 1) * BAND_PAST + s * ATT_SUB + col
                bias_ref[h, s] = jnp.where(in_band & (key_pos >= 0), toeplitz, NEG)

    units = [(h, s) for h in range(ATT_HEADS_PER_STEP) for s in range(n_sub)]

    def scores(h, s):
        lo, hi = s * ATT_SUB, (s + 1) * ATT_SUB
        cols = slice(h * ATT_HD, (h + 1) * ATT_HD)
        k = jnp.concatenate([kp_ref[lo:, cols], kc_ref[:hi, cols]], axis=0)
        return lax.dot_general(q_ref[lo:hi, cols], k, nt, preferred_element_type=F32) + bias_ref[h, s]

    def finish(h, s, sc):
        lo, hi = s * ATT_SUB, (s + 1) * ATT_SUB
        cols = slice(h * ATT_HD, (h + 1) * ATT_HD)
        v = jnp.concatenate([vp_ref[lo:, cols], vc_ref[:hi, cols]], axis=0)
        p = jnp.exp2(sc - jnp.max(sc, axis=-1, keepdims=True))
        denom = jnp.sum(p, axis=-1, keepdims=True)
        o = jnp.dot(p.astype(BF16), v, preferred_element_type=F32) / denom
        o_ref[lo:hi, cols] = o.astype(o_ref.dtype)

    sc = scores(*units[0])
    for n, unit in enumerate(units):
        sc_next = scores(*units[n + 1]) if n + 1 < len(units) else None
        finish(*unit, sc)
        sc = sc_next


def _band_rel(table):
    assert BAND_PAST + ATT_SUB - REL_LEN // 2 == MAX_REL and REL_LEN // 2 <= 2 * MAX_REL
    far = jnp.broadcast_to(table[:, 2 * MAX_REL:], (ATT_HEADS, REL_LEN // 2))
    near = table[:, 2 * MAX_REL - REL_LEN // 2 + 1:][:, ::-1]
    return jnp.concatenate([far, near], axis=1).astype(F32)[:, None, :]


def _band_attn(q, kv, table, jobs=()):
    t = q.shape[0]
    n_keys = BAND_PAST + ATT_SUB
    prev = lambda i: jnp.maximum(i - 1, 0)
    n_groups = ATT_HEADS // ATT_HEADS_PER_STEP
    blk = (BAND_PAST, ATT_HEADS_PER_STEP * ATT_HD)
    n_blocks = t // BAND_PAST
    job_in, job_out, job_shape = _job_specs(jobs, lambda h, i: h * n_blocks + i, n_groups * n_blocks)
    return pl.pallas_call(
        functools.partial(_band_attn_kernel, jobs=jobs, n_steps=n_groups * n_blocks),
        grid=(n_groups, n_blocks),
        in_specs=[
            pl.BlockSpec(blk, lambda h, i: (i, h)),
            pl.BlockSpec(blk, lambda h, i: (prev(i), h)),
            pl.BlockSpec(blk, lambda h, i: (i, h)),
            pl.BlockSpec(blk, lambda h, i: (prev(i), n_groups + h)),
            pl.BlockSpec(blk, lambda h, i: (i, n_groups + h)),
            pl.BlockSpec((ATT_HEADS_PER_STEP, 1, REL_LEN), lambda h, i: (h, 0, 0)),
            *job_in,
        ],
        out_specs=[pl.BlockSpec(blk, lambda h, i: (i, h)), *job_out],
        out_shape=[jax.ShapeDtypeStruct((t, D_MODEL), BF16), *job_shape],
        scratch_shapes=[pltpu.VMEM((ATT_HEADS_PER_STEP, BAND_PAST // ATT_SUB, ATT_SUB, n_keys), F32)],
        compiler_params=_params("arbitrary", "arbitrary"),
        name="band_attn",
    )(q, kv, kv, kv, kv, _band_rel(table), *(job.src for job in jobs))


def _step_attn_kernel(q_ref, kvn_ref, ck_ref, cv_ref, rp_ref, rn_ref, o_ref, bias_ref, bias_new_ref):
    nt = (((1,), (1,)), ((), ()))
    n_past = ck_ref.shape[1]
    n_keys = n_past * ATT_HEADS

    seq = q_ref.shape[0]

    @pl.when(pl.program_id(0) == 0)
    def _():
        def toeplitz(ref):
            rows = [pltpu.roll(jnp.broadcast_to(ref[h], (seq, ref.shape[-1])), 0, 1, stride=1, stride_axis=0)
                    for h in range(ATT_HEADS)]
            return jnp.concatenate(rows, axis=0)

        def spread(bias, col_of_key, head_of_key):
            pick = jnp.where(lax.broadcasted_iota(jnp.int32, col_of_key.shape, 0) == col_of_key, 1.0, 0.0).astype(BF16)
            hi = bias.astype(BF16)
            lo = (bias - hi.astype(F32)).astype(BF16)
            wide = jnp.dot(hi, pick, preferred_element_type=F32) + jnp.dot(lo, pick, preferred_element_type=F32)
            q_head = lax.broadcasted_iota(jnp.int32, wide.shape, 0) // seq
            return jnp.where(q_head == head_of_key[0:1, :], wide, NEG)

        key = lax.broadcasted_iota(jnp.int32, (n_past + LANE, n_keys), 1)
        bias_ref[...] = spread(toeplitz(rp_ref), LANE + key // ATT_HEADS, key % ATT_HEADS)
        key = lax.broadcasted_iota(jnp.int32, (LANE, ATT_HEADS * seq), 1)
        bias_new_ref[...] = spread(toeplitz(rn_ref), seq + key % seq, key // seq)

    heads = lambda ref, off: jnp.concatenate(
        [ref[:, off + h * ATT_HD:off + (h + 1) * ATT_HD] for h in range(ATT_HEADS)], axis=0)
    q = heads(q_ref, 0)
    k_new = heads(kvn_ref, 0).astype(BF16)
    v_new = heads(kvn_ref, D_MODEL).astype(BF16)
    k_past = ck_ref[0].reshape(n_keys, ATT_HD).astype(BF16)
    v_past = cv_ref[0].reshape(n_keys, ATT_HD).astype(BF16)
    sp = lax.dot_general(q, k_past, nt, preferred_element_type=F32) + bias_ref[...]
    sn = lax.dot_general(q, k_new, nt, preferred_element_type=F32) + bias_new_ref[...]
    mx = jnp.maximum(jnp.max(sp, axis=-1, keepdims=True), jnp.max(sn, axis=-1, keepdims=True))
    pp = jnp.exp2(sp - mx)
    pn = jnp.exp2(sn - mx)
    denom = jnp.sum(pp, axis=-1, keepdims=True) + jnp.sum(pn, axis=-1, keepdims=True)
    o = (jnp.dot(pp.astype(BF16), v_past, preferred_element_type=F32)
         + jnp.dot(pn.astype(BF16), v_new, preferred_element_type=F32)) / denom
    for h in range(ATT_HEADS):
        o_ref[:, h * ATT_HD:(h + 1) * ATT_HD] = o[h * seq:(h + 1) * seq, :].astype(o_ref.dtype)


def _step_attn(q, kv_new, cache_k, cache_v, table, batch, seq):
    n_past = cache_k.shape[1]
    rows = ATT_HEADS * seq
    assert n_past >= MAX_REL and seq <= LANE // 2
    n_far = n_past + LANE - MAX_REL + 1
    rel_past = jnp.concatenate([jnp.broadcast_to(table[:, 2 * MAX_REL:], (ATT_HEADS, n_far)),
                                table[:, 2 * MAX_REL - 1:MAX_REL:-1]], axis=1)
    rel_new = table[:, MAX_REL + seq:MAX_REL + seq - LANE:-1]
    rel_past, rel_new = (r.astype(F32)[:, None, :] * LOG2E for r in (rel_past, rel_new))
    return pl.pallas_call(
        _step_attn_kernel,
        grid=(batch,),
        in_specs=[
            pl.BlockSpec((seq, D_MODEL), lambda b: (b, 0)),
            pl.BlockSpec((seq, 2 * D_MODEL), lambda b: (b, 0)),
            pl.BlockSpec((1, n_past, ATT_HEADS, ATT_HD), lambda b: (b, 0, 0, 0)),
            pl.BlockSpec((1, n_past, ATT_HEADS, ATT_HD), lambda b: (b, 0, 0, 0)),
            pl.BlockSpec((ATT_HEADS, 1, n_past + LANE), lambda b: (0, 0, 0)),
            pl.BlockSpec((ATT_HEADS, 1, LANE), lambda b: (0, 0, 0)),
        ],
        out_specs=pl.BlockSpec((seq, D_MODEL), lambda b: (b, 0)),
        out_shape=jax.ShapeDtypeStruct((batch * seq, D_MODEL), BF16),
        scratch_shapes=[pltpu.VMEM((rows, n_past * ATT_HEADS), F32), pltpu.VMEM((rows, rows), F32)],
        compiler_params=_params("arbitrary"),
        name="step_attn",
    )(q, kv_new, cache_k, cache_v, rel_past, rel_new)


def _trunk(x, ada, ada_kv, row_block, gla_s0, past, wts):
    batch, seq, _ = x.shape
    m = batch * seq
    tm = min(ROW_TILE, m)
    x = x.reshape(m, D_MODEL)
    assert batch <= MOD_ROWS and (batch == 1 or m == tm)
    per_seq = None if batch == 1 else seq

    def mod(l, sub, which):
        return _Mod(ada, l, row_block, 3 * sub + which, per_seq)

    def ffn(x, l, sub, idx):
        return (x, mod(l, sub, 0), mod(l, sub, 1), mod(l, sub, 2)), (l, sub, idx)

    def side_jobs(ffn_keys, names, n_steps):
        ffn_keys = [key for key in ffn_keys if key not in wts["ffn"]]
        names = [name for name in names if name not in wts]
        jobs = tuple(job for key in ffn_keys for job in _ffn_cast_jobs(*wts["ffn_f32"], *key, n_steps))
        jobs += tuple(_CastJob(wts["f32"][name], 0, 0, D_MODEL // n_steps, (wts["f32"][name].shape[-1],))
                      for name in names)

        def keep(cast):
            for n, key in enumerate(ffn_keys):
                wts["ffn"][key] = tuple(cast[3 * n:3 * n + 3])
            for n, name in enumerate(names):
                wts[name] = cast[3 * len(ffn_keys) + n]
        return jobs, keep

    x = yield ffn(x, 0, 0, 0)
    proj, gk_low = _modmm(x, mod(0, 1, 0), mod(0, 1, 1), wts["a_in_t"], tm, BF16,
                          widths=(GLA_MAIN, GLA_GATE_RANK), w_is_t=True)
    gla_heads = GLA_HEADS_PER_STEP if batch == 1 else GLA_HEADS
    jobs, keep = side_jobs(((0, 1), (1, 0)), ("a_out",), _gla_steps(batch, seq, gla_heads))
    o, gla_state, *cast = _gla(proj, gk_low, wts["a_gk2"], wts["b_a_gk"], wts["g_a_norm"], gla_s0, batch, seq,
                               gla_heads, jobs)
    keep(cast)
    jobs, keep = side_jobs((), ("kv", "b_q", "b_out"), m // tm)
    x, *cast = _proj_ln(o, wts["a_out"], x, mod(0, 1, 2), wts["ln_g"][0, 1][None], wts["ln_b"][0, 1][None], tm, jobs)
    keep(cast)
    x = yield ffn(x, 0, 2, 1)

    kv_shift = _Mod(ada_kv, 0, row_block, 0, per_seq)
    kv_scale = _Mod(ada_kv, 0, row_block, 1, per_seq)
    if past is None:
        assert tm == min(BAND_PAST, seq)
        kv, kv_out = _modmm(x, kv_shift, kv_scale, wts["kv"], tm, BF16, tail_f32=True)
    else:
        kv_out = _modmm(x, kv_shift, kv_scale, wts["kv"], tm, F32)

    x = yield ffn(x, 1, 0, 0)
    q = _modmm(x, mod(1, 1, 0), mod(1, 1, 1), wts["b_q"], tm, BF16,
               out_scale=ATT_HD ** -0.5 * LOG2E)
    if past is None:
        jobs, keep = side_jobs(((1, 1),), (), (ATT_HEADS // ATT_HEADS_PER_STEP) * (seq // BAND_PAST))
        o, *cast = _band_attn(q, kv, wts["rel_bias"], jobs)
        keep(cast)
    else:
        o = _step_attn(q, kv_out, past[0], past[1], wts["rel_bias"], batch, seq)
    x, = _proj_ln(o, wts["b_out"], x, mod(1, 1, 2), wts["ln_g"][1, 1][None], wts["ln_b"][1, 1][None], tm)
    x = yield ffn(x, 1, 2, 1)

    n_rows = kv_out.shape[0] // batch
    k_out = kv_out[:, :D_MODEL].reshape(batch, n_rows, ATT_HEADS, ATT_HD)
    v_out = kv_out[:, D_MODEL:].reshape(batch, n_rows, ATT_HEADS, ATT_HD)
    return x.reshape(batch, seq, D_MODEL), gla_state[None], k_out, v_out


def kernel(x_prompt, x_sample, state_gla, cache_band_k, cache_band_v, c_prompt, c_sample, w_ada, b_ada, ln_g, ln_b, w_ffn_up, w_ffn_down, w_a_in, w_a_gk2, b_a_gk, g_a_norm, w_a_out, w_ada_kv, b_ada_kv, w_kv, w_b_q, rel_bias, w_b_out):
    assert DEPTH == 2 and w_a_in.shape[0] == 1 and w_b_q.shape[0] == 1
    n_prompt, n_sample = x_prompt.shape[0], x_sample.shape[0]
    assert n_prompt == 1 and n_sample <= MOD_ROWS

    lead = lambda w: w.reshape((1, 1) + w.shape[-2:])
    wts = {
        "ffn_f32": (w_ffn_up, w_ffn_down),
        "ffn": {(0, 0): (*_cast_split(w_ffn_up, (D_FF, D_FF), CAST_ROWS),
                         *_cast_split(w_ffn_down, (D_MODEL,), D_FF // 8))},
        "a_in_t": _cast_split(lead(jnp.swapaxes(w_a_in, 1, 2)), (D_MODEL,), A_IN_CAST_ROWS)[0],
        "a_gk2": w_a_gk2[0].astype(BF16),
        "b_a_gk": b_a_gk[0][None],
        "g_a_norm": g_a_norm[0][None],
        "f32": {"a_out": lead(w_a_out), "kv": lead(w_kv), "b_q": lead(w_b_q), "b_out": lead(w_b_out)},
        "rel_bias": rel_bias[0],
        "ln_g": ln_g,
        "ln_b": ln_b,
    }

    c_all = jnp.concatenate([jnp.pad(c, ((0, MOD_ROWS - c.shape[0]), (0, 0))) for c in (c_prompt, c_sample)], axis=0)
    ada = _ada(c_all, w_ada, b_ada)
    ada_kv = _ada(c_all, w_ada_kv[None], b_ada_kv[None])

    gla_zero = jnp.zeros((n_prompt, GLA_HEADS, GLA_HK, GLA_HV), state_gla.dtype)
    trunks = [_trunk(x_prompt, ada, ada_kv, 0, gla_zero, None, wts),
              _trunk(x_sample, ada, ada_kv, 1, state_gla[0], (cache_band_k, cache_band_v), wts)]
    requests = [next(trunk) for trunk in trunks]
    results = [None, None]
    while None in results:
        streams = [stream for stream, _ in requests]
        (l, sub, idx), = {key for _, key in requests}
        new_x = _ffn(streams, *wts["ffn"][l, idx], wts["ln_g"][l, sub][None], wts["ln_b"][l, sub][None],
                     FFN_ROW_TILE)
        for n, (trunk, x) in enumerate(zip(trunks, new_x)):
            try:
                requests[n] = trunk.send(x)
            except StopIteration as done:
                results[n] = done.value
    (y_p, s_p, k_p, v_p), (y_s, s_s, k_s, v_s) = results
    return (y_p, y_s, s_p, s_s, k_p, v_p, k_s, v_s)
```

```python
import functools
from typing import NamedTuple

import jax
import jax.numpy as jnp
from jax import lax
from jax.experimental import pallas as pl
from jax.experimental.pallas import tpu as pltpu

F32 = jnp.float32
BF16 = jnp.bfloat16

D_MODEL = 2048
DEPTH = 2
CHUNK = 64
GLA_HEADS = 4
GLA_DK = D_MODEL // 2
GLA_DV = D_MODEL
GLA_HK = GLA_DK // GLA_HEADS
GLA_HV = GLA_DV // GLA_HEADS
GLA_GATE_RANK = 16
GLA_GATE_NORMALIZER = 16.0
GLA_MAIN = 2 * GLA_DK + 2 * GLA_DV
ATT_HEADS = 16
ATT_HD = D_MODEL // ATT_HEADS
BAND_PAST = 8 * CHUNK
MAX_REL = 256
D_FF = 5504
ALPHA = (2 * DEPTH) ** 0.25
LN_EPS = 1e-5
RMS_EPS = 1e-6
NEG = -1e30
LOG2E = 1.4426950408889634

LANE = 128
FF_TILE = 768
FF_LAST = D_FF - (D_FF - 1) // FF_TILE * FF_TILE
ROW_TILE = 512
FFN_ROW_TILE = 1024
A_IN_CAST_ROWS = 560
CAST_ROWS = 256
FFN_CAST_DOWN_ROWS = 128
GLA_HEADS_PER_STEP = 2
PROJ_SUB = 128
MODMM_COLS = 1024
ATT_SUB = 256
ATT_HEADS_PER_STEP = 4
REL_LEN = BAND_PAST + 2 * ATT_SUB
MOD_ROWS = 8
ADA_ROWS = 2 * MOD_ROWS
ADA_TILE = 2048
VMEM_LIMIT = 56 * 1024 * 1024
FFN_VMEM_LIMIT = 60 * 1024 * 1024


def _params(*sem):
    return pltpu.CompilerParams(dimension_semantics=sem, vmem_limit_bytes=VMEM_LIMIT)


def _layer_norm(z, g, b):
    mu = jnp.mean(z, axis=-1, keepdims=True)
    zc = z - mu
    var = jnp.mean(zc * zc, axis=-1, keepdims=True)
    return zc * lax.rsqrt(var + LN_EPS) * g + b


def _silu(a):
    half = 0.5 * a
    return half + half * jnp.tanh(half)


def _ada_kernel(c_ref, w_ref, b_ref, o_ref):
    o_ref[...] = jnp.dot(_silu(c_ref[...]), w_ref[...], preferred_element_type=F32) + b_ref[...]


def _ada(c_all, w, b):
    n_l, _, n = w.shape
    return pl.pallas_call(
        _ada_kernel,
        grid=(n_l, n // ADA_TILE),
        in_specs=[
            pl.BlockSpec((ADA_ROWS, D_MODEL), lambda l, j: (0, 0)),
            pl.BlockSpec((None, D_MODEL, ADA_TILE), lambda l, j: (l, 0, j)),
            pl.BlockSpec((None, 1, ADA_TILE), lambda l, j: (l, 0, j)),
        ],
        out_specs=pl.BlockSpec((None, ADA_ROWS, ADA_TILE), lambda l, j: (l, 0, j)),
        out_shape=jax.ShapeDtypeStruct((n_l, ADA_ROWS, n), F32),
        compiler_params=_params("arbitrary", "arbitrary"),
        name="ada",
    )(c_all, w, b.reshape(n_l, 1, n))


class _Mod(NamedTuple):
    table: jax.Array
    layer: int
    row_block: int
    col: int
    seq: int | None

    @property
    def spec(self):
        return pl.BlockSpec((None, MOD_ROWS, D_MODEL), lambda *_: (self.layer, self.row_block, self.col))


def _mod_rows(ref, n_rows, seq):
    if seq is None:
        return ref[0:1, :]
    n_seq = n_rows // seq
    return jnp.broadcast_to(ref[0:n_seq, :][:, None, :], (n_seq, seq, D_MODEL)).reshape(n_rows, D_MODEL)


def _ffn_kernel(*refs, seqs):
    n = len(seqs)
    wa_ref, wu_ref, wd_ref, g_ref, b_ref = refs[4 * n:4 * n + 5]
    o_refs = refs[4 * n + 5:5 * n + 5]
    h_refs = refs[5 * n + 5:6 * n + 5]
    x_buf, x_sem = refs[6 * n + 5:]
    i, j = pl.program_id(0), pl.program_id(1)
    last = pl.num_programs(1) - 1
    tm = x_buf.shape[0]

    def fetch(tile):
        return pltpu.make_async_copy(refs[0].at[pl.ds(tile * tm, tm), :], x_buf, x_sem)

    @pl.when((i == 0) & (j == 0))
    def _():
        fetch(0).start()

    @pl.when(j == 0)
    def _():
        fetch(i).wait()

    @pl.when((j == 1) & (i + 1 < pl.num_programs(0)))
    def _():
        fetch(i + 1).start()

    def stream(x_ref, sh_ref, sc_ref, gt_ref, o_ref, h_ref, seq, active):
        mod = lambda ref: _mod_rows(ref, x_ref.shape[0], seq)

        def chunk(cols):
            h = h_ref[...]
            a = jnp.dot(h, wa_ref[:, :cols], preferred_element_type=F32)
            u = jnp.dot(h, wu_ref[:, :cols], preferred_element_type=F32)
            y = jnp.dot((_silu(a) * u).astype(BF16), wd_ref[:cols, :], preferred_element_type=F32)
            o_ref[...] += (0.5 * (1.0 + mod(gt_ref))) * y

        @pl.when(active & (j == 0))
        def _():
            x = x_ref[...]
            h_ref[...] = (x * (1.0 + mod(sc_ref)) + mod(sh_ref)).astype(BF16)
            o_ref[...] = ALPHA * x
            chunk(FF_TILE)

        @pl.when(active & (j > 0) & (j < last))
        def _():
            chunk(FF_TILE)

        @pl.when(active & (j == last))
        def _():
            chunk(FF_LAST)
            o_ref[...] = _layer_norm(o_ref[...], g_ref[...], b_ref[...])

    for s in range(n):
        active = True if s == 0 else (i == pl.num_programs(0) - 1)
        stream(x_buf if s == 0 else refs[4 * s], *refs[4 * s + 1:4 * s + 4], o_refs[s], h_refs[s], seqs[s], active)


def _ffn(streams, w_a, w_u, w_down, ln_g, ln_b, tm):
    m = streams[0][0].shape[0]
    n_ff = pl.cdiv(D_FF, FF_TILE)
    assert n_ff >= 2
    vec = pl.BlockSpec((1, D_MODEL), lambda i, j: (0, 0))
    up = pl.BlockSpec((D_MODEL, FF_TILE), lambda i, j: (0, j))
    o_specs = [pl.BlockSpec((tm, D_MODEL), lambda i, j: (i, 0))]
    o_specs += [pl.BlockSpec(x.shape, lambda i, j: (0, 0)) for x, *_ in streams[1:]]
    x_specs = [pl.BlockSpec(memory_space=pl.ANY)] + o_specs[1:]
    in_specs, args = [], []
    for x_spec, (x, shift, scale, gate) in zip(x_specs, streams):
        in_specs += [x_spec, shift.spec, scale.spec, gate.spec]
        args += [x, shift.table, scale.table, gate.table]
    return pl.pallas_call(
        functools.partial(_ffn_kernel, seqs=tuple(shift.seq for _, shift, *_ in streams)),
        grid=(m // tm, n_ff),
        in_specs=[
            *in_specs,
            up, up,
            pl.BlockSpec((FF_TILE, D_MODEL), lambda i, j: (j, 0)),
            vec, vec,
        ],
        out_specs=o_specs,
        out_shape=[jax.ShapeDtypeStruct(x.shape, F32) for x, *_ in streams],
        scratch_shapes=[*(pltpu.VMEM(spec.block_shape, BF16) for spec in o_specs),
                        pltpu.VMEM((tm, D_MODEL), F32), pltpu.SemaphoreType.DMA(())],
        compiler_params=pltpu.CompilerParams(dimension_semantics=("arbitrary", "arbitrary"),
                                             vmem_limit_bytes=FFN_VMEM_LIMIT),
        name="ffn",
    )(*args, w_a, w_u, w_down, ln_g, ln_b)


def _cast_kernel(x_ref, *o_refs):
    off = 0
    for o_ref in o_refs:
        n = o_ref.shape[-1]
        o_ref[...] = x_ref[:, off:off + n].astype(o_ref.dtype)
        off += n


class _CastJob(NamedTuple):
    src: jax.Array
    layer: int
    idx: int
    rows: int
    widths: tuple

    @property
    def n_blocks(self):
        return self.src.shape[2] // self.rows

    def specs(self, step_of):
        r, n = self.src.shape[2:]
        assert r % self.rows == 0 and sum(self.widths) <= n and all(wd % LANE == 0 for wd in self.widths)
        blk = lambda *ids: jnp.minimum(step_of(*ids), self.n_blocks - 1)
        in_spec = pl.BlockSpec((None, None, self.rows, n), lambda *ids: (self.layer, self.idx, blk(*ids), 0))
        out_specs = [pl.BlockSpec((self.rows, wd), lambda *ids: (blk(*ids), 0)) for wd in self.widths]
        out_shape = [jax.ShapeDtypeStruct((r, wd), BF16) for wd in self.widths]
        return in_spec, out_specs, out_shape


def _side_casts(step, n_steps, jobs, in_refs, out_refs):
    pos = 0
    for job, x_ref in zip(jobs, in_refs):
        outs = out_refs[pos:pos + len(job.widths)]
        pos += len(job.widths)
        if job.n_blocks == n_steps:
            _cast_kernel(x_ref, *outs)
        else:
            pl.when(step < job.n_blocks)(functools.partial(_cast_kernel, x_ref, *outs))


def _job_specs(jobs, step_of, n_steps):
    in_specs, out_specs, out_shape = [], [], []
    for job in jobs:
        assert job.n_blocks <= n_steps
        i, o, s = job.specs(step_of)
        in_specs.append(i)
        out_specs += o
        out_shape += s
    return in_specs, out_specs, out_shape


def _cast_split(w, widths, rows, layer=0, idx=0):
    job = _CastJob(w, layer, idx, rows, tuple(widths))
    in_spec, out_specs, out_shape = job.specs(lambda i: i)
    return pl.pallas_call(
        _cast_kernel,
        grid=(job.n_blocks,),
        in_specs=[in_spec],
        out_specs=out_specs,
        out_shape=out_shape,
        compiler_params=_params("parallel"),
        name="cast",
    )(w)


def _ffn_cast_jobs(w_ffn_up, w_ffn_down, layer, idx, n_steps):
    return (_CastJob(w_ffn_up, layer, idx, D_MODEL // n_steps, (D_FF, D_FF)),
            _CastJob(w_ffn_down, layer, idx, FFN_CAST_DOWN_ROWS, (D_MODEL,)))


def _modmm_kernel(x_ref, sh_ref, sc_ref, w_ref, o_ref, *rest, seq, w_is_t, out_scale, tail_f32):
    h_ref = rest[-1]
    mod = lambda ref: _mod_rows(ref, x_ref.shape[0], seq)
    h_ref[...] = (x_ref[...] * (1.0 + mod(sc_ref)) + mod(sh_ref)).astype(BF16)
    h = h_ref[...]
    nt = (((1,), (1,)), ((), ()))

    def project(lo, hi):
        if w_is_t:
            y = lax.dot_general(h, w_ref[lo:hi, :], nt, preferred_element_type=F32)
        else:
            y = jnp.dot(h, w_ref[:, lo:hi], preferred_element_type=F32)
        return y if out_scale is None else y * out_scale

    n = o_ref.shape[1]
    if tail_f32:
        tail_ref, = rest[:-1]
        for lo in range(0, n, MODMM_COLS):
            y = project(lo, lo + MODMM_COLS)
            o_ref[:, lo:lo + MODMM_COLS] = y.astype(o_ref.dtype)
            tail_ref[:, lo:lo + MODMM_COLS] = y
        return
    for lo in range(0, n, MODMM_COLS):
        o_ref[:, lo:lo + MODMM_COLS] = project(lo, lo + MODMM_COLS).astype(o_ref.dtype)
    for extra_ref in rest[:-1]:
        extra_ref[...] = project(n, n + extra_ref.shape[1]).astype(extra_ref.dtype)
        n += extra_ref.shape[1]


def _modmm(x, shift, scale, w, tm, out_dtype, widths=None, w_is_t=False, out_scale=None, tail_f32=False):
    m = x.shape[0]
    widths = widths or (w.shape[0] if w_is_t else w.shape[1],)
    assert widths[0] % MODMM_COLS == 0 and not (tail_f32 and len(widths) > 1)
    out_specs = [pl.BlockSpec((tm, wd), lambda i: (i, 0)) for wd in widths]
    out_shape = [jax.ShapeDtypeStruct((m, wd), out_dtype) for wd in widths]
    if tail_f32:
        out_specs.append(pl.BlockSpec((tm, widths[0]), lambda i: (0, 0)))
        out_shape.append(jax.ShapeDtypeStruct((tm, widths[0]), F32))
    out = pl.pallas_call(
        functools.partial(_modmm_kernel, seq=shift.seq, w_is_t=w_is_t, out_scale=out_scale, tail_f32=tail_f32),
        grid=(m // tm,),
        in_specs=[
            pl.BlockSpec((tm, D_MODEL), lambda i: (i, 0)),
            shift.spec, scale.spec,
            pl.BlockSpec(w.shape, lambda i: (0, 0), pipeline_mode=pl.Buffered(1)),
        ],
        out_specs=out_specs,
        out_shape=out_shape,
        scratch_shapes=[pltpu.VMEM((tm, D_MODEL), BF16)],
        compiler_params=_params("arbitrary"),
        name="modmm",
    )(x, shift.table, scale.table, w)
    return out if len(out) > 1 else out[0]


def _proj_ln_kernel(a_ref, w_ref, x_ref, gt_ref, g_ref, b_ref, *rest, seq, jobs, n_steps):
    n_jobs = len(jobs)
    o_ref = rest[n_jobs]
    _side_casts(pl.program_id(0), n_steps, jobs, rest[:n_jobs], rest[n_jobs + 1:])
    sub = min(PROJ_SUB, a_ref.shape[0])
    n_sub = a_ref.shape[0] // sub
    dot = lambda r: jnp.dot(a_ref[r * sub:(r + 1) * sub, :], w_ref[...], preferred_element_type=F32)
    gate = _mod_rows(gt_ref, a_ref.shape[0], seq)
    y = dot(0)
    for r in range(n_sub):
        y_next = dot(r + 1) if r + 1 < n_sub else None
        rows = slice(r * sub, (r + 1) * sub)
        z = ALPHA * x_ref[rows, :] + (1.0 + gate[rows if gate.shape[0] > 1 else slice(None), :]) * y
        o_ref[rows, :] = _layer_norm(z, g_ref[...], b_ref[...])
        y = y_next


def _proj_ln(a, w, x, gate, ln_g, ln_b, tm, jobs=()):
    m = x.shape[0]
    vec = pl.BlockSpec((1, D_MODEL), lambda i: (0, 0))
    job_in, job_out, job_shape = _job_specs(jobs, lambda i: i, m // tm)
    return pl.pallas_call(
        functools.partial(_proj_ln_kernel, seq=gate.seq, jobs=jobs, n_steps=m // tm),
        grid=(m // tm,),
        in_specs=[
            pl.BlockSpec((tm, D_MODEL), lambda i: (i, 0)),
            pl.BlockSpec((D_MODEL, D_MODEL), lambda i: (0, 0)),
            pl.BlockSpec((tm, D_MODEL), lambda i: (i, 0)),
            gate.spec,
            vec, vec,
            *job_in,
        ],
        out_specs=[pl.BlockSpec((tm, D_MODEL), lambda i: (i, 0)), *job_out],
        out_shape=[jax.ShapeDtypeStruct((m, D_MODEL), F32), *job_shape],
        compiler_params=_params("arbitrary"),
        name="proj_ln",
    )(a, w, x, gate.table, ln_g, ln_b, *(job.src for job in jobs))


def _gla_kernel(q_ref, k_ref, v_ref, g_ref, gk_ref, w2_ref, bgk_ref, gn_ref, s0_ref, *rest,
                blk, n_blk, n_heads, jobs, n_steps):
    n_jobs = len(jobs)
    o_ref, s_ref = rest[n_jobs:n_jobs + 2]
    step = (pl.program_id(0) * pl.num_programs(1) + pl.program_id(1)) * pl.num_programs(2) + pl.program_id(2)
    _side_casts(step, n_steps, jobs, rest[:n_jobs], rest[n_jobs + 2:])

    @pl.when(pl.program_id(2) == 0)
    def _():
        s_ref[...] = s0_ref[...]

    causal = (lax.broadcasted_iota(jnp.int32, (blk, blk), 0)
              >= lax.broadcasted_iota(jnp.int32, (blk, blk), 1))
    tril = jnp.broadcast_to(jnp.where(causal, 1.0, 0.0).astype(BF16), (n_blk, blk, blk))
    eye = (lax.broadcasted_iota(jnp.int32, (GLA_HK, GLA_HK), 0)
           == lax.broadcasted_iota(jnp.int32, (GLA_HK, GLA_HK), 1))
    tn = (((0,), (0,)), ((), ()))
    bnn = (((2,), (1,)), ((0,), (0,)))
    bnt = (((2,), (2,)), ((0,), (0,)))

    def blocks(t):
        return t.reshape(n_blk, blk, t.shape[-1])

    for head in range(n_heads):
        kc = slice(head * GLA_HK, (head + 1) * GLA_HK)
        vc = slice(head * GLA_HV, (head + 1) * GLA_HV)
        z = jnp.dot(gk_ref[...], w2_ref[:, kc], preferred_element_type=F32) + bgk_ref[:, kc]
        log_a = blocks((jnp.minimum(z, 0.0) - jnp.log(1.0 + jnp.exp(-jnp.abs(z)))) * (1.0 / GLA_GATE_NORMALIZER))
        hi = log_a.astype(BF16)
        lo = (log_a - hi.astype(F32)).astype(BF16)
        b = (lax.dot_general(tril, hi, bnn, preferred_element_type=F32)
             + lax.dot_general(tril, lo, bnn, preferred_element_type=F32))
        b_last = b[:, blk - 1:blk, :]
        q = blocks(q_ref[:, kc].astype(F32)) * (GLA_HK ** -0.5)
        k = blocks(k_ref[:, kc].astype(F32))
        v = blocks(v_ref[:, vc])
        q_t = (q * jnp.exp(b)).astype(BF16)
        k_t = (k * jnp.exp(-b)).astype(BF16)
        k_dec = (k * jnp.exp(b_last - b)).astype(BF16)
        scores = lax.dot_general(q_t, k_t, bnt, preferred_element_type=F32)
        scores = jnp.where(causal[None], scores, 0.0).astype(BF16)
        o_intra = lax.dot_general(scores, v, bnn, preferred_element_type=F32)
        decay = jnp.exp(b_last)

        update = [lax.dot_general(k_dec[c], v[c], tn, preferred_element_type=F32) for c in range(n_blk)]
        decay_col = [jnp.sum(jnp.where(eye, jnp.broadcast_to(decay[c], (GLA_HK, GLA_HK)), 0.0),
                             axis=1, keepdims=True) for c in range(n_blk)]
        state = s_ref[0, head]
        starts = []
        for c in range(n_blk):
            starts.append(state.astype(BF16))
            state = decay_col[c] * state + update[c]
        s_ref[0, head] = state
        o = o_intra + jnp.stack([jnp.dot(q_t[c], starts[c], preferred_element_type=F32) for c in range(n_blk)])
        on = o * lax.rsqrt(jnp.mean(o * o, axis=-1, keepdims=True) + RMS_EPS) * gn_ref[...]
        gate = _silu(g_ref[:, vc].astype(F32)).reshape(o.shape)
        o_ref[:, vc] = (on * gate).reshape(o_ref.shape[0], GLA_HV).astype(o_ref.dtype)


def _gla_steps(batch, seq, n_heads):
    return batch * (GLA_HEADS // n_heads) * (seq // min(ROW_TILE, seq))


def _gla(proj, gk_low, w_gk2, b_gk, g_norm, s0, batch, seq, n_heads, jobs=()):
    blk = min(CHUNK, seq)
    rows = min(ROW_TILE, seq)
    n_steps = seq // rows
    n_groups = GLA_HEADS // n_heads
    m = batch * seq
    wk, wv = n_heads * GLA_HK, n_heads * GLA_HV
    k_off = GLA_DK // wk
    v_off = 2 * GLA_DK // wv
    g_off = (2 * GLA_DK + GLA_DV) // wv
    row = lambda b, h, c: b * n_steps + c
    state_spec = pl.BlockSpec((1, n_heads, GLA_HK, GLA_HV), lambda b, h, c: (b, h, 0, 0))
    grid_steps = _gla_steps(batch, seq, n_heads)
    job_in, job_out, job_shape = _job_specs(jobs, lambda b, h, c: (b * n_groups + h) * n_steps + c, grid_steps)
    return pl.pallas_call(
        functools.partial(_gla_kernel, blk=blk, n_blk=rows // blk, n_heads=n_heads, jobs=jobs, n_steps=grid_steps),
        grid=(batch, n_groups, n_steps),
        in_specs=[
            pl.BlockSpec((rows, wk), lambda b, h, c: (row(b, h, c), h)),
            pl.BlockSpec((rows, wk), lambda b, h, c: (row(b, h, c), k_off + h)),
            pl.BlockSpec((rows, wv), lambda b, h, c: (row(b, h, c), v_off + h)),
            pl.BlockSpec((rows, wv), lambda b, h, c: (row(b, h, c), g_off + h)),
            pl.BlockSpec((rows, GLA_GATE_RANK), lambda b, h, c: (row(b, h, c), 0)),
            pl.BlockSpec((GLA_GATE_RANK, wk), lambda b, h, c: (0, h)),
            pl.BlockSpec((1, wk), lambda b, h, c: (0, h)),
            pl.BlockSpec((1, GLA_HV), lambda b, h, c: (0, 0)),
            state_spec,
            *job_in,
        ],
        out_specs=[
            pl.BlockSpec((rows, wv), lambda b, h, c: (row(b, h, c), h)),
            state_spec,
            *job_out,
        ],
        out_shape=[
            jax.ShapeDtypeStruct((m, GLA_DV), BF16),
            jax.ShapeDtypeStruct(s0.shape, F32),
            *job_shape,
        ],
        compiler_params=_params("arbitrary", "arbitrary", "arbitrary"),
        name="gla",
    )(proj, proj, proj, proj, gk_low, w_gk2, b_gk, g_norm, s0, *(job.src for job in jobs))


def _band_attn_kernel(q_ref, kp_ref, kc_ref, vp_ref, vc_ref, rel_ref, *rest, jobs, n_steps):
    n_jobs = len(jobs)
    o_ref, bias_ref = rest[n_jobs], rest[-1]
    i = pl.program_id(1)
    _side_casts(pl.program_id(0) * pl.num_programs(1) + i, n_steps, jobs, rest[:n_jobs], rest[n_jobs + 1:-1])
    nt = (((1,), (1,)), ((), ()))
    n_keys = BAND_PAST + ATT_SUB
    n_sub = BAND_PAST // ATT_SUB

    @pl.when(i <= 1)
    def _():
        col = lax.broadcasted_iota(jnp.int32, (ATT_SUB, n_keys), 1)
        qc = lax.broadcasted_iota(jnp.int32, (ATT_SUB, n_keys), 0) // CHUNK
        kc = col // CHUNK
        in_band = (kc >= qc) & (kc <= qc + BAND_PAST // CHUNK)
        for h in range(ATT_HEADS_PER_STEP):
            rel = jnp.broadcast_to(rel_ref[h], (ATT_SUB, REL_LEN))
            toeplitz = pltpu.roll(rel, 0, 1, stride=1, stride_axis=0)[:, ATT_SUB:] * LOG2E
            for s in range(n_sub):
                key_pos = (i - 1) * BAND_PAST + s * ATT_SUB + col
                bias_ref[h, s] = jnp.where(in_band & (key_pos >= 0), toeplitz, NEG)

    units = [(h, s) for h in range(ATT_HEADS_PER_STEP) for s in range(n_sub)]

    def scores(h, s):
        lo, hi = s * ATT_SUB, (s + 1) * ATT_SUB
        cols = slice(h * ATT_HD, (h + 1) * ATT_HD)
        k = jnp.concatenate([kp_ref[lo:, cols], kc_ref[:hi, cols]], axis=0)
        return lax.dot_general(q_ref[lo:hi, cols], k, nt, preferred_element_type=F32) + bias_ref[h, s]

    def finish(h, s, sc):
        lo, hi = s * ATT_SUB, (s + 1) * ATT_SUB
        cols = slice(h * ATT_HD, (h + 1) * ATT_HD)
        v = jnp.concatenate([vp_ref[lo:, cols], vc_ref[:hi, cols]], axis=0)
        p = jnp.exp2(sc - jnp.max(sc, axis=-1, keepdims=True))
        denom = jnp.sum(p, axis=-1, keepdims=True)
        o = jnp.dot(p.astype(BF16), v, preferred_element_type=F32) / denom
        o_ref[lo:hi, cols] = o.astype(o_ref.dtype)

    sc = scores(*units[0])
    for n, unit in enumerate(units):
        sc_next = scores(*units[n + 1]) if n + 1 < len(units) else None
        finish(*unit, sc)
        sc = sc_next


def _band_rel(table):
    assert BAND_PAST + ATT_SUB - REL_LEN // 2 == MAX_REL and REL_LEN // 2 <= 2 * MAX_REL
    far = jnp.broadcast_to(table[:, 2 * MAX_REL:], (ATT_HEADS, REL_LEN // 2))
    near = table[:, 2 * MAX_REL - REL_LEN // 2 + 1:][:, ::-1]
    return jnp.concatenate([far, near], axis=1).astype(F32)[:, None, :]


def _band_attn(q, kv, table, jobs=()):
    t = q.shape[0]
    n_keys = BAND_PAST + ATT_SUB
    prev = lambda i: jnp.maximum(i - 1, 0)
    n_groups = ATT_HEADS // ATT_HEADS_PER_STEP
    blk = (BAND_PAST, ATT_HEADS_PER_STEP * ATT_HD)
    n_blocks = t // BAND_PAST
    job_in, job_out, job_shape = _job_specs(jobs, lambda h, i: h * n_blocks + i, n_groups * n_blocks)
    return pl.pallas_call(
        functools.partial(_band_attn_kernel, jobs=jobs, n_steps=n_groups * n_blocks),
        grid=(n_groups, n_blocks),
        in_specs=[
            pl.BlockSpec(blk, lambda h, i: (i, h)),
            pl.BlockSpec(blk, lambda h, i: (prev(i), h)),
            pl.BlockSpec(blk, lambda h, i: (i, h)),
            pl.BlockSpec(blk, lambda h, i: (prev(i), n_groups + h)),
            pl.BlockSpec(blk, lambda h, i: (i, n_groups + h)),
            pl.BlockSpec((ATT_HEADS_PER_STEP, 1, REL_LEN), lambda h, i: (h, 0, 0)),
            *job_in,
        ],
        out_specs=[pl.BlockSpec(blk, lambda h, i: (i, h)), *job_out],
        out_shape=[jax.ShapeDtypeStruct((t, D_MODEL), BF16), *job_shape],
        scratch_shapes=[pltpu.VMEM((ATT_HEADS_PER_STEP, BAND_PAST // ATT_SUB, ATT_SUB, n_keys), F32)],
        compiler_params=_params("arbitrary", "arbitrary"),
        name="band_attn",
    )(q, kv, kv, kv, kv, _band_rel(table), *(job.src for job in jobs))


def _step_attn_kernel(q_ref, kvn_ref, ck_ref, cv_ref, rp_ref, rn_ref, o_ref, bias_ref, bias_new_ref):
    nt = (((1,), (1,)), ((), ()))
    n_past = ck_ref.shape[1]
    n_keys = n_past * ATT_HEADS

    seq = q_ref.shape[0]

    @pl.when(pl.program_id(0) == 0)
    def _():
        def toeplitz(ref):
            rows = [pltpu.roll(jnp.broadcast_to(ref[h], (seq, ref.shape[-1])), 0, 1, stride=1, stride_axis=0)
                    for h in range(ATT_HEADS)]
            return jnp.concatenate(rows, axis=0)

        def spread(bias, col_of_key, head_of_key):
            pick = jnp.where(lax.broadcasted_iota(jnp.int32, col_of_key.shape, 0) == col_of_key, 1.0, 0.0).astype(BF16)
            hi = bias.astype(BF16)
            lo = (bias - hi.astype(F32)).astype(BF16)
            wide = jnp.dot(hi, pick, preferred_element_type=F32) + jnp.dot(lo, pick, preferred_element_type=F32)
            q_head = lax.broadcasted_iota(jnp.int32, wide.shape, 0) // seq
            return jnp.where(q_head == head_of_key[0:1, :], wide, NEG)

        key = lax.broadcasted_iota(jnp.int32, (n_past + LANE, n_keys), 1)
        bias_ref[...] = spread(toeplitz(rp_ref), LANE + key // ATT_HEADS, key % ATT_HEADS)
        key = lax.broadcasted_iota(jnp.int32, (LANE, ATT_HEADS * seq), 1)
        bias_new_ref[...] = spread(toeplitz(rn_ref), seq + key % seq, key // seq)

    heads = lambda ref, off: jnp.concatenate(
        [ref[:, off + h * ATT_HD:off + (h + 1) * ATT_HD] for h in range(ATT_HEADS)], axis=0)
    q = heads(q_ref, 0)
    k_new = heads(kvn_ref, 0).astype(BF16)
    v_new = heads(kvn_ref, D_MODEL).astype(BF16)
    k_past = ck_ref[0].reshape(n_keys, ATT_HD).astype(BF16)
    v_past = cv_ref[0].reshape(n_keys, ATT_HD).astype(BF16)
    sp = lax.dot_general(q, k_past, nt, preferred_element_type=F32) + bias_ref[...]
    sn = lax.dot_general(q, k_new, nt, preferred_element_type=F32) + bias_new_ref[...]
    mx = jnp.maximum(jnp.max(sp, axis=-1, keepdims=True), jnp.max(sn, axis=-1, keepdims=True))
    pp = jnp.exp2(sp - mx)
    pn = jnp.exp2(sn - mx)
    denom = jnp.sum(pp, axis=-1, keepdims=True) + jnp.sum(pn, axis=-1, keepdims=True)
    o = (jnp.dot(pp.astype(BF16), v_past, preferred_element_type=F32)
         + jnp.dot(pn.astype(BF16), v_new, preferred_element_type=F32)) / denom
    for h in range(ATT_HEADS):
        o_ref[:, h * ATT_HD:(h + 1) * ATT_HD] = o[h * seq:(h + 1) * seq, :].astype(o_ref.dtype)


def _step_attn(q, kv_new, cache_k, cache_v, table, batch, seq):
    n_past = cache_k.shape[1]
    rows = ATT_HEADS * seq
    assert n_past >= MAX_REL and seq <= LANE // 2
    n_far = n_past + LANE - MAX_REL + 1
    rel_past = jnp.concatenate([jnp.broadcast_to(table[:, 2 * MAX_REL:], (ATT_HEADS, n_far)),
                                table[:, 2 * MAX_REL - 1:MAX_REL:-1]], axis=1)
    rel_new = table[:, MAX_REL + seq:MAX_REL + seq - LANE:-1]
    rel_past, rel_new = (r.astype(F32)[:, None, :] * LOG2E for r in (rel_past, rel_new))
    return pl.pallas_call(
        _step_attn_kernel,
        grid=(batch,),
        in_specs=[
            pl.BlockSpec((seq, D_MODEL), lambda b: (b, 0)),
            pl.BlockSpec((seq, 2 * D_MODEL), lambda b: (b, 0)),
            pl.BlockSpec((1, n_past, ATT_HEADS, ATT_HD), lambda b: (b, 0, 0, 0)),
            pl.BlockSpec((1, n_past, ATT_HEADS, ATT_HD), lambda b: (b, 0, 0, 0)),
            pl.BlockSpec((ATT_HEADS, 1, n_past + LANE), lambda b: (0, 0, 0)),
            pl.BlockSpec((ATT_HEADS, 1, LANE), lambda b: (0, 0, 0)),
        ],
        out_specs=pl.BlockSpec((seq, D_MODEL), lambda b: (b, 0)),
        out_shape=jax.ShapeDtypeStruct((batch * seq, D_MODEL), BF16),
        scratch_shapes=[pltpu.VMEM((rows, n_past * ATT_HEADS), F32), pltpu.VMEM((rows, rows), F32)],
        compiler_params=_params("arbitrary"),
        name="step_attn",
    )(q, kv_new, cache_k, cache_v, rel_past, rel_new)


def _trunk(x, ada, ada_kv, row_block, gla_s0, past, wts):
    batch, seq, _ = x.shape
    m = batch * seq
    tm = min(ROW_TILE, m)
    x = x.reshape(m, D_MODEL)
    assert batch <= MOD_ROWS and (batch == 1 or m == tm)
    per_seq = None if batch == 1 else seq

    def mod(l, sub, which):
        return _Mod(ada, l, row_block, 3 * sub + which, per_seq)

    def ffn(x, l, sub, idx):
        return (x, mod(l, sub, 0), mod(l, sub, 1), mod(l, sub, 2)), (l, sub, idx)

    def side_jobs(ffn_keys, names, n_steps):
        ffn_keys = [key for key in ffn_keys if key not in wts["ffn"]]
        names = [name for name in names if name not in wts]
        jobs = tuple(job for key in ffn_keys for job in _ffn_cast_jobs(*wts["ffn_f32"], *key, n_steps))
        jobs += tuple(_CastJob(wts["f32"][name], 0, 0, D_MODEL // n_steps, (wts["f32"][name].shape[-1],))
                      for name in names)

        def keep(cast):
            for n, key in enumerate(ffn_keys):
                wts["ffn"][key] = tuple(cast[3 * n:3 * n + 3])
            for n, name in enumerate(names):
                wts[name] = cast[3 * len(ffn_keys) + n]
        return jobs, keep

    x = yield ffn(x, 0, 0, 0)
    proj, gk_low = _modmm(x, mod(0, 1, 0), mod(0, 1, 1), wts["a_in_t"], tm, BF16,
                          widths=(GLA_MAIN, GLA_GATE_RANK), w_is_t=True)
    gla_heads = GLA_HEADS_PER_STEP if batch == 1 else GLA_HEADS
    jobs, keep = side_jobs(((0, 1), (1, 0)), ("a_out",), _gla_steps(batch, seq, gla_heads))
    o, gla_state, *cast = _gla(proj, gk_low, wts["a_gk2"], wts["b_a_gk"], wts["g_a_norm"], gla_s0, batch, seq,
                               gla_heads, jobs)
    keep(cast)
    jobs, keep = side_jobs((), ("kv", "b_q", "b_out"), m // tm)
    x, *cast = _proj_ln(o, wts["a_out"], x, mod(0, 1, 2), wts["ln_g"][0, 1][None], wts["ln_b"][0, 1][None], tm, jobs)
    keep(cast)
    x = yield ffn(x, 0, 2, 1)

    kv_shift = _Mod(ada_kv, 0, row_block, 0, per_seq)
    kv_scale = _Mod(ada_kv, 0, row_block, 1, per_seq)
    if past is None:
        assert tm == min(BAND_PAST, seq)
        kv, kv_out = _modmm(x, kv_shift, kv_scale, wts["kv"], tm, BF16, tail_f32=True)
    else:
        kv_out = _modmm(x, kv_shift, kv_scale, wts["kv"], tm, F32)

    x = yield ffn(x, 1, 0, 0)
    q = _modmm(x, mod(1, 1, 0), mod(1, 1, 1), wts["b_q"], tm, BF16,
               out_scale=ATT_HD ** -0.5 * LOG2E)
    if past is None:
        jobs, keep = side_jobs(((1, 1),), (), (ATT_HEADS // ATT_HEADS_PER_STEP) * (seq // BAND_PAST))
        o, *cast = _band_attn(q, kv, wts["rel_bias"], jobs)
        keep(cast)
    else:
        o = _step_attn(q, kv_out, past[0], past[1], wts["rel_bias"], batch, seq)
    x, = _proj_ln(o, wts["b_out"], x, mod(1, 1, 2), wts["ln_g"][1, 1][None], wts["ln_b"][1, 1][None], tm)
    x = yield ffn(x, 1, 2, 1)

    n_rows = kv_out.shape[0] // batch
    k_out = kv_out[:, :D_MODEL].reshape(batch, n_rows, ATT_HEADS, ATT_HD)
    v_out = kv_out[:, D_MODEL:].reshape(batch, n_rows, ATT_HEADS, ATT_HD)
    return x.reshape(batch, seq, D_MODEL), gla_state[None], k_out, v_out


def kernel(x_prompt, x_sample, state_gla, cache_band_k, cache_band_v, c_prompt, c_sample, w_ada, b_ada, ln_g, ln_b, w_ffn_up, w_ffn_down, w_a_in, w_a_gk2, b_a_gk, g_a_norm, w_a_out, w_ada_kv, b_ada_kv, w_kv, w_b_q, rel_bias, w_b_out):
    assert DEPTH == 2 and w_a_in.shape[0] == 1 and w_b_q.shape[0] == 1
    n_prompt, n_sample = x_prompt.shape[0], x_sample.shape[0]
    assert n_prompt == 1 and n_sample <= MOD_ROWS

    lead = lambda w: w.reshape((1, 1) + w.shape[-2:])
    wts = {
        "ffn_f32": (w_ffn_up, w_ffn_down),
        "ffn": {(0, 0): (*_cast_split(w_ffn_up, (D_FF, D_FF), CAST_ROWS),
                         *_cast_split(w_ffn_down, (D_MODEL,), D_FF // 8))},
        "a_in_t": _cast_split(lead(jnp.swapaxes(w_a_in, 1, 2)), (D_MODEL,), A_IN_CAST_ROWS)[0],
        "a_gk2": w_a_gk2[0].astype(BF16),
        "b_a_gk": b_a_gk[0][None],
        "g_a_norm": g_a_norm[0][None],
        "f32": {"a_out": lead(w_a_out), "kv": lead(w_kv), "b_q": lead(w_b_q), "b_out": lead(w_b_out)},
        "rel_bias": rel_bias[0],
        "ln_g": ln_g,
        "ln_b": ln_b,
    }

    c_all = jnp.concatenate([jnp.pad(c, ((0, MOD_ROWS - c.shape[0]), (0, 0))) for c in (c_prompt, c_sample)], axis=0)
    ada = _ada(c_all, w_ada, b_ada)
    ada_kv = _ada(c_all, w_ada_kv[None], b_ada_kv[None])

    gla_zero = jnp.zeros((n_prompt, GLA_HEADS, GLA_HK, GLA_HV), state_gla.dtype)
    trunks = [_trunk(x_prompt, ada, ada_kv, 0, gla_zero, None, wts),
              _trunk(x_sample, ada, ada_kv, 1, state_gla[0], (cache_band_k, cache_band_v), wts)]
    requests = [next(trunk) for trunk in trunks]
    results = [None, None]
    while None in results:
        streams = [stream for stream, _ in requests]
        (l, sub, idx), = {key for _, key in requests}
        new_x = _ffn(streams, *wts["ffn"][l, idx], wts["ln_g"][l, sub][None], wts["ln_b"][l, sub][None],
                     FFN_ROW_TILE)
        for n, (trunk, x) in enumerate(zip(trunks, new_x)):
            try:
                requests[n] = trunk.send(x)
            except StopIteration as done:
                results[n] = done.value
    (y_p, s_p, k_p, v_p), (y_s, s_s, k_s, v_s) = results
    return (y_p, y_s, s_p, s_s, k_p, v_p, k_s, v_s)
```

```python
import functools
from typing import NamedTuple

import jax
import jax.numpy as jnp
from jax import lax
from jax.experimental import pallas as pl
from jax.experimental.pallas import tpu as pltpu

F32 = jnp.float32
BF16 = jnp.bfloat16

D_MODEL = 2048
DEPTH = 2
CHUNK = 64
GLA_HEADS = 4
GLA_DK = D_MODEL // 2
GLA_DV = D_MODEL
GLA_HK = GLA_DK // GLA_HEADS
GLA_HV = GLA_DV // GLA_HEADS
GLA_GATE_RANK = 16
GLA_GATE_NORMALIZER = 16.0
GLA_MAIN = 2 * GLA_DK + 2 * GLA_DV
ATT_HEADS = 16
ATT_HD = D_MODEL // ATT_HEADS
BAND_PAST = 8 * CHUNK
MAX_REL = 256
D_FF = 5504
ALPHA = (2 * DEPTH) ** 0.25
LN_EPS = 1e-5
RMS_EPS = 1e-6
NEG = -1e30
LOG2E = 1.4426950408889634

LANE = 128
FF_TILE = 768
FF_LAST = D_FF - (D_FF - 1) // FF_TILE * FF_TILE
ROW_TILE = 512
FFN_ROW_TILE = 1024
A_IN_CAST_ROWS = 80
CAST_ROWS = 256
FFN_CAST_DOWN_ROWS = 128
GLA_HEADS_PER_STEP = 2
PROJ_SUB = 128
MODMM_COLS = 1024
ATT_SUB = 256
ATT_HEADS_PER_STEP = 4
REL_LEN = BAND_PAST + 2 * ATT_SUB
MOD_ROWS = 8
ADA_ROWS = 2 * MOD_ROWS
ADA_TILE = 2048
VMEM_LIMIT = 56 * 1024 * 1024
FFN_VMEM_LIMIT = 60 * 1024 * 1024


def _params(*sem):
    return pltpu.CompilerParams(dimension_semantics=sem, vmem_limit_bytes=VMEM_LIMIT)


def _layer_norm(z, g, b):
    mu = jnp.mean(z, axis=-1, keepdims=True)
    zc = z - mu
    var = jnp.mean(zc * zc, axis=-1, keepdims=True)
    return zc * lax.rsqrt(var + LN_EPS) * g + b


def _silu(a):
    half = 0.5 * a
    return half + half * jnp.tanh(half)


def _ada_kernel(c_ref, w_ref, b_ref, o_ref):
    o_ref[...] = jnp.dot(_silu(c_ref[...]), w_ref[...], preferred_element_type=F32) + b_ref[...]


def _ada(c_all, w, b):
    n_l, _, n = w.shape
    return pl.pallas_call(
        _ada_kernel,
        grid=(n_l, n // ADA_TILE),
        in_specs=[
            pl.BlockSpec((ADA_ROWS, D_MODEL), lambda l, j: (0, 0)),
            pl.BlockSpec((None, D_MODEL, ADA_TILE), lambda l, j: (l, 0, j)),
            pl.BlockSpec((None, 1, ADA_TILE), lambda l, j: (l, 0, j)),
        ],
        out_specs=pl.BlockSpec((None, ADA_ROWS, ADA_TILE), lambda l, j: (l, 0, j)),
        out_shape=jax.ShapeDtypeStruct((n_l, ADA_ROWS, n), F32),
        compiler_params=_params("arbitrary", "arbitrary"),
        name="ada",
    )(c_all, w, b.reshape(n_l, 1, n))


class _Mod(NamedTuple):
    table: jax.Array
    layer: int
    row_block: int
    col: int
    seq: int | None

    @property
    def spec(self):
        return pl.BlockSpec((None, MOD_ROWS, D_MODEL), lambda *_: (self.layer, self.row_block, self.col))


def _mod_rows(ref, n_rows, seq):
    if seq is None:
        return ref[0:1, :]
    n_seq = n_rows // seq
    return jnp.broadcast_to(ref[0:n_seq, :][:, None, :], (n_seq, seq, D_MODEL)).reshape(n_rows, D_MODEL)


def _ffn_kernel(*refs, seqs, jobs):
    n = len(seqs)
    wa_ref, wu_ref, wd_ref, g_ref, b_ref = refs[4 * n:4 * n + 5]
    n_in = 4 * n + 5 + len(jobs)
    o_refs = refs[n_in:n_in + n]
    n_out = n_in + n + sum(len(job.widths) for job in jobs)
    h_refs = refs[n_out:n_out + n]
    x_buf, x_sem = refs[n_out + n:]
    i, j = pl.program_id(0), pl.program_id(1)
    last = pl.num_programs(1) - 1
    _side_casts(i * pl.num_programs(1) + j, None, jobs, refs[4 * n + 5:n_in], refs[n_in + n:n_out])
    tm = x_buf.shape[0]

    def fetch(tile):
        return pltpu.make_async_copy(refs[0].at[pl.ds(tile * tm, tm), :], x_buf, x_sem)

    @pl.when((i == 0) & (j == 0))
    def _():
        fetch(0).start()

    @pl.when(j == 0)
    def _():
        fetch(i).wait()

    @pl.when((j == 1) & (i + 1 < pl.num_programs(0)))
    def _():
        fetch(i + 1).start()

    def stream(x_ref, sh_ref, sc_ref, gt_ref, o_ref, h_ref, seq, active):
        mod = lambda ref: _mod_rows(ref, x_ref.shape[0], seq)

        def chunk(cols):
            h = h_ref[...]
            a = jnp.dot(h, wa_ref[:, :cols], preferred_element_type=F32)
            u = jnp.dot(h, wu_ref[:, :cols], preferred_element_type=F32)
            y = jnp.dot((_silu(a) * u).astype(BF16), wd_ref[:cols, :], preferred_element_type=F32)
            o_ref[...] += (0.5 * (1.0 + mod(gt_ref))) * y

        @pl.when(active & (j == 0))
        def _():
            x = x_ref[...]
            h_ref[...] = (x * (1.0 + mod(sc_ref)) + mod(sh_ref)).astype(BF16)
            o_ref[...] = ALPHA * x
            chunk(FF_TILE)

        @pl.when(active & (j > 0) & (j < last))
        def _():
            chunk(FF_TILE)

        @pl.when(active & (j == last))
        def _():
            chunk(FF_LAST)
            o_ref[...] = _layer_norm(o_ref[...], g_ref[...], b_ref[...])

    for s in range(n):
        active = True if s == 0 else (i == pl.num_programs(0) - 1)
        stream(x_buf if s == 0 else refs[4 * s], *refs[4 * s + 1:4 * s + 4], o_refs[s], h_refs[s], seqs[s], active)


def _ffn(streams, w_a, w_u, w_down, ln_g, ln_b, tm, jobs=()):
    m = streams[0][0].shape[0]
    n_ff = pl.cdiv(D_FF, FF_TILE)
    assert n_ff >= 2
    job_in, job_out, job_shape = _job_specs(jobs, lambda i, j: i * n_ff + j, m // tm * n_ff)
    vec = pl.BlockSpec((1, D_MODEL), lambda i, j: (0, 0))
    up = pl.BlockSpec((D_MODEL, FF_TILE), lambda i, j: (0, j))
    o_specs = [pl.BlockSpec((tm, D_MODEL), lambda i, j: (i, 0))]
    o_specs += [pl.BlockSpec(x.shape, lambda i, j: (0, 0)) for x, *_ in streams[1:]]
    x_specs = [pl.BlockSpec(memory_space=pl.ANY)] + o_specs[1:]
    in_specs, args = [], []
    for x_spec, (x, shift, scale, gate) in zip(x_specs, streams):
        in_specs += [x_spec, shift.spec, scale.spec, gate.spec]
        args += [x, shift.table, scale.table, gate.table]
    return pl.pallas_call(
        functools.partial(_ffn_kernel, seqs=tuple(shift.seq for _, shift, *_ in streams), jobs=jobs),
        grid=(m // tm, n_ff),
        in_specs=[
            *in_specs,
            up, up,
            pl.BlockSpec((FF_TILE, D_MODEL), lambda i, j: (j, 0)),
            vec, vec,
            *job_in,
        ],
        out_specs=[*o_specs, *job_out],
        out_shape=[*(jax.ShapeDtypeStruct(x.shape, F32) for x, *_ in streams), *job_shape],
        scratch_shapes=[*(pltpu.VMEM(spec.block_shape, BF16) for spec in o_specs),
                        pltpu.VMEM((tm, D_MODEL), F32), pltpu.SemaphoreType.DMA(())],
        compiler_params=pltpu.CompilerParams(dimension_semantics=("arbitrary", "arbitrary"),
                                             vmem_limit_bytes=FFN_VMEM_LIMIT),
        name="ffn",
    )(*args, w_a, w_u, w_down, ln_g, ln_b, *(job.src for job in jobs))


def _cast_kernel(x_ref, *o_refs):
    off = 0
    for o_ref in o_refs:
        n = o_ref.shape[-1]
        o_ref[...] = x_ref[:, off:off + n].astype(o_ref.dtype)
        off += n


class _CastJob(NamedTuple):
    src: jax.Array
    layer: int
    idx: int
    rows: int
    widths: tuple

    @property
    def n_blocks(self):
        return self.src.shape[2] // self.rows

    def specs(self, step_of):
        r, n = self.src.shape[2:]
        assert r % self.rows == 0 and sum(self.widths) <= n and all(wd % LANE == 0 for wd in self.widths)
        blk = lambda *ids: jnp.minimum(step_of(*ids), self.n_blocks - 1)
        in_spec = pl.BlockSpec((None, None, self.rows, n), lambda *ids: (self.layer, self.idx, blk(*ids), 0))
        out_specs = [pl.BlockSpec((self.rows, wd), lambda *ids: (blk(*ids), 0)) for wd in self.widths]
        out_shape = [jax.ShapeDtypeStruct((r, wd), BF16) for wd in self.widths]
        return in_spec, out_specs, out_shape


def _side_casts(step, n_steps, jobs, in_refs, out_refs):
    pos = 0
    for job, x_ref in zip(jobs, in_refs):
        outs = out_refs[pos:pos + len(job.widths)]
        pos += len(job.widths)
        if job.n_blocks == n_steps:
            _cast_kernel(x_ref, *outs)
        else:
            pl.when(step < job.n_blocks)(functools.partial(_cast_kernel, x_ref, *outs))


def _job_specs(jobs, step_of, n_steps):
    in_specs, out_specs, out_shape = [], [], []
    for job in jobs:
        assert job.n_blocks <= n_steps
        i, o, s = job.specs(step_of)
        in_specs.append(i)
        out_specs += o
        out_shape += s
    return in_specs, out_specs, out_shape


def _cast_split(w, widths, rows, layer=0, idx=0):
    job = _CastJob(w, layer, idx, rows, tuple(widths))
    in_spec, out_specs, out_shape = job.specs(lambda i: i)
    return pl.pallas_call(
        _cast_kernel,
        grid=(job.n_blocks,),
        in_specs=[in_spec],
        out_specs=out_specs,
        out_shape=out_shape,
        compiler_params=_params("parallel"),
        name="cast",
    )(w)


def _ffn_cast_jobs(w_ffn_up, w_ffn_down, layer, idx, n_steps):
    return (_CastJob(w_ffn_up, layer, idx, D_MODEL // n_steps, (D_FF, D_FF)),
            _CastJob(w_ffn_down, layer, idx, FFN_CAST_DOWN_ROWS, (D_MODEL,)))


def _modmm_kernel(x_ref, sh_ref, sc_ref, w_ref, o_ref, *rest, seq, w_is_t, out_scale, tail_f32):
    h_ref = rest[-1]
    mod = lambda ref: _mod_rows(ref, x_ref.shape[0], seq)
    h_ref[...] = (x_ref[...] * (1.0 + mod(sc_ref)) + mod(sh_ref)).astype(BF16)
    h = h_ref[...]
    nt = (((1,), (1,)), ((), ()))

    def project(lo, hi):
        if w_is_t:
            y = lax.dot_general(h, w_ref[lo:hi, :], nt, preferred_element_type=F32)
        else:
            y = jnp.dot(h, w_ref[:, lo:hi], preferred_element_type=F32)
        return y if out_scale is None else y * out_scale

    n = o_ref.shape[1]
    if tail_f32:
        tail_ref, = rest[:-1]
        for lo in range(0, n, MODMM_COLS):
            y = project(lo, lo + MODMM_COLS)
            o_ref[:, lo:lo + MODMM_COLS] = y.astype(o_ref.dtype)
            tail_ref[:, lo:lo + MODMM_COLS] = y
        return
    for lo in range(0, n, MODMM_COLS):
        o_ref[:, lo:lo + MODMM_COLS] = project(lo, lo + MODMM_COLS).astype(o_ref.dtype)
    for extra_ref in rest[:-1]:
        extra_ref[...] = project(n, n + extra_ref.shape[1]).astype(extra_ref.dtype)
        n += extra_ref.shape[1]


def _modmm(x, shift, scale, w, tm, out_dtype, widths=None, w_is_t=False, out_scale=None, tail_f32=False):
    m = x.shape[0]
    widths = widths or (w.shape[0] if w_is_t else w.shape[1],)
    assert widths[0] % MODMM_COLS == 0 and not (tail_f32 and len(widths) > 1)
    out_specs = [pl.BlockSpec((tm, wd), lambda i: (i, 0)) for wd in widths]
    out_shape = [jax.ShapeDtypeStruct((m, wd), out_dtype) for wd in widths]
    if tail_f32:
        out_specs.append(pl.BlockSpec((tm, widths[0]), lambda i: (0, 0)))
        out_shape.append(jax.ShapeDtypeStruct((tm, widths[0]), F32))
    out = pl.pallas_call(
        functools.partial(_modmm_kernel, seq=shift.seq, w_is_t=w_is_t, out_scale=out_scale, tail_f32=tail_f32),
        grid=(m // tm,),
        in_specs=[
            pl.BlockSpec((tm, D_MODEL), lambda i: (i, 0)),
            shift.spec, scale.spec,
            pl.BlockSpec(w.shape, lambda i: (0, 0), pipeline_mode=pl.Buffered(1)),
        ],
        out_specs=out_specs,
        out_shape=out_shape,
        scratch_shapes=[pltpu.VMEM((tm, D_MODEL), BF16)],
        compiler_params=_params("arbitrary"),
        name="modmm",
    )(x, shift.table, scale.table, w)
    return out if len(out) > 1 else out[0]


def _proj_ln_kernel(a_ref, w_ref, x_ref, gt_ref, g_ref, b_ref, *rest, seq, jobs, n_steps):
    n_jobs = len(jobs)
    o_ref = rest[n_jobs]
    _side_casts(pl.program_id(0), n_steps, jobs, rest[:n_jobs], rest[n_jobs + 1:])
    sub = min(PROJ_SUB, a_ref.shape[0])
    n_sub = a_ref.shape[0] // sub
    dot = lambda r: jnp.dot(a_ref[r * sub:(r + 1) * sub, :], w_ref[...], preferred_element_type=F32)
    gate = _mod_rows(gt_ref, a_ref.shape[0], seq)
    y = dot(0)
    for r in range(n_sub):
        y_next = dot(r + 1) if r + 1 < n_sub else None
        rows = slice(r * sub, (r + 1) * sub)
        z = ALPHA * x_ref[rows, :] + (1.0 + gate[rows if gate.shape[0] > 1 else slice(None), :]) * y
        o_ref[rows, :] = _layer_norm(z, g_ref[...], b_ref[...])
        y = y_next


def _proj_ln(a, w, x, gate, ln_g, ln_b, tm, jobs=()):
    m = x.shape[0]
    vec = pl.BlockSpec((1, D_MODEL), lambda i: (0, 0))
    job_in, job_out, job_shape = _job_specs(jobs, lambda i: i, m // tm)
    return pl.pallas_call(
        functools.partial(_proj_ln_kernel, seq=gate.seq, jobs=jobs, n_steps=m // tm),
        grid=(m // tm,),
        in_specs=[
            pl.BlockSpec((tm, D_MODEL), lambda i: (i, 0)),
            pl.BlockSpec((D_MODEL, D_MODEL), lambda i: (0, 0)),
            pl.BlockSpec((tm, D_MODEL), lambda i: (i, 0)),
            gate.spec,
            vec, vec,
            *job_in,
        ],
        out_specs=[pl.BlockSpec((tm, D_MODEL), lambda i: (i, 0)), *job_out],
        out_shape=[jax.ShapeDtypeStruct((m, D_MODEL), F32), *job_shape],
        compiler_params=_params("arbitrary"),
        name="proj_ln",
    )(a, w, x, gate.table, ln_g, ln_b, *(job.src for job in jobs))


def _gla_kernel(q_ref, k_ref, v_ref, g_ref, gk_ref, w2_ref, bgk_ref, gn_ref, s0_ref, *rest,
                blk, n_blk, n_heads, jobs, n_steps):
    n_jobs = len(jobs)
    o_ref, s_ref = rest[n_jobs:n_jobs + 2]
    step = (pl.program_id(0) * pl.num_programs(1) + pl.program_id(1)) * pl.num_programs(2) + pl.program_id(2)
    _side_casts(step, n_steps, jobs, rest[:n_jobs], rest[n_jobs + 2:])

    @pl.when(pl.program_id(2) == 0)
    def _():
        s_ref[...] = s0_ref[...]

    causal = (lax.broadcasted_iota(jnp.int32, (blk, blk), 0)
              >= lax.broadcasted_iota(jnp.int32, (blk, blk), 1))
    tril = jnp.broadcast_to(jnp.where(causal, 1.0, 0.0).astype(BF16), (n_blk, blk, blk))
    eye = (lax.broadcasted_iota(jnp.int32, (GLA_HK, GLA_HK), 0)
           == lax.broadcasted_iota(jnp.int32, (GLA_HK, GLA_HK), 1))
    tn = (((0,), (0,)), ((), ()))
    bnn = (((2,), (1,)), ((0,), (0,)))
    bnt = (((2,), (2,)), ((0,), (0,)))

    def blocks(t):
        return t.reshape(n_blk, blk, t.shape[-1])

    for head in range(n_heads):
        kc = slice(head * GLA_HK, (head + 1) * GLA_HK)
        vc = slice(head * GLA_HV, (head + 1) * GLA_HV)
        z = jnp.dot(gk_ref[...], w2_ref[:, kc], preferred_element_type=F32) + bgk_ref[:, kc]
        log_a = blocks((jnp.minimum(z, 0.0) - jnp.log(1.0 + jnp.exp(-jnp.abs(z)))) * (1.0 / GLA_GATE_NORMALIZER))
        hi = log_a.astype(BF16)
        lo = (log_a - hi.astype(F32)).astype(BF16)
        b = (lax.dot_general(tril, hi, bnn, preferred_element_type=F32)
             + lax.dot_general(tril, lo, bnn, preferred_element_type=F32))
        b_last = b[:, blk - 1:blk, :]
        q = blocks(q_ref[:, kc].astype(F32)) * (GLA_HK ** -0.5)
        k = blocks(k_ref[:, kc].astype(F32))
        v = blocks(v_ref[:, vc])
        q_t = (q * jnp.exp(b)).astype(BF16)
        k_t = (k * jnp.exp(-b)).astype(BF16)
        k_dec = (k * jnp.exp(b_last - b)).astype(BF16)
        scores = lax.dot_general(q_t, k_t, bnt, preferred_element_type=F32)
        scores = jnp.where(causal[None], scores, 0.0).astype(BF16)
        o_intra = lax.dot_general(scores, v, bnn, preferred_element_type=F32)
        decay = jnp.exp(b_last)

        update = [lax.dot_general(k_dec[c], v[c], tn, preferred_element_type=F32) for c in range(n_blk)]
        decay_col = [jnp.sum(jnp.where(eye, jnp.broadcast_to(decay[c], (GLA_HK, GLA_HK)), 0.0),
                             axis=1, keepdims=True) for c in range(n_blk)]
        state = s_ref[0, head]
        starts = []
        for c in range(n_blk):
            starts.append(state.astype(BF16))
            state = decay_col[c] * state + update[c]
        s_ref[0, head] = state
        o = o_intra + jnp.stack([jnp.dot(q_t[c], starts[c], preferred_element_type=F32) for c in range(n_blk)])
        on = o * lax.rsqrt(jnp.mean(o * o, axis=-1, keepdims=True) + RMS_EPS) * gn_ref[...]
        gate = _silu(g_ref[:, vc].astype(F32)).reshape(o.shape)
        o_ref[:, vc] = (on * gate).reshape(o_ref.shape[0], GLA_HV).astype(o_ref.dtype)


def _gla_steps(batch, seq, n_heads):
    return batch * (GLA_HEADS // n_heads) * (seq // min(ROW_TILE, seq))


def _gla(proj, gk_low, w_gk2, b_gk, g_norm, s0, batch, seq, n_heads, jobs=()):
    blk = min(CHUNK, seq)
    rows = min(ROW_TILE, seq)
    n_steps = seq // rows
    n_groups = GLA_HEADS // n_heads
    m = batch * seq
    wk, wv = n_heads * GLA_HK, n_heads * GLA_HV
    k_off = GLA_DK // wk
    v_off = 2 * GLA_DK // wv
    g_off = (2 * GLA_DK + GLA_DV) // wv
    row = lambda b, h, c: b * n_steps + c
    state_spec = pl.BlockSpec((1, n_heads, GLA_HK, GLA_HV), lambda b, h, c: (b, h, 0, 0))
    grid_steps = _gla_steps(batch, seq, n_heads)
    job_in, job_out, job_shape = _job_specs(jobs, lambda b, h, c: (b * n_groups + h) * n_steps + c, grid_steps)
    return pl.pallas_call(
        functools.partial(_gla_kernel, blk=blk, n_blk=rows // blk, n_heads=n_heads, jobs=jobs, n_steps=grid_steps),
        grid=(batch, n_groups, n_steps),
        in_specs=[
            pl.BlockSpec((rows, wk), lambda b, h, c: (row(b, h, c), h)),
            pl.BlockSpec((rows, wk), lambda b, h, c: (row(b, h, c), k_off + h)),
            pl.BlockSpec((rows, wv), lambda b, h, c: (row(b, h, c), v_off + h)),
            pl.BlockSpec((rows, wv), lambda b, h, c: (row(b, h, c), g_off + h)),
            pl.BlockSpec((rows, GLA_GATE_RANK), lambda b, h, c: (row(b, h, c), 0)),
            pl.BlockSpec((GLA_GATE_RANK, wk), lambda b, h, c: (0, h)),
            pl.BlockSpec((1, wk), lambda b, h, c: (0, h)),
            pl.BlockSpec((1, GLA_HV), lambda b, h, c: (0, 0)),
            state_spec,
            *job_in,
        ],
        out_specs=[
            pl.BlockSpec((rows, wv), lambda b, h, c: (row(b, h, c), h)),
            state_spec,
            *job_out,
        ],
        out_shape=[
            jax.ShapeDtypeStruct((m, GLA_DV), BF16),
            jax.ShapeDtypeStruct(s0.shape, F32),
            *job_shape,
        ],
        compiler_params=_params("arbitrary", "arbitrary", "arbitrary"),
        name="gla",
    )(proj, proj, proj, proj, gk_low, w_gk2, b_gk, g_norm, s0, *(job.src for job in jobs))


def _band_attn_kernel(q_ref, kp_ref, kc_ref, vp_ref, vc_ref, rel_ref, *rest, jobs, n_steps):
    n_jobs = len(jobs)
    o_ref, bias_ref = rest[n_jobs], rest[-1]
    i = pl.program_id(1)
    _side_casts(pl.program_id(0) * pl.num_programs(1) + i, n_steps, jobs, rest[:n_jobs], rest[n_jobs + 1:-1])
    nt = (((1,), (1,)), ((), ()))
    n_keys = BAND_PAST + ATT_SUB
    n_sub = BAND_PAST // ATT_SUB

    @pl.when(i <= 1)
    def _():
        col = lax.broadcasted_iota(jnp.int32, (ATT_SUB, n_keys), 1)
        qc = lax.broadcasted_iota(jnp.int32, (ATT_SUB, n_keys), 0) // CHUNK
        kc = col // CHUNK
        in_band = (kc >= qc) & (kc <= qc + BAND_PAST // CHUNK)
        for h in range(ATT_HEADS_PER_STEP):
            rel = jnp.broadcast_to(rel_ref[h], (ATT_SUB, REL_LEN))
            toeplitz = pltpu.roll(rel, 0, 1, stride=1, stride_axis=0)[:, ATT_SUB:] * LOG2E
            for s in range(n_sub):
                key_pos = (i - 1) * BAND_PAST + s * ATT_SUB + col
                bias_ref[h, s] = jnp.where(in_band & (key_pos >= 0), toeplitz, NEG)

    units = [(h, s) for h in range(ATT_HEADS_PER_STEP) for s in range(n_sub)]

    def scores(h, s):
        lo, hi = s * ATT_SUB, (s + 1) * ATT_SUB
        cols = slice(h * ATT_HD, (h + 1) * ATT_HD)
        k = jnp.concatenate([kp_ref[lo:, cols], kc_ref[:hi, cols]], axis=0)
        return lax.dot_general(q_ref[lo:hi, cols], k, nt, preferred_element_type=F32) + bias_ref[h, s]

    def finish(h, s, sc):
        lo, hi = s * ATT_SUB, (s + 1) * ATT_SUB
        cols = slice(h * ATT_HD, (h + 1) * ATT_HD)
        v = jnp.concatenate([vp_ref[lo:, cols], vc_ref[:hi, cols]], axis=0)
        p = jnp.exp2(sc - jnp.max(sc, axis=-1, keepdims=True))
        denom = jnp.sum(p, axis=-1, keepdims=True)
        o = jnp.dot(p.astype(BF16), v, preferred_element_type=F32) / denom
        o_ref[lo:hi, cols] = o.astype(o_ref.dtype)

    sc = scores(*units[0])
    for n, unit in enumerate(units):
        sc_next = scores(*units[n + 1]) if n + 1 < len(units) else None
        finish(*unit, sc)
        sc = sc_next


def _band_rel(table):
    assert BAND_PAST + ATT_SUB - REL_LEN // 2 == MAX_REL and REL_LEN // 2 <= 2 * MAX_REL
    far = jnp.broadcast_to(table[:, 2 * MAX_REL:], (ATT_HEADS, REL_LEN // 2))
    near = table[:, 2 * MAX_REL - REL_LEN // 2 + 1:][:, ::-1]
    return jnp.concatenate([far, near], axis=1).astype(F32)[:, None, :]


def _band_attn(q, kv, table, jobs=()):
    t = q.shape[0]
    n_keys = BAND_PAST + ATT_SUB
    prev = lambda i: jnp.maximum(i - 1, 0)
    n_groups = ATT_HEADS // ATT_HEADS_PER_STEP
    blk = (BAND_PAST, ATT_HEADS_PER_STEP * ATT_HD)
    n_blocks = t // BAND_PAST
    job_in, job_out, job_shape = _job_specs(jobs, lambda h, i: h * n_blocks + i, n_groups * n_blocks)
    return pl.pallas_call(
        functools.partial(_band_attn_kernel, jobs=jobs, n_steps=n_groups * n_blocks),
        grid=(n_groups, n_blocks),
        in_specs=[
            pl.BlockSpec(blk, lambda h, i: (i, h)),
            pl.BlockSpec(blk, lambda h, i: (prev(i), h)),
            pl.BlockSpec(blk, lambda h, i: (i, h)),
            pl.BlockSpec(blk, lambda h, i: (prev(i), n_groups + h)),
            pl.BlockSpec(blk, lambda h, i: (i, n_groups + h)),
            pl.BlockSpec((ATT_HEADS_PER_STEP, 1, REL_LEN), lambda h, i: (h, 0, 0)),
            *job_in,
        ],
        out_specs=[pl.BlockSpec(blk, lambda h, i: (i, h)), *job_out],
        out_shape=[jax.ShapeDtypeStruct((t, D_MODEL), BF16), *job_shape],
        scratch_shapes=[pltpu.VMEM((ATT_HEADS_PER_STEP, BAND_PAST // ATT_SUB, ATT_SUB, n_keys), F32)],
        compiler_params=_params("arbitrary", "arbitrary"),
        name="band_attn",
    )(q, kv, kv, kv, kv, _band_rel(table), *(job.src for job in jobs))


def _step_attn_kernel(q_ref, kvn_ref, ck_ref, cv_ref, rp_ref, rn_ref, o_ref, bias_ref, bias_new_ref):
    nt = (((1,), (1,)), ((), ()))
    n_past = ck_ref.shape[1]
    n_keys = n_past * ATT_HEADS

    seq = q_ref.shape[0]

    @pl.when(pl.program_id(0) == 0)
    def _():
        def toeplitz(ref):
            rows = [pltpu.roll(jnp.broadcast_to(ref[h], (seq, ref.shape[-1])), 0, 1, stride=1, stride_axis=0)
                    for h in range(ATT_HEADS)]
            return jnp.concatenate(rows, axis=0)

        def spread(bias, col_of_key, head_of_key):
            pick = jnp.where(lax.broadcasted_iota(jnp.int32, col_of_key.shape, 0) == col_of_key, 1.0, 0.0).astype(BF16)
            hi = bias.astype(BF16)
            lo = (bias - hi.astype(F32)).astype(BF16)
            wide = jnp.dot(hi, pick, preferred_element_type=F32) + jnp.dot(lo, pick, preferred_element_type=F32)
            q_head = lax.broadcasted_iota(jnp.int32, wide.shape, 0) // seq
            return jnp.where(q_head == head_of_key[0:1, :], wide, NEG)

        key = lax.broadcasted_iota(jnp.int32, (n_past + LANE, n_keys), 1)
        bias_ref[...] = spread(toeplitz(rp_ref), LANE + key // ATT_HEADS, key % ATT_HEADS)
        key = lax.broadcasted_iota(jnp.int32, (LANE, ATT_HEADS * seq), 1)
        bias_new_ref[...] = spread(toeplitz(rn_ref), seq + key % seq, key // seq)

    heads = lambda ref, off: jnp.concatenate(
        [ref[:, off + h * ATT_HD:off + (h + 1) * ATT_HD] for h in range(ATT_HEADS)], axis=0)
    q = heads(q_ref, 0)
    k_new = heads(kvn_ref, 0).astype(BF16)
    v_new = heads(kvn_ref, D_MODEL).astype(BF16)
    k_past = ck_ref[0].reshape(n_keys, ATT_HD).astype(BF16)
    v_past = cv_ref[0].reshape(n_keys, ATT_HD).astype(BF16)
    sp = lax.dot_general(q, k_past, nt, preferred_element_type=F32) + bias_ref[...]
    sn = lax.dot_general(q, k_new, nt, preferred_element_type=F32) + bias_new_ref[...]
    mx = jnp.maximum(jnp.max(sp, axis=-1, keepdims=True), jnp.max(sn, axis=-1, keepdims=True))
    pp = jnp.exp2(sp - mx)
    pn = jnp.exp2(sn - mx)
    denom = jnp.sum(pp, axis=-1, keepdims=True) + jnp.sum(pn, axis=-1, keepdims=True)
    o = (jnp.dot(pp.astype(BF16), v_past, preferred_element_type=F32)
         + jnp.dot(pn.astype(BF16), v_new, preferred_element_type=F32)) / denom
    for h in range(ATT_HEADS):
        o_ref[:, h * ATT_HD:(h + 1) * ATT_HD] = o[h * seq:(h + 1) * seq, :].astype(o_ref.dtype)


def _step_attn(q, kv_new, cache_k, cache_v, table, batch, seq):
    n_past = cache_k.shape[1]
    rows = ATT_HEADS * seq
    assert n_past >= MAX_REL and seq <= LANE // 2
    n_far = n_past + LANE - MAX_REL + 1
    rel_past = jnp.concatenate([jnp.broadcast_to(table[:, 2 * MAX_REL:], (ATT_HEADS, n_far)),
                                table[:, 2 * MAX_REL - 1:MAX_REL:-1]], axis=1)
    rel_new = table[:, MAX_REL + seq:MAX_REL + seq - LANE:-1]
    rel_past, rel_new = (r.astype(F32)[:, None, :] * LOG2E for r in (rel_past, rel_new))
    return pl.pallas_call(
        _step_attn_kernel,
        grid=(batch,),
        in_specs=[
            pl.BlockSpec((seq, D_MODEL), lambda b: (b, 0)),
            pl.BlockSpec((seq, 2 * D_MODEL), lambda b: (b, 0)),
            pl.BlockSpec((1, n_past, ATT_HEADS, ATT_HD), lambda b: (b, 0, 0, 0)),
            pl.BlockSpec((1, n_past, ATT_HEADS, ATT_HD), lambda b: (b, 0, 0, 0)),
            pl.BlockSpec((ATT_HEADS, 1, n_past + LANE), lambda b: (0, 0, 0)),
            pl.BlockSpec((ATT_HEADS, 1, LANE), lambda b: (0, 0, 0)),
        ],
        out_specs=pl.BlockSpec((seq, D_MODEL), lambda b: (b, 0)),
        out_shape=jax.ShapeDtypeStruct((batch * seq, D_MODEL), BF16),
        scratch_shapes=[pltpu.VMEM((rows, n_past * ATT_HEADS), F32), pltpu.VMEM((rows, rows), F32)],
        compiler_params=_params("arbitrary"),
        name="step_attn",
    )(q, kv_new, cache_k, cache_v, rel_past, rel_new)


def _trunk(x, ada, ada_kv, row_block, gla_s0, past, wts):
    batch, seq, _ = x.shape
    m = batch * seq
    tm = min(ROW_TILE, m)
    x = x.reshape(m, D_MODEL)
    assert batch <= MOD_ROWS and (batch == 1 or m == tm)
    per_seq = None if batch == 1 else seq

    def mod(l, sub, which):
        return _Mod(ada, l, row_block, 3 * sub + which, per_seq)

    def ffn(x, l, sub, idx):
        return (x, mod(l, sub, 0), mod(l, sub, 1), mod(l, sub, 2)), (l, sub, idx)

    def side_jobs(ffn_keys, names, n_steps):
        ffn_keys = [key for key in ffn_keys if key not in wts["ffn"]]
        names = [name for name in names if name not in wts]
        jobs = tuple(job for key in ffn_keys for job in _ffn_cast_jobs(*wts["ffn_f32"], *key, n_steps))
        jobs += tuple(_CastJob(wts["f32"][name], 0, 0, D_MODEL // n_steps, (wts["f32"][name].shape[-1],))
                      for name in names)

        def keep(cast):
            for n, key in enumerate(ffn_keys):
                wts["ffn"][key] = tuple(cast[3 * n:3 * n + 3])
            for n, name in enumerate(names):
                wts[name] = cast[3 * len(ffn_keys) + n]
        return jobs, keep

    x = yield ffn(x, 0, 0, 0)
    proj, gk_low = _modmm(x, mod(0, 1, 0), mod(0, 1, 1), wts["a_in_t"], tm, BF16,
                          widths=(GLA_MAIN, GLA_GATE_RANK), w_is_t=True)
    gla_heads = GLA_HEADS_PER_STEP if batch == 1 else GLA_HEADS
    jobs, keep = side_jobs(((0, 1), (1, 0)), ("a_out",), _gla_steps(batch, seq, gla_heads))
    o, gla_state, *cast = _gla(proj, gk_low, wts["a_gk2"], wts["b_a_gk"], wts["g_a_norm"], gla_s0, batch, seq,
                               gla_heads, jobs)
    keep(cast)
    jobs, keep = side_jobs((), ("kv", "b_q", "b_out"), m // tm)
    x, *cast = _proj_ln(o, wts["a_out"], x, mod(0, 1, 2), wts["ln_g"][0, 1][None], wts["ln_b"][0, 1][None], tm, jobs)
    keep(cast)
    x = yield ffn(x, 0, 2, 1)

    kv_shift = _Mod(ada_kv, 0, row_block, 0, per_seq)
    kv_scale = _Mod(ada_kv, 0, row_block, 1, per_seq)
    if past is None:
        assert tm == min(BAND_PAST, seq)
        kv, kv_out = _modmm(x, kv_shift, kv_scale, wts["kv"], tm, BF16, tail_f32=True)
    else:
        kv_out = _modmm(x, kv_shift, kv_scale, wts["kv"], tm, F32)

    x = yield ffn(x, 1, 0, 0)
    q = _modmm(x, mod(1, 1, 0), mod(1, 1, 1), wts["b_q"], tm, BF16,
               out_scale=ATT_HD ** -0.5 * LOG2E)
    if past is None:
        jobs, keep = side_jobs(((1, 1),), (), (ATT_HEADS // ATT_HEADS_PER_STEP) * (seq // BAND_PAST))
        o, *cast = _band_attn(q, kv, wts["rel_bias"], jobs)
        keep(cast)
    else:
        o = _step_attn(q, kv_out, past[0], past[1], wts["rel_bias"], batch, seq)
    x, = _proj_ln(o, wts["b_out"], x, mod(1, 1, 2), wts["ln_g"][1, 1][None], wts["ln_b"][1, 1][None], tm)
    x = yield ffn(x, 1, 2, 1)

    n_rows = kv_out.shape[0] // batch
    k_out = kv_out[:, :D_MODEL].reshape(batch, n_rows, ATT_HEADS, ATT_HD)
    v_out = kv_out[:, D_MODEL:].reshape(batch, n_rows, ATT_HEADS, ATT_HD)
    return x.reshape(batch, seq, D_MODEL), gla_state[None], k_out, v_out


def kernel(x_prompt, x_sample, state_gla, cache_band_k, cache_band_v, c_prompt, c_sample, w_ada, b_ada, ln_g, ln_b, w_ffn_up, w_ffn_down, w_a_in, w_a_gk2, b_a_gk, g_a_norm, w_a_out, w_ada_kv, b_ada_kv, w_kv, w_b_q, rel_bias, w_b_out):
    assert DEPTH == 2 and w_a_in.shape[0] == 1 and w_b_q.shape[0] == 1
    n_prompt, n_sample = x_prompt.shape[0], x_sample.shape[0]
    assert n_prompt == 1 and n_sample <= MOD_ROWS

    lead = lambda w: w.reshape((1, 1) + w.shape[-2:])
    wts = {
        "ffn_f32": (w_ffn_up, w_ffn_down),
        "ffn": {(0, 0): (*_cast_split(w_ffn_up, (D_FF, D_FF), CAST_ROWS),
                         *_cast_split(w_ffn_down, (D_MODEL,), D_FF // 8))},
        "a_gk2": w_a_gk2[0].astype(BF16),
        "b_a_gk": b_a_gk[0][None],
        "g_a_norm": g_a_norm[0][None],
        "f32": {"a_in_t": lead(jnp.swapaxes(w_a_in, 1, 2)), "a_out": lead(w_a_out), "kv": lead(w_kv),
                "b_q": lead(w_b_q), "b_out": lead(w_b_out)},
        "rel_bias": rel_bias[0],
        "ln_g": ln_g,
        "ln_b": ln_b,
    }

    c_all = jnp.concatenate([jnp.pad(c, ((0, MOD_ROWS - c.shape[0]), (0, 0))) for c in (c_prompt, c_sample)], axis=0)
    ada = _ada(c_all, w_ada, b_ada)
    ada_kv = _ada(c_all, w_ada_kv[None], b_ada_kv[None])

    gla_zero = jnp.zeros((n_prompt, GLA_HEADS, GLA_HK, GLA_HV), state_gla.dtype)
    trunks = [_trunk(x_prompt, ada, ada_kv, 0, gla_zero, None, wts),
              _trunk(x_sample, ada, ada_kv, 1, state_gla[0], (cache_band_k, cache_band_v), wts)]
    requests = [next(trunk) for trunk in trunks]
    results = [None, None]
    while None in results:
        streams = [stream for stream, _ in requests]
        (l, sub, idx), = {key for _, key in requests}
        jobs = () if "a_in_t" in wts else (_CastJob(wts["f32"]["a_in_t"], 0, 0, A_IN_CAST_ROWS, (D_MODEL,)),)
        *new_x, = _ffn(streams, *wts["ffn"][l, idx], wts["ln_g"][l, sub][None], wts["ln_b"][l, sub][None],
                       FFN_ROW_TILE, jobs)
        if jobs:
            wts["a_in_t"] = new_x.pop()
        for n, (trunk, x) in enumerate(zip(trunks, new_x)):
            try:
                requests[n] = trunk.send(x)
            except StopIteration as done:
                results[n] = done.value
    (y_p, s_p, k_p, v_p), (y_s, s_s, k_s, v_s) = results
    return (y_p, y_s, s_p, s_s, k_p, v_p, k_s, v_s)
```
